```python
import jax, jax.numpy as jnp
from jax import lax
import numpy as np

D_MODEL = 2048
BATCH = 32
SEQ = 256
DEPTH = 4
DEC_BATCH = 8
DEC_SEQ = 4096
PAST_LEN = 512

GRID_W = 64
N_MIXERS = 3
N_A_LAYERS = (DEPTH + 2) // 3
N_B_LAYERS = (DEPTH + 1) // 3
N_C_LAYERS = DEPTH // 3
CHUNK = 128
A_WIDTH = D_MODEL
A_GROUPS = 8
A_GROUP_DIM = A_WIDTH // A_GROUPS
HEAD_DIM = 128
N_Q_HEADS = D_MODEL // HEAD_DIM
N_KV_HEADS = 4
Q_PER_KV = N_Q_HEADS // N_KV_HEADS
Q_BLOCK = 128
ROPE_THETA = 10000.0
ROPE_AXIS_DIM = HEAD_DIM // 2
POOL_WINDOWS = (2, 4, 8, 16)
POOL_GROUP_DIM = D_MODEL // len(POOL_WINDOWS)
FFN_HIDDEN = -(-8 * D_MODEL // (3 * 256)) * 256
N_MOD = 6
EPS = 1e-6

kernel_name = 'hybrid_diffusion_prefix_trunk_step'


def _rms(x, g):
    xf = x.astype(jnp.float32)
    y = xf * lax.rsqrt(jnp.mean(xf * xf, axis=-1, keepdims=True) + EPS)
    return (y * g.astype(jnp.float32)).astype(x.dtype)


def _swiglu(h, w_gu, w_down):
    gu = h @ w_gu
    g, u = gu[..., :FFN_HIDDEN], gu[..., FFN_HIDDEN:]
    return (jax.nn.silu(g) * u) @ w_down


def _chunk_gmlp(h, w_in, g_v, w_s, b_s, w_out):
    B, T, _ = h.shape
    a = h @ w_in
    u, v = a[..., :A_WIDTH], a[..., A_WIDTH:]
    v = _rms(v, g_v).reshape(B, T // CHUNK, CHUNK, A_GROUPS, A_GROUP_DIM)
    s = jnp.einsum('gpq,bnqge->bnpge', w_s, v) + b_s.T[None, None, :, :, None]
    return (u * s.reshape(B, T, A_WIDTH)) @ w_out


def _pool_mixer(h, w_pool, scale):
    B, T, D = h.shape
    hf = h.astype(jnp.float32)
    cs = jnp.concatenate([jnp.zeros((B, 1, D), jnp.float32), jnp.cumsum(hf, axis=1)], axis=1)
    t = jnp.arange(T)
    outs = []
    for j, w in enumerate(POOL_WINDOWS):
        lo = jnp.clip(t - w // 2, 0, T)
        hi = jnp.clip(t + w // 2, 0, T)
        sl = slice(j * POOL_GROUP_DIM, (j + 1) * POOL_GROUP_DIM)
        cnt = (hi - lo).astype(jnp.float32)[None, :, None]
        outs.append((cs[:, hi, sl] - cs[:, lo, sl]) / cnt - hf[:, :, sl])
    p = jnp.stack(outs, axis=2).astype(h.dtype)
    y = jnp.einsum('btgc,gcd->btgd', p, w_pool).reshape(B, T, D)
    return y * scale


def _rope_2d(x):
    T = x.shape[1]
    t = jnp.arange(T)
    n_rows = T // GRID_W
    row = jnp.minimum(t // GRID_W, n_rows - 1).astype(jnp.float32)
    col = (t % GRID_W).astype(jnp.float32)
    inv = jnp.power(ROPE_THETA, -jnp.arange(0, ROPE_AXIS_DIM, 2, dtype=jnp.float32) / ROPE_AXIS_DIM)
    ang = jnp.concatenate([row[:, None] * inv, col[:, None] * inv], axis=-1)
    ang = ang.reshape((1, T) + (1,) * (x.ndim - 3) + (HEAD_DIM // 2,))
    cos, sin = jnp.cos(ang), jnp.sin(ang)
    xr = x.astype(jnp.float32).reshape(x.shape[:-1] + (HEAD_DIM // 2, 2))
    x1, x2 = xr[..., 0], xr[..., 1]
    out = jnp.stack([x1 * cos - x2 * sin, x1 * sin + x2 * cos], axis=-1)
    return out.reshape(x.shape).astype(x.dtype)


def _qkv(h, w_qkv, q_norm, k_norm):
    B, T, _ = h.shape
    qkv = h @ w_qkv
    nq, nk = N_Q_HEADS * HEAD_DIM, N_KV_HEADS * HEAD_DIM
    q = qkv[..., :nq].reshape(B, T, N_KV_HEADS, Q_PER_KV, HEAD_DIM)
    k = qkv[..., nq:nq + nk].reshape(B, T, N_KV_HEADS, HEAD_DIM)
    v = qkv[..., nq + nk:].reshape(B, T, N_KV_HEADS, HEAD_DIM)
    return _rms(q, q_norm), _rms(k, k_norm), v


def _attend(q, k, v):
    B, T = q.shape[:2]
    nb = T // Q_BLOCK
    qb = q.reshape(B, nb, Q_BLOCK, N_KV_HEADS, Q_PER_KV, HEAD_DIM).swapaxes(0, 1)
    scale = HEAD_DIM ** -0.5

    def one(qi):
        s = jnp.einsum('bqhgd,bkhd->bhgqk', qi, k).astype(jnp.float32) * scale
        p = jax.nn.softmax(s, axis=-1).astype(v.dtype)
        return jnp.einsum('bhgqk,bkhd->bqhgd', p, v)

    o = lax.map(one, qb)
    return o.swapaxes(0, 1).reshape(B, T, N_Q_HEADS * HEAD_DIM)


def _trunk(x, cond, cache_k, cache_v, p):
    latent = cache_k is not None
    Bc = cond.shape[0]
    ia = ib = ic = 0
    new_k, new_v = [], []
    for l in range(DEPTH):
        mod = (jax.nn.silu(cond) @ p['w_mod'][l] + p['b_mod'][l]).reshape(Bc, 1, N_MOD, D_MODEL)
        h = _rms(x, p['norm_mix_pre'][l]) * (1 + mod[:, :, 1]) + mod[:, :, 0]
        kind = l % N_MIXERS
        if kind == 0:
            m = _chunk_gmlp(h, p['a_w_in'][ia], p['a_norm_v'][ia], p['a_w_s'][ia], p['a_b_s'][ia], p['a_w_out'][ia])
            ia += 1
        elif kind == 1:
            q, k, v = _qkv(h, p['b_w_qkv'][ib], p['b_q_norm'][ib], p['b_k_norm'][ib])
            if latent:
                q, k = _rope_2d(q), _rope_2d(k)
                k_all = jnp.concatenate([k, cache_k[:, ib].astype(k.dtype)], axis=1)
                v_all = jnp.concatenate([v, cache_v[:, ib].astype(v.dtype)], axis=1)
                o = _attend(q, k_all, v_all)
            else:
                new_k.append(k)
                new_v.append(v)
                o = _attend(q, k, v)
            m = o @ p['b_w_o'][ib]
            ib += 1
        else:
            m = _pool_mixer(h, p['c_w_pool'][ic], p['c_scale'][ic])
            ic += 1
        x = x + mod[:, :, 2] * _rms(m, p['norm_mix_post'][l])
        h = _rms(x, p['norm_ffn_pre'][l]) * (1 + mod[:, :, 4]) + mod[:, :, 3]
        f = _swiglu(h, p['f_w_gu'][l], p['f_w_down'][l])
        x = x + mod[:, :, 5] * _rms(f, p['norm_ffn_post'][l])
    if latent:
        return x, None, None
    return x, jnp.stack(new_k, axis=1), jnp.stack(new_v, axis=1)


def setup_inputs(seed: int = 0) -> dict:
    key = jax.random.key(seed)
    ks = jax.random.split(key, 26)
    n = lambda k, shape, s: jax.random.normal(k, shape, jnp.float32) * s
    D = D_MODEL
    qkv_out = (N_Q_HEADS + 2 * N_KV_HEADS) * HEAD_DIM
    kv_shape = (DEC_BATCH, N_B_LAYERS, PAST_LEN, N_KV_HEADS, HEAD_DIM)
    return {
        'x_prompt': n(ks[0], (BATCH, SEQ, D), 1.0),
        'x_sample': n(ks[1], (DEC_BATCH, DEC_SEQ, D), 1.0),
        'cache_k': n(ks[2], kv_shape, 1.0),
        'cache_v': n(ks[3], kv_shape, 1.0),
        'c': n(ks[4], (DEC_BATCH, D), 1.0),
        'c_ctx': n(ks[5], (D,), 1.0),
        'w_mod': n(ks[6], (DEPTH, D, N_MOD * D), D ** -0.5),
        'b_mod': n(ks[7], (DEPTH, N_MOD * D), 0.01),
        'norm_mix_pre': 1.0 + n(ks[8], (DEPTH, D), 0.02),
        'norm_mix_post': 1.0 + n(ks[9], (DEPTH, D), 0.02),
        'norm_ffn_pre': 1.0 + n(ks[10], (DEPTH, D), 0.02),
        'norm_ffn_post': 1.0 + n(ks[11], (DEPTH, D), 0.02),
        'a_w_in': n(ks[12], (N_A_LAYERS, D, 2 * A_WIDTH), D ** -0.5),
        'a_norm_v': 1.0 + n(ks[13], (N_A_LAYERS, A_WIDTH), 0.02),
        'a_w_s': n(ks[14], (N_A_LAYERS, A_GROUPS, CHUNK, CHUNK), CHUNK ** -0.5),
        'a_b_s': 1.0 + n(ks[15], (N_A_LAYERS, A_GROUPS, CHUNK), 0.02),
        'a_w_out': n(ks[16], (N_A_LAYERS, A_WIDTH, D), A_WIDTH ** -0.5),
        'b_w_qkv': n(ks[17], (N_B_LAYERS, D, qkv_out), D ** -0.5),
        'b_q_norm': 1.0 + n(ks[18], (N_B_LAYERS, HEAD_DIM), 0.02),
        'b_k_norm': 1.0 + n(ks[19], (N_B_LAYERS, HEAD_DIM), 0.02),
        'b_w_o': n(ks[20], (N_B_LAYERS, N_Q_HEADS * HEAD_DIM, D), (N_Q_HEADS * HEAD_DIM) ** -0.5),
        'c_w_pool': n(ks[21], (N_C_LAYERS, len(POOL_WINDOWS), POOL_GROUP_DIM, POOL_GROUP_DIM), POOL_GROUP_DIM ** -0.5),
        'c_scale': 1.0 + n(ks[22], (N_C_LAYERS, D), 0.1),
        'f_w_gu': n(ks[23], (DEPTH, D, 2 * FFN_HIDDEN), D ** -0.5),
        'f_w_down': n(ks[24], (DEPTH, FFN_HIDDEN, D), FFN_HIDDEN ** -0.5),
    }


def reference(x_prompt, x_sample, cache_k, cache_v, c, c_ctx, w_mod, b_mod,
              norm_mix_pre, norm_mix_post, norm_ffn_pre, norm_ffn_post,
              a_w_in, a_norm_v, a_w_s, a_b_s, a_w_out,
              b_w_qkv, b_q_norm, b_k_norm, b_w_o,
              c_w_pool, c_scale, f_w_gu, f_w_down):
    p = {
        'w_mod': w_mod, 'b_mod': b_mod,
        'norm_mix_pre': norm_mix_pre, 'norm_mix_post': norm_mix_post,
        'norm_ffn_pre': norm_ffn_pre, 'norm_ffn_post': norm_ffn_post,
        'a_w_in': a_w_in, 'a_norm_v': a_norm_v, 'a_w_s': a_w_s, 'a_b_s': a_b_s, 'a_w_out': a_w_out,
        'b_w_qkv': b_w_qkv, 'b_q_norm': b_q_norm, 'b_k_norm': b_k_norm, 'b_w_o': b_w_o,
        'c_w_pool': c_w_pool, 'c_scale': c_scale, 'f_w_gu': f_w_gu, 'f_w_down': f_w_down,
    }
    y_prompt, state_k, state_v = _trunk(x_prompt, c_ctx[None, :], None, None, p)
    y_sample, _, _ = _trunk(x_sample, c, cache_k, cache_v, p)
    return (y_prompt, y_sample, state_k, state_v)
```

```python
import collections
import functools

import jax
import jax.numpy as jnp
from jax import lax
from jax.experimental import pallas as pl
from jax.experimental.pallas import tpu as pltpu

F32 = jnp.float32
BF16 = jnp.bfloat16

EPS = 1e-6
N_MOD = 6
GRID_W = 64
ROPE_THETA = 10000.0
POOL_WINDOWS = (2, 4, 8, 16)
POOL_HALO = 8
MOD_ROWS = 16
NEG_BIG = -1e30

V7X_VMEM_BUDGET = 58 * 1024 * 1024
ROW_TILE = 512
FFN_HIDDEN_TILE = 512
COL_TILE = 512
ATTN_Q_TILE = 256
ATTN_K_TILE = 512
MOD_COL_TILE = 1024

Trunk = collections.namedtuple("Trunk", "batch seq mod_base rows_per_cond latent")


def _nbytes(shape, dtype):
    n = 1
    for s in shape:
        n *= s
    return n * jnp.dtype(dtype).itemsize


def _vmem_limit(pipelined, scratch=(), temps=()):
    total = 2 * sum(_nbytes(s, d) for s, d in pipelined)
    total += sum(_nbytes(s, d) for s, d in scratch)
    total += sum(_nbytes(s, d) for s, d in temps)
    return int(min(max(total + (4 << 20), 16 << 20), V7X_VMEM_BUDGET))


def _params(semantics, limit):
    return pltpu.CompilerParams(dimension_semantics=semantics, vmem_limit_bytes=limit)


def _rms(xf, g):
    ms = jnp.mean(xf * xf, axis=-1, keepdims=True)
    return (xf * lax.rsqrt(ms + EPS)) * g


def _norm_mod(xf, g, scale, shift):
    return _rms(xf, g) * (1.0 + scale) + shift


def _row_tile(trunk):
    rows = trunk.batch * trunk.seq
    tm = min(ROW_TILE, rows)
    assert rows % tm == 0 and (trunk.rows_per_cond % tm == 0)
    return tm


def _mod_spec(trunk, tm, m):
    def index(i, *_):
        return ((trunk.mod_base + (i * tm) // trunk.rows_per_cond) * N_MOD + m, 0, 0)
    return index


def _const2(i, *_):
    return (0, 0)


def _const3(i, *_):
    return (0, 0, 0)


def _mod_kernel(c_ref, w_ref, b_ref, o_ref):
    c = c_ref[...]
    s = (c * jax.nn.sigmoid(c)).astype(BF16)
    o_ref[0] = jnp.dot(s, w_ref[0].astype(BF16), preferred_element_type=F32) + b_ref[0]


def _mod_call(cond, w_mod, b_mod):
    depth, d, n = w_mod.shape
    tn = min(MOD_COL_TILE, n)
    assert n % tn == 0
    blocks = [((MOD_ROWS, d), F32), ((1, d, tn), F32), ((1, 1, tn), F32), ((1, MOD_ROWS, tn), F32)]
    return pl.pallas_call(
        _mod_kernel,
        grid=(depth, n // tn),
        in_specs=[
            pl.BlockSpec((MOD_ROWS, d), lambda l, j: (0, 0)),
            pl.BlockSpec((1, d, tn), lambda l, j: (l, 0, j)),
            pl.BlockSpec((1, 1, tn), lambda l, j: (l, 0, j)),
        ],
        out_specs=pl.BlockSpec((1, MOD_ROWS, tn), lambda l, j: (l, 0, j)),
        out_shape=jax.ShapeDtypeStruct((depth, MOD_ROWS, n), F32),
        compiler_params=_params(("parallel", "parallel"), _vmem_limit(blocks, temps=[((d, tn), BF16)])),
        name="mod",
    )(cond, w_mod, b_mod.reshape(depth, 1, n))


def _ffn_kernel(x_ref, shift_ref, scale_ref, gate_ref, gpre_ref, gpost_ref, wg_ref, wu_ref, wd_ref,
                o_ref, h_sc, acc_sc, *, n_chunks):
    c = pl.program_id(1)

    @pl.when(c == 0)
    def _():
        h = _norm_mod(x_ref[...], gpre_ref[...], scale_ref[0], shift_ref[0])
        h_sc[...] = h.astype(BF16)
        acc_sc[...] = jnp.zeros_like(acc_sc)

    h = h_sc[...]
    g = jnp.dot(h, wg_ref[...], preferred_element_type=F32)
    u = jnp.dot(h, wu_ref[...], preferred_element_type=F32)
    a = (g * jax.nn.sigmoid(g)) * u
    acc_sc[...] += jnp.dot(a.astype(BF16), wd_ref[...], preferred_element_type=F32)

    @pl.when(c == n_chunks - 1)
    def _():
        o_ref[...] = x_ref[...] + gate_ref[0] * _rms(acc_sc[...], gpost_ref[...])


def _ffn_call(x, mod_l, g_pre, g_post, w_gu, w_down, trunk):
    rows, d = x.shape
    hidden = w_down.shape[0]
    tm = _row_tile(trunk)
    th = min(FFN_HIDDEN_TILE, hidden)
    assert hidden % th == 0
    nc = hidden // th
    blocks = [((tm, d), F32), ((tm, d), F32), ((d, th), BF16), ((d, th), BF16), ((th, d), BF16)]
    scratch = [((tm, d), BF16), ((tm, d), F32)]
    temps = [((tm, th), F32)] * 4 + [((tm, d), F32)]
    return pl.pallas_call(
        functools.partial(_ffn_kernel, n_chunks=nc),
        grid=(rows // tm, nc),
        in_specs=[
            pl.BlockSpec((tm, d), lambda i, c: (i, 0)),
            pl.BlockSpec((1, 1, d), _mod_spec(trunk, tm, 3)),
            pl.BlockSpec((1, 1, d), _mod_spec(trunk, tm, 4)),
            pl.BlockSpec((1, 1, d), _mod_spec(trunk, tm, 5)),
            pl.BlockSpec((1, d), _const2),
            pl.BlockSpec((1, d), _const2),
            pl.BlockSpec((d, th), lambda i, c: (0, c)),
            pl.BlockSpec((d, th), lambda i, c: (0, nc + c)),
            pl.BlockSpec((th, d), lambda i, c: (c, 0)),
        ],
        out_specs=pl.BlockSpec((tm, d), lambda i, c: (i, 0)),
        out_shape=jax.ShapeDtypeStruct((rows, d), F32),
        scratch_shapes=[pltpu.VMEM(s, t) for s, t in scratch],
        compiler_params=_params(("parallel", "arbitrary"), _vmem_limit(blocks, scratch, temps)),
        name="ffn",
    )(x, mod_l, mod_l, mod_l, g_pre, g_post, w_gu, w_gu, w_down)


def _proj_kernel(y_ref, w_ref, x_ref, gate_ref, gpost_ref, o_ref):
    m = jnp.dot(y_ref[...], w_ref[...], preferred_element_type=F32)
    o_ref[...] = x_ref[...] + gate_ref[0] * _rms(m, gpost_ref[...])


def _proj_call(y, w, x, mod_l, g_post, trunk):
    rows, d = x.shape
    k = y.shape[1]
    tm = _row_tile(trunk)
    blocks = [((tm, k), BF16), ((k, d), BF16), ((tm, d), F32), ((tm, d), F32)]
    return pl.pallas_call(
        _proj_kernel,
        grid=(rows // tm,),
        in_specs=[
            pl.BlockSpec((tm, k), lambda i: (i, 0)),
            pl.BlockSpec((k, d), _const2),
            pl.BlockSpec((tm, d), lambda i: (i, 0)),
            pl.BlockSpec((1, 1, d), _mod_spec(trunk, tm, 2)),
            pl.BlockSpec((1, d), _const2),
        ],
        out_specs=pl.BlockSpec((tm, d), lambda i: (i, 0)),
        out_shape=jax.ShapeDtypeStruct((rows, d), F32),
        compiler_params=_params(("parallel",), _vmem_limit(blocks, temps=[((tm, d), F32)] * 2)),
        name="proj",
    )(y, w, x, mod_l, g_post)


def _mixa_kernel(x_ref, shift_ref, scale_ref, gpre_ref, win_ref, gv_ref, ws_ref, bs_ref,
                 y_ref, h_sc, a_sc, *, n_blocks, n_u_blocks, chunk, groups):
    j = pl.program_id(1)

    @pl.when(j == 0)
    def _():
        h = _norm_mod(x_ref[...], gpre_ref[...], scale_ref[0], shift_ref[0])
        h_sc[...] = h.astype(BF16)

    a_sc[j] = jnp.dot(h_sc[...], win_ref[...], preferred_element_type=F32)

    @pl.when(j == n_blocks - 1)
    def _():
        tm, tn = a_sc.shape[1], a_sc.shape[2]
        width = n_u_blocks * tn
        gdim = width // groups
        ss = jnp.zeros((tm, 1), F32)
        for b in range(n_u_blocks, n_blocks):
            vb = a_sc[b]
            ss = ss + jnp.sum(vb * vb, axis=-1, keepdims=True)
        inv = lax.rsqrt(ss / width + EPS)
        for g in range(groups):
            col = g * gdim
            b, off = col // tn, col % tn
            gv = gv_ref[:, col:col + gdim]
            for c in range(tm // chunk):
                r = c * chunk
                v = (a_sc[n_u_blocks + b, r:r + chunk, off:off + gdim] * inv[r:r + chunk]) * gv
                s = jnp.dot(ws_ref[g], v.astype(BF16), preferred_element_type=F32) + bs_ref[g]
                u = a_sc[b, r:r + chunk, off:off + gdim]
                y_ref[r:r + chunk, col:col + gdim] = (u * s).astype(y_ref.dtype)


def _mixa_call(x, mod_l, g_pre, w_in, g_v, w_s, b_s, trunk):
    rows, d = x.shape
    width = w_in.shape[1] // 2
    groups, chunk = w_s.shape[0], w_s.shape[1]
    tm = _row_tile(trunk)
    tn = min(COL_TILE, width)
    gdim = width // groups
    assert width % tn == 0 and tn % gdim == 0 and tm % chunk == 0 and trunk.seq % chunk == 0
    nb = 2 * width // tn
    blocks = [((tm, d), F32), ((d, tn), BF16), ((tm, width), BF16), ((groups, chunk, chunk), BF16),
              ((groups, chunk, 128), F32)]
    scratch = [((tm, d), BF16), ((nb, tm, tn), F32)]
    return pl.pallas_call(
        functools.partial(_mixa_kernel, n_blocks=nb, n_u_blocks=nb // 2, chunk=chunk, groups=groups),
        grid=(rows // tm, nb),
        in_specs=[
            pl.BlockSpec((tm, d), lambda i, j: (i, 0)),
            pl.BlockSpec((1, 1, d), _mod_spec(trunk, tm, 0)),
            pl.BlockSpec((1, 1, d), _mod_spec(trunk, tm, 1)),
            pl.BlockSpec((1, d), _const2),
            pl.BlockSpec((d, tn), lambda i, j: (0, j)),
            pl.BlockSpec((1, width), _const2),
            pl.BlockSpec((groups, chunk, chunk), _const3),
            pl.BlockSpec((groups, chunk, 1), _const3),
        ],
        out_specs=pl.BlockSpec((tm, width), lambda i, j: (i, 0)),
        out_shape=jax.ShapeDtypeStruct((rows, width), BF16),
        scratch_shapes=[pltpu.VMEM(s, t) for s, t in scratch],
        compiler_params=_params(("parallel", "arbitrary"),
                                _vmem_limit(blocks, scratch, temps=[((tm, d), F32), ((tm, tn), F32)])),
        name="mixa",
    )(x, mod_l, mod_l, g_pre, w_in, g_v, w_s, b_s.reshape(groups, chunk, 1))


def _rope_tables(seq, head_dim):
    axis_dim = head_dim // 2
    t = jnp.arange(seq)
    n_rows = seq // GRID_W
    row = jnp.minimum(t // GRID_W, n_rows - 1).astype(F32)
    col = (t % GRID_W).astype(F32)
    inv = jnp.power(ROPE_THETA, -jnp.arange(0, axis_dim, 2, dtype=F32) / axis_dim)
    ang = jnp.concatenate([row[:, None] * inv, col[:, None] * inv], axis=-1)
    cos, sin = jnp.cos(ang), jnp.sin(ang)
    zero = jnp.zeros_like(sin)
    cos2 = jnp.stack([cos, cos], axis=-1).reshape(seq, head_dim)
    sin_a = jnp.stack([-sin, zero], axis=-1).reshape(seq, head_dim)
    sin_b = jnp.stack([zero, sin], axis=-1).reshape(seq, head_dim)
    return cos2, sin_a, sin_b


def _qkv_kernel(*refs, n_q_blocks, head_dim, latent, q_scale):
    if latent:
        (x_ref, shift_ref, scale_ref, gpre_ref, w_ref, qg_ref, kg_ref, cos_ref, sa_ref, sb_ref,
         q_ref, k_ref, v_ref, h_sc) = refs
    else:
        (x_ref, shift_ref, scale_ref, gpre_ref, w_ref, qg_ref, kg_ref, q_ref, k_ref, v_ref, h_sc) = refs
    j = pl.program_id(1)

    @pl.when(j == 0)
    def _():
        h = _norm_mod(x_ref[...], gpre_ref[...], scale_ref[0], shift_ref[0])
        h_sc[...] = h.astype(BF16)

    a = jnp.dot(h_sc[...], w_ref[...], preferred_element_type=F32)
    n_heads = a.shape[1] // head_dim

    def head(hh, gain):
        blk = _rms(a[:, hh * head_dim:(hh + 1) * head_dim], gain)
        if latent:
            nxt = pltpu.roll(blk, head_dim - 1, 1)
            prv = pltpu.roll(blk, 1, 1)
            blk = blk * cos_ref[...] + nxt * sa_ref[...] + prv * sb_ref[...]
        return blk

    @pl.when(j < n_q_blocks)
    def _():
        for hh in range(n_heads):
            q_ref[:, hh * head_dim:(hh + 1) * head_dim] = (head(hh, qg_ref[...]) * q_scale).astype(q_ref.dtype)

    @pl.when(j == n_q_blocks)
    def _():
        for hh in range(n_heads):
            k_ref[:, hh * head_dim:(hh + 1) * head_dim] = head(hh, kg_ref[...]).astype(k_ref.dtype)

    @pl.when(j == n_q_blocks + 1)
    def _():
        v_ref[...] = a.astype(v_ref.dtype)


def _qkv_call(x, mod_l, g_pre, w_qkv, q_gain, k_gain, n_kv, trunk, kv_dtype):
    rows, d = x.shape
    hd = q_gain.shape[-1]
    tn = n_kv * hd
    nq = w_qkv.shape[1] - 2 * tn
    assert nq % tn == 0
    nqb = nq // tn
    tm = _row_tile(trunk)
    in_specs = [
        pl.BlockSpec((tm, d), lambda i, j: (i, 0)),
        pl.BlockSpec((1, 1, d), _mod_spec(trunk, tm, 0)),
        pl.BlockSpec((1, 1, d), _mod_spec(trunk, tm, 1)),
        pl.BlockSpec((1, d), _const2),
        pl.BlockSpec((d, tn), lambda i, j: (0, j)),
        pl.BlockSpec((1, hd), _const2),
        pl.BlockSpec((1, hd), _const2),
    ]
    args = [x, mod_l, mod_l, g_pre, w_qkv, q_gain, k_gain]
    if trunk.latent:
        assert trunk.seq % tm == 0 and trunk.seq % GRID_W == 0
        tiles_per_seq = trunk.seq // tm
        in_specs += [pl.BlockSpec((tm, hd), lambda i, j: (i % tiles_per_seq, 0))] * 3
        args += list(_rope_tables(trunk.seq, hd))
    blocks = [((tm, d), F32), ((d, tn), BF16), ((tm, tn), BF16), ((tm, tn), kv_dtype), ((tm, tn), kv_dtype),
              ((tm, hd), F32), ((tm, hd), F32), ((tm, hd), F32)]
    scratch = [((tm, d), BF16)]
    return pl.pallas_call(
        functools.partial(_qkv_kernel, n_q_blocks=nqb, head_dim=hd, latent=trunk.latent, q_scale=hd ** -0.5),
        grid=(rows // tm, nqb + 2),
        in_specs=in_specs,
        out_specs=[
            pl.BlockSpec((tm, tn), lambda i, j: (i, jnp.minimum(j, nqb - 1))),
            pl.BlockSpec((tm, tn), lambda i, j: (i, 0)),
            pl.BlockSpec((tm, tn), lambda i, j: (i, 0)),
        ],
        out_shape=[
            jax.ShapeDtypeStruct((rows, nq), BF16),
            jax.ShapeDtypeStruct((rows, tn), kv_dtype),
            jax.ShapeDtypeStruct((rows, tn), kv_dtype),
        ],
        scratch_shapes=[pltpu.VMEM(s, t) for s, t in scratch],
        compiler_params=_params(("parallel", "arbitrary"),
                                _vmem_limit(blocks, scratch, temps=[((tm, d), F32), ((tm, tn), F32)])),
        name="qkv",
    )(*args)


def _attn_kernel(q_ref, k_ref, v_ref, o_ref, q_sc, m_sc, l_sc, acc_sc, *, n_k_blocks, q_per_kv, head_dim):
    ki = pl.program_id(3)
    tq = q_ref.shape[1]

    @pl.when(ki == 0)
    def _():
        for g in range(q_per_kv):
            q_sc[g * tq:(g + 1) * tq, :] = q_ref[0, :, g * head_dim:(g + 1) * head_dim]
        m_sc[...] = jnp.full_like(m_sc, NEG_BIG)
        l_sc[...] = jnp.zeros_like(l_sc)
        acc_sc[...] = jnp.zeros_like(acc_sc)

    k = k_ref[0].astype(BF16)
    v = v_ref[0].astype(BF16)
    s = lax.dot_general(q_sc[...], k, (((1,), (1,)), ((), ())), preferred_element_type=F32)
    m_prev = m_sc[...]
    m_new = jnp.maximum(m_prev, jnp.max(s, axis=-1, keepdims=True))
    alpha = jnp.exp(m_prev - m_new)
    p = jnp.exp(s - m_new)
    l_sc[...] = alpha * l_sc[...] + jnp.sum(p, axis=-1, keepdims=True)
    acc_sc[...] = alpha * acc_sc[...] + jnp.dot(p.astype(BF16), v, preferred_element_type=F32)
    m_sc[...] = m_new

    @pl.when(ki == n_k_blocks - 1)
    def _():
        out = acc_sc[...] / l_sc[...]
        for g in range(q_per_kv):
            o_ref[0, :, g * head_dim:(g + 1) * head_dim] = out[g * tq:(g + 1) * tq].astype(o_ref.dtype)


def _attn_call(q, k, v, n_kv):
    b, t, nq = q.shape
    s = k.shape[1]
    hd = k.shape[2] // n_kv
    gw = nq // n_kv
    tq = min(ATTN_Q_TILE, t)
    tk = min(ATTN_K_TILE, s)
    assert t % tq == 0 and s % tk == 0
    nk = s // tk
    rows = (gw // hd) * tq
    blocks = [((1, tq, gw), BF16), ((1, tk, hd), k.dtype), ((1, tk, hd), v.dtype), ((1, tq, gw), BF16)]
    scratch = [((rows, hd), BF16), ((rows, 128), F32), ((rows, 128), F32), ((rows, hd), F32)]
    return pl.pallas_call(
        functools.partial(_attn_kernel, n_k_blocks=nk, q_per_kv=gw // hd, head_dim=hd),
        grid=(b, n_kv, t // tq, nk),
        in_specs=[
            pl.BlockSpec((1, tq, gw), lambda bi, h, qi, ki: (bi, qi, h)),
            pl.BlockSpec((1, tk, hd), lambda bi, h, qi, ki: (bi, ki, h)),
            pl.BlockSpec((1, tk, hd), lambda bi, h, qi, ki: (bi, ki, h)),
        ],
        out_specs=pl.BlockSpec((1, tq, gw), lambda bi, h, qi, ki: (bi, qi, h)),
        out_shape=jax.ShapeDtypeStruct((b, t, nq), BF16),
        scratch_shapes=[
            pltpu.VMEM((rows, hd), BF16),
            pltpu.VMEM((rows, 1), F32),
            pltpu.VMEM((rows, 1), F32),
            pltpu.VMEM((rows, hd), F32),
        ],
        compiler_params=_params(("parallel", "parallel", "parallel", "arbitrary"),
                                _vmem_limit(blocks, scratch, temps=[((rows, tk), F32)] * 3)),
        name="attn",
    )(q, k, v)


def _pool_kernel(x_ref, xp_ref, xn_ref, shift_ref, scale_ref, gate_ref, gpre_ref, gpost_ref, w_ref, cs_ref,
                 o_ref, h_sc, m_sc, *, seq):
    tm, d = x_ref.shape
    gdim = d // len(POOL_WINDOWS)
    gpre, scale, shift = gpre_ref[...], scale_ref[0], shift_ref[0]
    h_sc[0:POOL_HALO, :] = _norm_mod(xp_ref[...], gpre, scale, shift)
    h_sc[POOL_HALO:POOL_HALO + tm, :] = _norm_mod(x_ref[...], gpre, scale, shift)
    h_sc[POOL_HALO + tm:, :] = _norm_mod(xn_ref[...], gpre, scale, shift)

    row = pl.program_id(0) * tm + lax.broadcasted_iota(jnp.int32, (tm, 1), 0)
    pos = lax.rem(row, seq)
    for j, w in enumerate(POOL_WINDOWS):
        half = w // 2
        cols = pl.ds(j * gdim, gdim)
        acc = jnp.zeros((tm, gdim), F32)
        for off in range(-half, half):
            valid = jnp.logical_and(pos + off >= 0, pos + off < seq)
            acc = acc + jnp.where(valid, h_sc[pl.ds(POOL_HALO + off, tm), cols], 0.0)
        cnt = (jnp.minimum(pos + half, seq) - jnp.maximum(pos - half, 0)).astype(F32)
        p = acc / cnt - h_sc[pl.ds(POOL_HALO, tm), cols]
        y = jnp.dot(p.astype(BF16), w_ref[j], preferred_element_type=F32)
        m_sc[:, cols] = y * cs_ref[:, cols]
    o_ref[...] = x_ref[...] + gate_ref[0] * _rms(m_sc[...], gpost_ref[...])


def _pool_call(x, mod_l, g_pre, g_post, w_pool, c_scale, trunk):
    rows, d = x.shape
    tm = _row_tile(trunk)
    assert tm % POOL_HALO == 0 and max(POOL_WINDOWS) // 2 <= POOL_HALO and (tm % trunk.seq == 0 or trunk.seq % tm == 0)
    groups, gdim = w_pool.shape[0], w_pool.shape[1]
    halo_per_tile = tm // POOL_HALO
    last_halo = rows // POOL_HALO - 1
    blocks = [((tm, d), F32), ((tm, d), F32), ((groups, gdim, gdim), BF16)]
    scratch = [((tm + 2 * POOL_HALO, d), F32), ((tm, d), F32)]
    return pl.pallas_call(
        functools.partial(_pool_kernel, seq=trunk.seq),
        grid=(rows // tm,),
        in_specs=[
            pl.BlockSpec((tm, d), lambda i: (i, 0)),
            pl.BlockSpec((POOL_HALO, d), lambda i: (jnp.maximum(i * halo_per_tile - 1, 0), 0)),
            pl.BlockSpec((POOL_HALO, d), lambda i: (jnp.minimum((i + 1) * halo_per_tile, last_halo), 0)),
            pl.BlockSpec((1, 1, d), _mod_spec(trunk, tm, 0)),
            pl.BlockSpec((1, 1, d), _mod_spec(trunk, tm, 1)),
            pl.BlockSpec((1, 1, d), _mod_spec(trunk, tm, 2)),
            pl.BlockSpec((1, d), _const2),
            pl.BlockSpec((1, d), _const2),
            pl.BlockSpec((groups, gdim, gdim), _const3),
            pl.BlockSpec((1, d), _const2),
        ],
        out_specs=pl.BlockSpec((tm, d), lambda i: (i, 0)),
        out_shape=jax.ShapeDtypeStruct((rows, d), F32),
        scratch_shapes=[pltpu.VMEM(s, t) for s, t in scratch],
        compiler_params=_params(("parallel",), _vmem_limit(blocks, scratch, temps=[((tm, d), F32)] * 2)),
        name="pool",
    )(x, x, x, mod_l, mod_l, mod_l, g_pre, g_post, w_pool, c_scale)


def _trunk(x3, trunk, mod, cache_kv, p):
    b, t, d = x3.shape
    x = x3.reshape(b * t, d)
    depth = p["w_mod"].shape[0]
    n_kv = p["n_kv"]
    ia = ib = ic = 0
    new_k, new_v = [], []
    row1 = lambda a: a.reshape(1, -1)
    for l in range(depth):
        mod_l = mod[l]
        g_pre, g_post = row1(p["norm_mix_pre"][l]), row1(p["norm_mix_post"][l])
        kind = l % 3
        if kind == 0:
            y = _mixa_call(x, mod_l, g_pre, p["a_w_in"][ia], row1(p["a_norm_v"][ia]), p["a_w_s"][ia],
                           p["a_b_s"][ia], trunk)
            x = _proj_call(y, p["a_w_out"][ia], x, mod_l, g_post, trunk)
            ia += 1
        elif kind == 1:
            kv_dtype = BF16 if trunk.latent else F32
            q, k, v = _qkv_call(x, mod_l, g_pre, p["b_w_qkv"][ib], row1(p["b_q_norm"][ib]),
                                row1(p["b_k_norm"][ib]), n_kv, trunk, kv_dtype)
            k3, v3 = k.reshape(b, t, -1), v.reshape(b, t, -1)
            if trunk.latent:
                ck, cv = cache_kv
                past = ck.shape[2]
                k3 = jnp.concatenate([k3, ck[:, ib].reshape(b, past, -1).astype(BF16)], axis=1)
                v3 = jnp.concatenate([v3, cv[:, ib].reshape(b, past, -1).astype(BF16)], axis=1)
            else:
                new_k.append(k3.reshape(b, t, n_kv, -1))
                new_v.append(v3.reshape(b, t, n_kv, -1))
            o = _attn_call(q.reshape(b, t, -1), k3, v3, n_kv)
            x = _proj_call(o.reshape(b * t, -1), p["b_w_o"][ib], x, mod_l, g_post, trunk)
            ib += 1
        else:
            x = _pool_call(x, mod_l, g_pre, g_post, p["c_w_pool"][ic], row1(p["c_scale"][ic]), trunk)
            ic += 1
        x = _ffn_call(x, mod_l, row1(p["norm_ffn_pre"][l]), row1(p["norm_ffn_post"][l]),
                      p["f_w_gu"][l], p["f_w_down"][l], trunk)
    return x.reshape(b, t, d), new_k, new_v


def kernel(x_prompt, x_sample, cache_k, cache_v, c, c_ctx, w_mod, b_mod, norm_mix_pre, norm_mix_post, norm_ffn_pre, norm_ffn_post, a_w_in, a_norm_v, a_w_s, a_b_s, a_w_out, b_w_qkv, b_q_norm, b_k_norm, b_w_o, c_w_pool, c_scale, f_w_gu, f_w_down):
    batch, seq, d = x_prompt.shape
    dec_batch, dec_seq, _ = x_sample.shape
    depth = w_mod.shape[0]
    assert 1 + dec_batch <= MOD_ROWS

    cond = jnp.concatenate([c_ctx[None, :], c, jnp.zeros((MOD_ROWS - 1 - dec_batch, d), F32)], axis=0)
    mod = _mod_call(cond, w_mod, b_mod).reshape(depth, MOD_ROWS * N_MOD, 1, d)

    p = {
        "w_mod": w_mod, "n_kv": cache_k.shape[3],
        "norm_mix_pre": norm_mix_pre, "norm_mix_post": norm_mix_post,
        "norm_ffn_pre": norm_ffn_pre, "norm_ffn_post": norm_ffn_post,
        "a_w_in": a_w_in.astype(BF16), "a_norm_v": a_norm_v, "a_w_s": a_w_s.astype(BF16), "a_b_s": a_b_s,
        "a_w_out": a_w_out.astype(BF16),
        "b_w_qkv": b_w_qkv.astype(BF16), "b_q_norm": b_q_norm, "b_k_norm": b_k_norm, "b_w_o": b_w_o.astype(BF16),
        "c_w_pool": c_w_pool.astype(BF16), "c_scale": c_scale,
        "f_w_gu": f_w_gu.astype(BF16), "f_w_down": f_w_down.astype(BF16),
    }
    ctx = Trunk(batch=batch, seq=seq, mod_base=0, rows_per_cond=batch * seq, latent=False)
    lat = Trunk(batch=dec_batch, seq=dec_seq, mod_base=1, rows_per_cond=dec_seq, latent=True)
    y_prompt, new_k, new_v = _trunk(x_prompt, ctx, mod, None, p)
    y_sample, _, _ = _trunk(x_sample, lat, mod, (cache_k, cache_v), p)
    return (y_prompt, y_sample, jnp.stack(new_k, axis=1), jnp.stack(new_v, axis=1))
```

```python
import collections
import functools

import jax
import jax.numpy as jnp
from jax import lax
from jax.experimental import pallas as pl
from jax.experimental.pallas import tpu as pltpu

F32 = jnp.float32
BF16 = jnp.bfloat16

EPS = 1e-6
N_MOD = 6
GRID_W = 64
ROPE_THETA = 10000.0
POOL_WINDOWS = (2, 4, 8, 16)
POOL_HALO = 8
MOD_ROWS = 16
LOG2_E = 1.4426950408889634
V7X_LANES = 128

V7X_VMEM_BUDGET = 58 * 1024 * 1024
ROW_TILE = 512
FFN_HIDDEN_TILE = 512
COL_TILE = 512
ATTN_Q_TILE = 256
ATTN_MAX_KEYS = 4608
ATTN_HEAD_GROUPS = 2
MOD_COL_TILE = 1024

Trunk = collections.namedtuple("Trunk", "batch seq mod_base rows_per_cond latent")


def _nbytes(shape, dtype):
    n = 1
    for s in shape:
        n *= s
    return n * jnp.dtype(dtype).itemsize


def _vmem_limit(pipelined, scratch=(), temps=()):
    total = 2 * sum(_nbytes(s, d) for s, d in pipelined)
    total += sum(_nbytes(s, d) for s, d in scratch)
    total += sum(_nbytes(s, d) for s, d in temps)
    return int(min(max(total + (4 << 20), 16 << 20), V7X_VMEM_BUDGET))


def _params(semantics, limit):
    return pltpu.CompilerParams(dimension_semantics=semantics, vmem_limit_bytes=limit)


def _rms(xf, g):
    ms = jnp.mean(xf * xf, axis=-1, keepdims=True)
    return (xf * lax.rsqrt(ms + EPS)) * g


def _norm_mod(xf, g, scale, shift):
    return _rms(xf, g) * (1.0 + scale) + shift


def _row_tile(trunk):
    rows = trunk.batch * trunk.seq
    tm = min(ROW_TILE, rows)
    assert rows % tm == 0 and (trunk.rows_per_cond % tm == 0)
    return tm


def _mod_spec(trunk, tm, m):
    def index(i, *_):
        return ((trunk.mod_base + (i * tm) // trunk.rows_per_cond) * N_MOD + m, 0, 0)
    return index


def _const2(i, *_):
    return (0, 0)


def _const3(i, *_):
    return (0, 0, 0)


def _mod_kernel(c_ref, w_ref, b_ref, o_ref):
    c = c_ref[...]
    s = (c * jax.nn.sigmoid(c)).astype(BF16)
    o_ref[0] = jnp.dot(s, w_ref[0].astype(BF16), preferred_element_type=F32) + b_ref[0]


def _mod_call(cond, w_mod, b_mod):
    depth, d, n = w_mod.shape
    tn = min(MOD_COL_TILE, n)
    assert n % tn == 0
    blocks = [((MOD_ROWS, d), F32), ((1, d, tn), F32), ((1, 1, tn), F32), ((1, MOD_ROWS, tn), F32)]
    return pl.pallas_call(
        _mod_kernel,
        grid=(depth, n // tn),
        in_specs=[
            pl.BlockSpec((MOD_ROWS, d), lambda l, j: (0, 0)),
            pl.BlockSpec((1, d, tn), lambda l, j: (l, 0, j)),
            pl.BlockSpec((1, 1, tn), lambda l, j: (l, 0, j)),
        ],
        out_specs=pl.BlockSpec((1, MOD_ROWS, tn), lambda l, j: (l, 0, j)),
        out_shape=jax.ShapeDtypeStruct((depth, MOD_ROWS, n), F32),
        compiler_params=_params(("parallel", "parallel"), _vmem_limit(blocks, temps=[((d, tn), BF16)])),
        name="mod",
    )(cond, w_mod, b_mod.reshape(depth, 1, n))


def _ffn_kernel(x_ref, shift_ref, scale_ref, gate_ref, gpre_ref, gpost_ref, wg_ref, wu_ref, wd_ref,
                o_ref, h_sc, acc_sc, *, n_chunks):
    c = pl.program_id(1)

    @pl.when(c == 0)
    def _():
        h = _norm_mod(x_ref[...], gpre_ref[...], scale_ref[0], shift_ref[0])
        h_sc[...] = h.astype(BF16)
        acc_sc[...] = jnp.zeros_like(acc_sc)

    h = h_sc[...]
    g = jnp.dot(h, wg_ref[...], preferred_element_type=F32)
    u = jnp.dot(h, wu_ref[...], preferred_element_type=F32)
    a = (g * jax.nn.sigmoid(g)) * u
    acc_sc[...] += jnp.dot(a.astype(BF16), wd_ref[...], preferred_element_type=F32)

    @pl.when(c == n_chunks - 1)
    def _():
        o_ref[...] = x_ref[...] + gate_ref[0] * _rms(acc_sc[...], gpost_ref[...])


def _ffn_call(x, mod_l, g_pre, g_post, w_gu, w_down, trunk):
    rows, d = x.shape
    hidden = w_down.shape[0]
    tm = _row_tile(trunk)
    th = min(FFN_HIDDEN_TILE, hidden)
    assert hidden % th == 0
    nc = hidden // th
    blocks = [((tm, d), F32), ((tm, d), F32), ((d, th), BF16), ((d, th), BF16), ((th, d), BF16)]
    scratch = [((tm, d), BF16), ((tm, d), F32)]
    temps = [((tm, th), F32)] * 4 + [((tm, d), F32)]
    return pl.pallas_call(
        functools.partial(_ffn_kernel, n_chunks=nc),
        grid=(rows // tm, nc),
        in_specs=[
            pl.BlockSpec((tm, d), lambda i, c: (i, 0)),
            pl.BlockSpec((1, 1, d), _mod_spec(trunk, tm, 3)),
            pl.BlockSpec((1, 1, d), _mod_spec(trunk, tm, 4)),
            pl.BlockSpec((1, 1, d), _mod_spec(trunk, tm, 5)),
            pl.BlockSpec((1, d), _const2),
            pl.BlockSpec((1, d), _const2),
            pl.BlockSpec((d, th), lambda i, c: (0, c)),
            pl.BlockSpec((d, th), lambda i, c: (0, nc + c)),
            pl.BlockSpec((th, d), lambda i, c: (c, 0)),
        ],
        out_specs=pl.BlockSpec((tm, d), lambda i, c: (i, 0)),
        out_shape=jax.ShapeDtypeStruct((rows, d), F32),
        scratch_shapes=[pltpu.VMEM(s, t) for s, t in scratch],
        compiler_params=_params(("parallel", "arbitrary"), _vmem_limit(blocks, scratch, temps)),
        name="ffn",
    )(x, mod_l, mod_l, mod_l, g_pre, g_post, w_gu, w_gu, w_down)


def _proj_kernel(y_ref, w_ref, x_ref, gate_ref, gpost_ref, o_ref):
    m = jnp.dot(y_ref[...], w_ref[...], preferred_element_type=F32)
    o_ref[...] = x_ref[...] + gate_ref[0] * _rms(m, gpost_ref[...])


def _proj_call(y, w, x, mod_l, g_post, trunk):
    rows, d = x.shape
    k = y.shape[1]
    tm = _row_tile(trunk)
    blocks = [((tm, k), BF16), ((k, d), BF16), ((tm, d), F32), ((tm, d), F32)]
    return pl.pallas_call(
        _proj_kernel,
        grid=(rows // tm,),
        in_specs=[
            pl.BlockSpec((tm, k), lambda i: (i, 0)),
            pl.BlockSpec((k, d), _const2),
            pl.BlockSpec((tm, d), lambda i: (i, 0)),
            pl.BlockSpec((1, 1, d), _mod_spec(trunk, tm, 2)),
            pl.BlockSpec((1, d), _const2),
        ],
        out_specs=pl.BlockSpec((tm, d), lambda i: (i, 0)),
        out_shape=jax.ShapeDtypeStruct((rows, d), F32),
        compiler_params=_params(("parallel",), _vmem_limit(blocks, temps=[((tm, d), F32)] * 2)),
        name="proj",
    )(y, w, x, mod_l, g_post)


def _mixa_kernel(x_ref, shift_ref, scale_ref, gpre_ref, win_ref, gv_ref, ws_ref, bs_ref,
                 y_ref, h_sc, a_sc, *, n_blocks, n_u_blocks, chunk, groups):
    j = pl.program_id(1)

    @pl.when(j == 0)
    def _():
        h = _norm_mod(x_ref[...], gpre_ref[...], scale_ref[0], shift_ref[0])
        h_sc[...] = h.astype(BF16)

    a_sc[j] = jnp.dot(h_sc[...], win_ref[...], preferred_element_type=F32)

    @pl.when(j == n_blocks - 1)
    def _():
        tm, tn = a_sc.shape[1], a_sc.shape[2]
        width = n_u_blocks * tn
        gdim = width // groups
        ss = jnp.zeros((tm, 1), F32)
        for b in range(n_u_blocks, n_blocks):
            vb = a_sc[b]
            ss = ss + jnp.sum(vb * vb, axis=-1, keepdims=True)
        inv = lax.rsqrt(ss / width + EPS)
        for g in range(groups):
            col = g * gdim
            b, off = col // tn, col % tn
            gv = gv_ref[:, col:col + gdim]
            for c in range(tm // chunk):
                r = c * chunk
                v = (a_sc[n_u_blocks + b, r:r + chunk, off:off + gdim] * inv[r:r + chunk]) * gv
                s = jnp.dot(ws_ref[g], v.astype(BF16), preferred_element_type=F32) + bs_ref[g]
                u = a_sc[b, r:r + chunk, off:off + gdim]
                y_ref[r:r + chunk, col:col + gdim] = (u * s).astype(y_ref.dtype)


def _mixa_call(x, mod_l, g_pre, w_in, g_v, w_s, b_s, trunk):
    rows, d = x.shape
    width = w_in.shape[1] // 2
    groups, chunk = w_s.shape[0], w_s.shape[1]
    tm = _row_tile(trunk)
    tn = min(COL_TILE, width)
    gdim = width // groups
    assert width % tn == 0 and tn % gdim == 0 and tm % chunk == 0 and trunk.seq % chunk == 0
    nb = 2 * width // tn
    blocks = [((tm, d), F32), ((d, tn), BF16), ((tm, width), BF16), ((groups, chunk, chunk), BF16),
              ((groups, chunk, 128), F32)]
    scratch = [((tm, d), BF16), ((nb, tm, tn), F32)]
    return pl.pallas_call(
        functools.partial(_mixa_kernel, n_blocks=nb, n_u_blocks=nb // 2, chunk=chunk, groups=groups),
        grid=(rows // tm, nb),
        in_specs=[
            pl.BlockSpec((tm, d), lambda i, j: (i, 0)),
            pl.BlockSpec((1, 1, d), _mod_spec(trunk, tm, 0)),
            pl.BlockSpec((1, 1, d), _mod_spec(trunk, tm, 1)),
            pl.BlockSpec((1, d), _const2),
            pl.BlockSpec((d, tn), lambda i, j: (0, j)),
            pl.BlockSpec((1, width), _const2),
            pl.BlockSpec((groups, chunk, chunk), _const3),
            pl.BlockSpec((groups, chunk, 1), _const3),
        ],
        out_specs=pl.BlockSpec((tm, width), lambda i, j: (i, 0)),
        out_shape=jax.ShapeDtypeStruct((rows, width), BF16),
        scratch_shapes=[pltpu.VMEM(s, t) for s, t in scratch],
        compiler_params=_params(("parallel", "arbitrary"),
                                _vmem_limit(blocks, scratch, temps=[((tm, d), F32), ((tm, tn), F32)])),
        name="mixa",
    )(x, mod_l, mod_l, g_pre, w_in, g_v, w_s, b_s.reshape(groups, chunk, 1))


def _rope_tables(seq, head_dim):
    axis_dim = head_dim // 2
    t = jnp.arange(seq)
    n_rows = seq // GRID_W
    row = jnp.minimum(t // GRID_W, n_rows - 1).astype(F32)
    col = (t % GRID_W).astype(F32)
    inv = jnp.power(ROPE_THETA, -jnp.arange(0, axis_dim, 2, dtype=F32) / axis_dim)
    ang = jnp.concatenate([row[:, None] * inv, col[:, None] * inv], axis=-1)
    cos, sin = jnp.cos(ang), jnp.sin(ang)
    zero = jnp.zeros_like(sin)
    cos2 = jnp.stack([cos, cos], axis=-1).reshape(seq, head_dim)
    sin_a = jnp.stack([-sin, zero], axis=-1).reshape(seq, head_dim)
    sin_b = jnp.stack([zero, sin], axis=-1).reshape(seq, head_dim)
    return cos2, sin_a, sin_b


def _qkv_kernel(*refs, n_q_blocks, head_dim, latent, q_scale):
    if latent:
        (x_ref, shift_ref, scale_ref, gpre_ref, w_ref, qg_ref, kg_ref, cos_ref, sa_ref, sb_ref,
         q_ref, k_ref, v_ref, h_sc) = refs
    else:
        (x_ref, shift_ref, scale_ref, gpre_ref, w_ref, qg_ref, kg_ref, q_ref, k_ref, v_ref, h_sc) = refs
    j = pl.program_id(1)

    @pl.when(j == 0)
    def _():
        h = _norm_mod(x_ref[...], gpre_ref[...], scale_ref[0], shift_ref[0])
        h_sc[...] = h.astype(BF16)

    a = jnp.dot(h_sc[...], w_ref[...], preferred_element_type=F32)
    n_heads = a.shape[1] // head_dim

    def head(hh, gain):
        blk = _rms(a[:, hh * head_dim:(hh + 1) * head_dim], gain)
        if latent:
            nxt = pltpu.roll(blk, head_dim - 1, 1)
            prv = pltpu.roll(blk, 1, 1)
            blk = blk * cos_ref[...] + nxt * sa_ref[...] + prv * sb_ref[...]
        return blk

    @pl.when(j < n_q_blocks)
    def _():
        for hh in range(n_heads):
            q_ref[:, hh * head_dim:(hh + 1) * head_dim] = (head(hh, qg_ref[...]) * q_scale).astype(q_ref.dtype)

    @pl.when(j == n_q_blocks)
    def _():
        for hh in range(n_heads):
            k_ref[:, hh * head_dim:(hh + 1) * head_dim] = head(hh, kg_ref[...]).astype(k_ref.dtype)

    @pl.when(j == n_q_blocks + 1)
    def _():
        v_ref[...] = a.astype(v_ref.dtype)


def _qkv_call(x, mod_l, g_pre, w_qkv, q_gain, k_gain, n_kv, trunk, kv_dtype):
    rows, d = x.shape
    hd = q_gain.shape[-1]
    tn = n_kv * hd
    nq = w_qkv.shape[1] - 2 * tn
    assert nq % tn == 0
    nqb = nq // tn
    tm = _row_tile(trunk)
    in_specs = [
        pl.BlockSpec((tm, d), lambda i, j: (i, 0)),
        pl.BlockSpec((1, 1, d), _mod_spec(trunk, tm, 0)),
        pl.BlockSpec((1, 1, d), _mod_spec(trunk, tm, 1)),
        pl.BlockSpec((1, d), _const2),
        pl.BlockSpec((d, tn), lambda i, j: (0, j)),
        pl.BlockSpec((1, hd), _const2),
        pl.BlockSpec((1, hd), _const2),
    ]
    args = [x, mod_l, mod_l, g_pre, w_qkv, q_gain, k_gain]
    if trunk.latent:
        assert trunk.seq % tm == 0 and trunk.seq % GRID_W == 0
        tiles_per_seq = trunk.seq // tm
        in_specs += [pl.BlockSpec((tm, hd), lambda i, j: (i % tiles_per_seq, 0))] * 3
        args += list(_rope_tables(trunk.seq, hd))
    blocks = [((tm, d), F32), ((d, tn), BF16), ((tm, tn), BF16), ((tm, tn), kv_dtype), ((tm, tn), kv_dtype),
              ((tm, hd), F32), ((tm, hd), F32), ((tm, hd), F32)]
    scratch = [((tm, d), BF16)]
    return pl.pallas_call(
        functools.partial(_qkv_kernel, n_q_blocks=nqb, head_dim=hd, latent=trunk.latent,
                          q_scale=hd ** -0.5 * LOG2_E),
        grid=(rows // tm, nqb + 2),
        in_specs=in_specs,
        out_specs=[
            pl.BlockSpec((tm, tn), lambda i, j: (i, jnp.minimum(j, nqb - 1))),
            pl.BlockSpec((tm, tn), lambda i, j: (i, 0)),
            pl.BlockSpec((tm, tn), lambda i, j: (i, 0)),
        ],
        out_shape=[
            jax.ShapeDtypeStruct((rows, nq), BF16),
            jax.ShapeDtypeStruct((rows, tn), kv_dtype),
            jax.ShapeDtypeStruct((rows, tn), kv_dtype),
        ],
        scratch_shapes=[pltpu.VMEM(s, t) for s, t in scratch],
        compiler_params=_params(("parallel", "arbitrary"),
                                _vmem_limit(blocks, scratch, temps=[((tm, d), F32), ((tm, tn), F32)])),
        name="qkv",
    )(*args)


def _attn_kernel(q_ref, k_ref, v_ref, o_ref, *, q_per_kv, head_dim):
    hd = head_dim
    tq, tk = q_ref.shape[1], k_ref.shape[1]
    n_groups = ATTN_HEAD_GROUPS if q_per_kv % ATTN_HEAD_GROUPS == 0 else 1
    per = q_per_kv // n_groups
    k = k_ref[0].astype(BF16)
    v1 = jnp.concatenate([v_ref[0].astype(BF16), jnp.ones((tk, hd), BF16)], axis=1)
    qs = [jnp.concatenate([q_ref[0, :, g * hd:(g + 1) * hd] for g in range(i * per, (i + 1) * per)], axis=0)
          for i in range(n_groups)]
    ss = [lax.dot_general(q, k, (((1,), (1,)), ((), ())), preferred_element_type=F32) for q in qs]
    m_curs = [jnp.max(s, axis=-1, keepdims=True) for s in ss]

    def write(i, acc):
        out = acc[:, :hd] / acc[:, hd:]
        for j in range(per):
            g = i * per + j
            o_ref[0, :, g * hd:(g + 1) * hd] = out[j * tq:(j + 1) * tq].astype(o_ref.dtype)

    ps = [jnp.exp2(s - m).astype(BF16) for s, m in zip(ss, m_curs)]
    accs = [jnp.dot(p, v1, preferred_element_type=F32) for p in ps]
    for i, acc in enumerate(accs):
        write(i, acc)


def _attn_call(q, k, v, n_kv):
    b, t, nq = q.shape
    s = k.shape[1]
    hd = k.shape[2] // n_kv
    gw = nq // n_kv
    tq = min(ATTN_Q_TILE, t)
    assert t % tq == 0 and s <= ATTN_MAX_KEYS and s % V7X_LANES == 0 and hd % V7X_LANES == 0
    rows = (gw // hd) * tq
    blocks = [((1, tq, gw), BF16), ((1, s, hd), k.dtype), ((1, s, hd), v.dtype), ((1, tq, gw), BF16)]
    temps = [((rows, s), F32)] * 2 + [((rows, s), BF16), ((s, 2 * hd), BF16), ((rows, 2 * hd), F32)]
    return pl.pallas_call(
        functools.partial(_attn_kernel, q_per_kv=gw // hd, head_dim=hd),
        grid=(b, n_kv, t // tq),
        in_specs=[
            pl.BlockSpec((1, tq, gw), lambda bi, h, qi: (bi, qi, h)),
            pl.BlockSpec((1, s, hd), lambda bi, h, qi: (bi, 0, h)),
            pl.BlockSpec((1, s, hd), lambda bi, h, qi: (bi, 0, h)),
        ],
        out_specs=pl.BlockSpec((1, tq, gw), lambda bi, h, qi: (bi, qi, h)),
        out_shape=jax.ShapeDtypeStruct((b, t, nq), BF16),
        compiler_params=_params(("parallel", "parallel", "parallel"), _vmem_limit(blocks, temps=temps)),
        name="attn",
    )(q, k, v)


def _pool_kernel(x_ref, xp_ref, xn_ref, shift_ref, scale_ref, gate_ref, gpre_ref, gpost_ref, w_ref, cs_ref,
                 o_ref, h_sc, m_sc, *, seq):
    tm, d = x_ref.shape
    gdim = d // len(POOL_WINDOWS)
    gpre, scale, shift = gpre_ref[...], scale_ref[0], shift_ref[0]
    h_sc[0:POOL_HALO, :] = _norm_mod(xp_ref[...], gpre, scale, shift)
    h_sc[POOL_HALO:POOL_HALO + tm, :] = _norm_mod(x_ref[...], gpre, scale, shift)
    h_sc[POOL_HALO + tm:, :] = _norm_mod(xn_ref[...], gpre, scale, shift)

    row = pl.program_id(0) * tm + lax.broadcasted_iota(jnp.int32, (tm, 1), 0)
    pos = lax.rem(row, seq)
    for j, w in enumerate(POOL_WINDOWS):
        half = w // 2
        cols = pl.ds(j * gdim, gdim)
        acc = jnp.zeros((tm, gdim), F32)
        for off in range(-half, half):
            valid = jnp.logical_and(pos + off >= 0, pos + off < seq)
            acc = acc + jnp.where(valid, h_sc[pl.ds(POOL_HALO + off, tm), cols], 0.0)
        cnt = (jnp.minimum(pos + half, seq) - jnp.maximum(pos - half, 0)).astype(F32)
        p = acc / cnt - h_sc[pl.ds(POOL_HALO, tm), cols]
        y = jnp.dot(p.astype(BF16), w_ref[j], preferred_element_type=F32)
        m_sc[:, cols] = y * cs_ref[:, cols]
    o_ref[...] = x_ref[...] + gate_ref[0] * _rms(m_sc[...], gpost_ref[...])


def _pool_call(x, mod_l, g_pre, g_post, w_pool, c_scale, trunk):
    rows, d = x.shape
    tm = _row_tile(trunk)
    assert tm % POOL_HALO == 0 and max(POOL_WINDOWS) // 2 <= POOL_HALO and (tm % trunk.seq == 0 or trunk.seq % tm == 0)
    groups, gdim = w_pool.shape[0], w_pool.shape[1]
    halo_per_tile = tm // POOL_HALO
    last_halo = rows // POOL_HALO - 1
    blocks = [((tm, d), F32), ((tm, d), F32), ((groups, gdim, gdim), BF16)]
    scratch = [((tm + 2 * POOL_HALO, d), F32), ((tm, d), F32)]
    return pl.pallas_call(
        functools.partial(_pool_kernel, seq=trunk.seq),
        grid=(rows // tm,),
        in_specs=[
            pl.BlockSpec((tm, d), lambda i: (i, 0)),
            pl.BlockSpec((POOL_HALO, d), lambda i: (jnp.maximum(i * halo_per_tile - 1, 0), 0)),
            pl.BlockSpec((POOL_HALO, d), lambda i: (jnp.minimum((i + 1) * halo_per_tile, last_halo), 0)),
            pl.BlockSpec((1, 1, d), _mod_spec(trunk, tm, 0)),
            pl.BlockSpec((1, 1, d), _mod_spec(trunk, tm, 1)),
            pl.BlockSpec((1, 1, d), _mod_spec(trunk, tm, 2)),
            pl.BlockSpec((1, d), _const2),
            pl.BlockSpec((1, d), _const2),
            pl.BlockSpec((groups, gdim, gdim), _const3),
            pl.BlockSpec((1, d), _const2),
        ],
        out_specs=pl.BlockSpec((tm, d), lambda i: (i, 0)),
        out_shape=jax.ShapeDtypeStruct((rows, d), F32),
        scratch_shapes=[pltpu.VMEM(s, t) for s, t in scratch],
        compiler_params=_params(("parallel",), _vmem_limit(blocks, scratch, temps=[((tm, d), F32)] * 2)),
        name="pool",
    )(x, x, x, mod_l, mod_l, mod_l, g_pre, g_post, w_pool, c_scale)


def _trunk(x3, trunk, mod, cache_kv, p):
    b, t, d = x3.shape
    x = x3.reshape(b * t, d)
    depth = p["w_mod"].shape[0]
    n_kv = p["n_kv"]
    ia = ib = ic = 0
    new_k, new_v = [], []
    row1 = lambda a: a.reshape(1, -1)
    for l in range(depth):
        mod_l = mod[l]
        g_pre, g_post = row1(p["norm_mix_pre"][l]), row1(p["norm_mix_post"][l])
        kind = l % 3
        if kind == 0:
            y = _mixa_call(x, mod_l, g_pre, p["a_w_in"][ia], row1(p["a_norm_v"][ia]), p["a_w_s"][ia],
                           p["a_b_s"][ia], trunk)
            x = _proj_call(y, p["a_w_out"][ia], x, mod_l, g_post, trunk)
            ia += 1
        elif kind == 1:
            kv_dtype = BF16 if trunk.latent else F32
            q, k, v = _qkv_call(x, mod_l, g_pre, p["b_w_qkv"][ib], row1(p["b_q_norm"][ib]),
                                row1(p["b_k_norm"][ib]), n_kv, trunk, kv_dtype)
            k3, v3 = k.reshape(b, t, -1), v.reshape(b, t, -1)
            if trunk.latent:
                ck, cv = cache_kv
                past = ck.shape[2]
                k3 = jnp.concatenate([k3, ck[:, ib].reshape(b, past, -1).astype(BF16)], axis=1)
                v3 = jnp.concatenate([v3, cv[:, ib].reshape(b, past, -1).astype(BF16)], axis=1)
            else:
                new_k.append(k3.reshape(b, t, n_kv, -1))
                new_v.append(v3.reshape(b, t, n_kv, -1))
            o = _attn_call(q.reshape(b, t, -1), k3, v3, n_kv)
            x = _proj_call(o.reshape(b * t, -1), p["b_w_o"][ib], x, mod_l, g_post, trunk)
            ib += 1
        else:
            x = _pool_call(x, mod_l, g_pre, g_post, p["c_w_pool"][ic], row1(p["c_scale"][ic]), trunk)
            ic += 1
        x = _ffn_call(x, mod_l, row1(p["norm_ffn_pre"][l]), row1(p["norm_ffn_post"][l]),
                      p["f_w_gu"][l], p["f_w_down"][l], trunk)
    return x.reshape(b, t, d), new_k, new_v


def kernel(x_prompt, x_sample, cache_k, cache_v, c, c_ctx, w_mod, b_mod, norm_mix_pre, norm_mix_post, norm_ffn_pre, norm_ffn_post, a_w_in, a_norm_v, a_w_s, a_b_s, a_w_out, b_w_qkv, b_q_norm, b_k_norm, b_w_o, c_w_pool, c_scale, f_w_gu, f_w_down):
    batch, seq, d = x_prompt.shape
    dec_batch, dec_seq, _ = x_sample.shape
    depth = w_mod.shape[0]
    assert 1 + dec_batch <= MOD_ROWS

    cond = jnp.concatenate([c_ctx[None, :], c, jnp.zeros((MOD_ROWS - 1 - dec_batch, d), F32)], axis=0)
    mod = _mod_call(cond, w_mod, b_mod).reshape(depth, MOD_ROWS * N_MOD, 1, d)

    p = {
        "w_mod": w_mod, "n_kv": cache_k.shape[3],
        "norm_mix_pre": norm_mix_pre, "norm_mix_post": norm_mix_post,
        "norm_ffn_pre": norm_ffn_pre, "norm_ffn_post": norm_ffn_post,
        "a_w_in": a_w_in.astype(BF16), "a_norm_v": a_norm_v, "a_w_s": a_w_s.astype(BF16), "a_b_s": a_b_s,
        "a_w_out": a_w_out.astype(BF16),
        "b_w_qkv": b_w_qkv.astype(BF16), "b_q_norm": b_q_norm, "b_k_norm": b_k_norm, "b_w_o": b_w_o.astype(BF16),
        "c_w_pool": c_w_pool.astype(BF16), "c_scale": c_scale,
        "f_w_gu": f_w_gu.astype(BF16), "f_w_down": f_w_down.astype(BF16),
    }
    ctx = Trunk(batch=batch, seq=seq, mod_base=0, rows_per_cond=batch * seq, latent=False)
    lat = Trunk(batch=dec_batch, seq=dec_seq, mod_base=1, rows_per_cond=dec_seq, latent=True)
    y_prompt, new_k, new_v = _trunk(x_prompt, ctx, mod, None, p)
    y_sample, _, _ = _trunk(x_sample, lat, mod, (cache_k, cache_v), p)
    return (y_prompt, y_sample, jnp.stack(new_k, axis=1), jnp.stack(new_v, axis=1))
```

```python
import collections
import functools

import jax
import jax.numpy as jnp
from jax import lax
from jax.experimental import pallas as pl
from jax.experimental.pallas import tpu as pltpu

F32 = jnp.float32
BF16 = jnp.bfloat16

EPS = 1e-6
N_MOD = 6
GRID_W = 64
ROPE_THETA = 10000.0
POOL_WINDOWS = (2, 4, 8, 16)
POOL_HALO = 8
MOD_ROWS = 16
LOG2_E = 1.4426950408889634
V7X_LANES = 128

V7X_VMEM_BUDGET = 58 * 1024 * 1024
ROW_TILE = 512
FFN_HIDDEN_TILE = 512
COL_TILE = 512
ATTN_Q_TILE = 256
ATTN_MAX_KEYS = 4608
ATTN_HEAD_GROUPS = 2
MOD_COL_TILE = 1024

Trunk = collections.namedtuple("Trunk", "batch seq mod_base rows_per_cond latent")


def _nbytes(shape, dtype):
    n = 1
    for s in shape:
        n *= s
    return n * jnp.dtype(dtype).itemsize


def _vmem_limit(pipelined, scratch=(), temps=()):
    total = 2 * sum(_nbytes(s, d) for s, d in pipelined)
    total += sum(_nbytes(s, d) for s, d in scratch)
    total += sum(_nbytes(s, d) for s, d in temps)
    return int(min(max(total + (4 << 20), 16 << 20), V7X_VMEM_BUDGET))


def _params(semantics, limit):
    return pltpu.CompilerParams(dimension_semantics=semantics, vmem_limit_bytes=limit)


def _rms(xf, g):
    ms = jnp.mean(xf * xf, axis=-1, keepdims=True)
    return (xf * lax.rsqrt(ms + EPS)) * g


def _norm_mod(xf, g, scale, shift):
    return _rms(xf, g) * (1.0 + scale) + shift


def _row_tile(trunk):
    rows = trunk.batch * trunk.seq
    tm = min(ROW_TILE, rows)
    assert rows % tm == 0 and (trunk.rows_per_cond % tm == 0)
    return tm


def _mod_spec(trunk, tm, m, ahead=0):
    last_tile = trunk.batch * trunk.seq // tm - 1

    def index(i, *_):
        tile = jnp.minimum(i + ahead, last_tile)
        return ((trunk.mod_base + (tile * tm) // trunk.rows_per_cond) * N_MOD + m, 0, 0)
    return index


def _next_tile(trunk, tm):
    last_tile = trunk.batch * trunk.seq // tm - 1
    return lambda i, *_: (jnp.minimum(i + 1, last_tile), 0)


def _mod_kernel(c_ref, w_ref, b_ref, o_ref):
    c = c_ref[...]
    s = (c * jax.nn.sigmoid(c)).astype(BF16)
    o_ref[0] = jnp.dot(s, w_ref[0].astype(BF16), preferred_element_type=F32) + b_ref[0]


def _mod_call(cond, w_mod, b_mod):
    depth, d, n = w_mod.shape
    tn = min(MOD_COL_TILE, n)
    assert n % tn == 0
    blocks = [((MOD_ROWS, d), F32), ((1, d, tn), F32), ((1, 1, tn), F32), ((1, MOD_ROWS, tn), F32)]
    return pl.pallas_call(
        _mod_kernel,
        grid=(depth, n // tn),
        in_specs=[
            pl.BlockSpec((MOD_ROWS, d), lambda l, j: (0, 0)),
            pl.BlockSpec((1, d, tn), lambda l, j: (l, 0, j)),
            pl.BlockSpec((1, 1, tn), lambda l, j: (l, 0, j)),
        ],
        out_specs=pl.BlockSpec((1, MOD_ROWS, tn), lambda l, j: (l, 0, j)),
        out_shape=jax.ShapeDtypeStruct((depth, MOD_ROWS, n), F32),
        compiler_params=_params(("parallel", "parallel"), _vmem_limit(blocks, temps=[((d, tn), BF16)])),
        name="mod",
    )(cond, w_mod, b_mod.reshape(depth, 1, n))


def _ffn_kernel(x_ref, xn_ref, shift_ref, scale_ref, shiftn_ref, scalen_ref, gate_ref, gpre_ref, gpost_ref,
                wg_ref, wu_ref, wd_ref, o_ref, h_even, h_odd, acc_sc, *, n_chunks):
    i, c = pl.program_id(0), pl.program_id(1)

    @pl.when(jnp.logical_and(i == 0, c == 0))
    def _():
        h_even[...] = _norm_mod(x_ref[...], gpre_ref[...], scale_ref[0], shift_ref[0]).astype(BF16)

    @pl.when(c == 0)
    def _():
        acc_sc[...] = jnp.zeros_like(acc_sc)

    def chunk(h_ref):
        h = h_ref[...]
        g = jnp.dot(h, wg_ref[...], preferred_element_type=F32)
        u = jnp.dot(h, wu_ref[...], preferred_element_type=F32)
        a = (g * jax.nn.sigmoid(g)) * u
        return jnp.dot(a.astype(BF16), wd_ref[...], preferred_element_type=F32)

    def step(h_cur, h_nxt):
        @pl.when(c < n_chunks - 1)
        def _():
            acc_sc[...] += chunk(h_cur)

        @pl.when(c == n_chunks - 1)
        def _():
            h_nxt[...] = _norm_mod(xn_ref[...], gpre_ref[...], scalen_ref[0], shiftn_ref[0]).astype(BF16)
            f = acc_sc[...] + chunk(h_cur)
            o_ref[...] = x_ref[...] + gate_ref[0] * _rms(f, gpost_ref[...])

    @pl.when(i % 2 == 0)
    def _():
        step(h_even, h_odd)

    @pl.when(i % 2 == 1)
    def _():
        step(h_odd, h_even)


def _ffn_call(x, mod_l, g_pre, g_post, w_gu, w_down, layer, trunk):
    rows, d = x.shape
    hidden = w_down.shape[1]
    tm = _row_tile(trunk)
    th = min(FFN_HIDDEN_TILE, hidden)
    assert hidden % th == 0
    nc = hidden // th
    blocks = [((tm, d), F32)] * 3 + [((d, th), BF16), ((d, th), BF16), ((th, d), BF16)]
    scratch = [((tm, d), BF16), ((tm, d), BF16), ((tm, d), F32)]
    temps = [((tm, th), F32)] * 4 + [((tm, d), F32)]
    return pl.pallas_call(
        functools.partial(_ffn_kernel, n_chunks=nc),
        grid=(rows // tm, nc),
        in_specs=[
            pl.BlockSpec((tm, d), lambda i, c: (i, 0)),
            pl.BlockSpec((tm, d), _next_tile(trunk, tm)),
            pl.BlockSpec((1, 1, d), _mod_spec(trunk, tm, 3)),
            pl.BlockSpec((1, 1, d), _mod_spec(trunk, tm, 4)),
            pl.BlockSpec((1, 1, d), _mod_spec(trunk, tm, 3, ahead=1)),
            pl.BlockSpec((1, 1, d), _mod_spec(trunk, tm, 4, ahead=1)),
            pl.BlockSpec((1, 1, d), _mod_spec(trunk, tm, 5)),
            pl.BlockSpec((None, 1, d), lambda i, c: (layer, 0, 0)),
            pl.BlockSpec((None, 1, d), lambda i, c: (layer, 0, 0)),
            pl.BlockSpec((None, d, th), lambda i, c: (layer, 0, c)),
            pl.BlockSpec((None, d, th), lambda i, c: (layer, 0, nc + c)),
            pl.BlockSpec((None, th, d), lambda i, c: (layer, c, 0)),
        ],
        out_specs=pl.BlockSpec((tm, d), lambda i, c: (i, 0)),
        out_shape=jax.ShapeDtypeStruct((rows, d), F32),
        scratch_shapes=[pltpu.VMEM(s, t) for s, t in scratch],
        compiler_params=_params(("arbitrary", "arbitrary"), _vmem_limit(blocks, scratch, temps)),
        name="ffn",
    )(x, x, mod_l, mod_l, mod_l, mod_l, mod_l, g_pre, g_post, w_gu, w_gu, w_down)


def _proj_kernel(y_ref, w_ref, x_ref, gate_ref, gpost_ref, o_ref):
    m = jnp.dot(y_ref[...], w_ref[...], preferred_element_type=F32)
    o_ref[...] = x_ref[...] + gate_ref[0] * _rms(m, gpost_ref[...])


def _proj_call(y, w, w_layer, x, mod_l, g_post, layer, trunk):
    rows, d = x.shape
    k = y.shape[1]
    tm = _row_tile(trunk)
    blocks = [((tm, k), BF16), ((k, d), BF16), ((tm, d), F32), ((tm, d), F32)]
    return pl.pallas_call(
        _proj_kernel,
        grid=(rows // tm,),
        in_specs=[
            pl.BlockSpec((tm, k), lambda i: (i, 0)),
            pl.BlockSpec((None, k, d), lambda i: (w_layer, 0, 0)),
            pl.BlockSpec((tm, d), lambda i: (i, 0)),
            pl.BlockSpec((1, 1, d), _mod_spec(trunk, tm, 2)),
            pl.BlockSpec((None, 1, d), lambda i: (layer, 0, 0)),
        ],
        out_specs=pl.BlockSpec((tm, d), lambda i: (i, 0)),
        out_shape=jax.ShapeDtypeStruct((rows, d), F32),
        compiler_params=_params(("parallel",), _vmem_limit(blocks, temps=[((tm, d), F32)] * 2)),
        name="proj",
    )(y, w, x, mod_l, g_post)


def _mixa_kernel(x_ref, shift_ref, scale_ref, gpre_ref, win_ref, gv_ref, ws_ref, bs_ref,
                 y_ref, h_sc, a_sc, *, n_blocks, n_u_blocks, chunk, groups):
    j = pl.program_id(1)

    @pl.when(j == 0)
    def _():
        h = _norm_mod(x_ref[...], gpre_ref[...], scale_ref[0], shift_ref[0])
        h_sc[...] = h.astype(BF16)

    a_sc[j] = jnp.dot(h_sc[...], win_ref[...], preferred_element_type=F32)

    @pl.when(j == n_blocks - 1)
    def _():
        tm, tn = a_sc.shape[1], a_sc.shape[2]
        width = n_u_blocks * tn
        gdim = width // groups
        ss = jnp.zeros((tm, 1), F32)
        for b in range(n_u_blocks, n_blocks):
            vb = a_sc[b]
            ss = ss + jnp.sum(vb * vb, axis=-1, keepdims=True)
        inv = lax.rsqrt(ss / width + EPS)
        for g in range(groups):
            col = g * gdim
            b, off = col // tn, col % tn
            gv = gv_ref[:, col:col + gdim]
            for c in range(tm // chunk):
                r = c * chunk
                v = (a_sc[n_u_blocks + b, r:r + chunk, off:off + gdim] * inv[r:r + chunk]) * gv
                s = jnp.dot(ws_ref[g], v.astype(BF16), preferred_element_type=F32) + bs_ref[g]
                u = a_sc[b, r:r + chunk, off:off + gdim]
                y_ref[r:r + chunk, col:col + gdim] = (u * s).astype(y_ref.dtype)


def _mixa_call(x, mod_l, g_pre, layer, w_in, g_v, w_s, b_s, a_layer, trunk):
    rows, d = x.shape
    width = w_in.shape[2] // 2
    groups, chunk = w_s.shape[1], w_s.shape[2]
    tm = _row_tile(trunk)
    tn = min(COL_TILE, width)
    gdim = width // groups
    assert width % tn == 0 and tn % gdim == 0 and tm % chunk == 0 and trunk.seq % chunk == 0
    nb = 2 * width // tn
    blocks = [((tm, d), F32), ((d, tn), BF16), ((tm, width), BF16), ((groups, chunk, chunk), BF16),
              ((groups, chunk, V7X_LANES), F32)]
    scratch = [((tm, d), BF16), ((nb, tm, tn), F32)]
    return pl.pallas_call(
        functools.partial(_mixa_kernel, n_blocks=nb, n_u_blocks=nb // 2, chunk=chunk, groups=groups),
        grid=(rows // tm, nb),
        in_specs=[
            pl.BlockSpec((tm, d), lambda i, j: (i, 0)),
            pl.BlockSpec((1, 1, d), _mod_spec(trunk, tm, 0)),
            pl.BlockSpec((1, 1, d), _mod_spec(trunk, tm, 1)),
            pl.BlockSpec((None, 1, d), lambda i, j: (layer, 0, 0)),
            pl.BlockSpec((None, d, tn), lambda i, j: (a_layer, 0, j)),
            pl.BlockSpec((None, 1, width), lambda i, j: (a_layer, 0, 0)),
            pl.BlockSpec((None, groups, chunk, chunk), lambda i, j: (a_layer, 0, 0, 0)),
            pl.BlockSpec((None, groups, chunk, 1), lambda i, j: (a_layer, 0, 0, 0)),
        ],
        out_specs=pl.BlockSpec((tm, width), lambda i, j: (i, 0)),
        out_shape=jax.ShapeDtypeStruct((rows, width), BF16),
        scratch_shapes=[pltpu.VMEM(s, t) for s, t in scratch],
        compiler_params=_params(("parallel", "arbitrary"),
                                _vmem_limit(blocks, scratch, temps=[((tm, d), F32), ((tm, tn), F32)])),
        name="mixa",
    )(x, mod_l, mod_l, g_pre, w_in, g_v, w_s, b_s)


def _rope_tables(seq, head_dim):
    axis_dim = head_dim // 2
    t = jnp.arange(seq)
    n_rows = seq // GRID_W
    row = jnp.minimum(t // GRID_W, n_rows - 1).astype(F32)
    col = (t % GRID_W).astype(F32)
    inv = jnp.power(ROPE_THETA, -jnp.arange(0, axis_dim, 2, dtype=F32) / axis_dim)
    ang = jnp.concatenate([row[:, None] * inv, col[:, None] * inv], axis=-1)
    cos, sin = jnp.cos(ang), jnp.sin(ang)
    zero = jnp.zeros_like(sin)
    cos2 = jnp.stack([cos, cos], axis=-1).reshape(seq, head_dim)
    sin_a = jnp.stack([-sin, zero], axis=-1).reshape(seq, head_dim)
    sin_b = jnp.stack([zero, sin], axis=-1).reshape(seq, head_dim)
    return cos2, sin_a, sin_b


def _qkv_kernel(*refs, n_q_blocks, head_dim, latent, q_scale):
    if latent:
        (x_ref, shift_ref, scale_ref, gpre_ref, w_ref, qg_ref, kg_ref, cos_ref, sa_ref, sb_ref,
         q_ref, k_ref, v_ref, h_sc) = refs
    else:
        (x_ref, shift_ref, scale_ref, gpre_ref, w_ref, qg_ref, kg_ref, q_ref, k_ref, v_ref, h_sc) = refs
    j = pl.program_id(1)

    @pl.when(j == 0)
    def _():
        h = _norm_mod(x_ref[...], gpre_ref[...], scale_ref[0], shift_ref[0])
        h_sc[...] = h.astype(BF16)

    a = jnp.dot(h_sc[...], w_ref[...], preferred_element_type=F32)
    n_heads = a.shape[1] // head_dim

    def head(hh, gain):
        blk = _rms(a[:, hh * head_dim:(hh + 1) * head_dim], gain)
        if latent:
            nxt = pltpu.roll(blk, head_dim - 1, 1)
            prv = pltpu.roll(blk, 1, 1)
            blk = blk * cos_ref[...] + nxt * sa_ref[...] + prv * sb_ref[...]
        return blk

    @pl.when(j < n_q_blocks)
    def _():
        for hh in range(n_heads):
            q_ref[:, hh * head_dim:(hh + 1) * head_dim] = (head(hh, qg_ref[...]) * q_scale).astype(q_ref.dtype)

    @pl.when(j == n_q_blocks)
    def _():
        for hh in range(n_heads):
            k_ref[:, hh * head_dim:(hh + 1) * head_dim] = head(hh, kg_ref[...]).astype(k_ref.dtype)

    @pl.when(j == n_q_blocks + 1)
    def _():
        v_ref[...] = a.astype(v_ref.dtype)


def _qkv_call(x, mod_l, g_pre, layer, w_qkv, q_gain, k_gain, b_layer, n_kv, trunk, kv_dtype):
    rows, d = x.shape
    hd = q_gain.shape[-1]
    tn = n_kv * hd
    nq = w_qkv.shape[2] - 2 * tn
    assert nq % tn == 0
    nqb = nq // tn
    tm = _row_tile(trunk)
    in_specs = [
        pl.BlockSpec((tm, d), lambda i, j: (i, 0)),
        pl.BlockSpec((1, 1, d), _mod_spec(trunk, tm, 0)),
        pl.BlockSpec((1, 1, d), _mod_spec(trunk, tm, 1)),
        pl.BlockSpec((None, 1, d), lambda i, j: (layer, 0, 0)),
        pl.BlockSpec((None, d, tn), lambda i, j: (b_layer, 0, j)),
        pl.BlockSpec((None, 1, hd), lambda i, j: (b_layer, 0, 0)),
        pl.BlockSpec((None, 1, hd), lambda i, j: (b_layer, 0, 0)),
    ]
    args = [x, mod_l, mod_l, g_pre, w_qkv, q_gain, k_gain]
    if trunk.latent:
        assert trunk.seq % tm == 0 and trunk.seq % GRID_W == 0
        tiles_per_seq = trunk.seq // tm
        in_specs += [pl.BlockSpec((tm, hd), lambda i, j: (i % tiles_per_seq, 0))] * 3
        args += list(_rope_tables(trunk.seq, hd))
    blocks = [((tm, d), F32), ((d, tn), BF16), ((tm, tn), BF16), ((tm, tn), kv_dtype), ((tm, tn), kv_dtype),
              ((tm, hd), F32), ((tm, hd), F32), ((tm, hd), F32)]
    scratch = [((tm, d), BF16)]
    return pl.pallas_call(
        functools.partial(_qkv_kernel, n_q_blocks=nqb, head_dim=hd, latent=trunk.latent,
                          q_scale=hd ** -0.5 * LOG2_E),
        grid=(rows // tm, nqb + 2),
        in_specs=in_specs,
        out_specs=[
            pl.BlockSpec((tm, tn), lambda i, j: (i, jnp.minimum(j, nqb - 1))),
            pl.BlockSpec((tm, tn), lambda i, j: (i, 0)),
            pl.BlockSpec((tm, tn), lambda i, j: (i, 0)),
        ],
        out_shape=[
            jax.ShapeDtypeStruct((rows, nq), BF16),
            jax.ShapeDtypeStruct((rows, tn), kv_dtype),
            jax.ShapeDtypeStruct((rows, tn), kv_dtype),
        ],
        scratch_shapes=[pltpu.VMEM(s, t) for s, t in scratch],
        compiler_params=_params(("parallel", "arbitrary"),
                                _vmem_limit(blocks, scratch, temps=[((tm, d), F32), ((tm, tn), F32)])),
        name="qkv",
    )(*args)


def _attn_kernel(q_ref, k_ref, v_ref, o_ref, *, q_per_kv, head_dim):
    hd = head_dim
    tq, tk = q_ref.shape[1], k_ref.shape[1]
    n_groups = ATTN_HEAD_GROUPS if q_per_kv % ATTN_HEAD_GROUPS == 0 else 1
    per = q_per_kv // n_groups
    k = k_ref[0].astype(BF16)
    v1 = jnp.concatenate([v_ref[0].astype(BF16), jnp.ones((tk, hd), BF16)], axis=1)
    qs = [jnp.concatenate([q_ref[0, :, g * hd:(g + 1) * hd] for g in range(i * per, (i + 1) * per)], axis=0)
          for i in range(n_groups)]
    ss = [lax.dot_general(q, k, (((1,), (1,)), ((), ())), preferred_element_type=F32) for q in qs]
    m_curs = [jnp.max(s, axis=-1, keepdims=True) for s in ss]

    def write(i, acc):
        out = acc[:, :hd] / acc[:, hd:]
        for j in range(per):
            g = i * per + j
            o_ref[0, :, g * hd:(g + 1) * hd] = out[j * tq:(j + 1) * tq].astype(o_ref.dtype)

    ps = [jnp.exp2(s - m).astype(BF16) for s, m in zip(ss, m_curs)]
    accs = [jnp.dot(p, v1, preferred_element_type=F32) for p in ps]
    for i, acc in enumerate(accs):
        write(i, acc)


def _attn_call(q, k, v, n_kv):
    b, t, nq = q.shape
    s = k.shape[1]
    hd = k.shape[2] // n_kv
    gw = nq // n_kv
    tq = min(ATTN_Q_TILE, t)
    assert t % tq == 0 and s <= ATTN_MAX_KEYS and s % V7X_LANES == 0 and hd % V7X_LANES == 0
    rows = (gw // hd) * tq
    blocks = [((1, tq, gw), BF16), ((1, s, hd), k.dtype), ((1, s, hd), v.dtype), ((1, tq, gw), BF16)]
    temps = [((rows, s), F32)] * 2 + [((rows, s), BF16), ((s, 2 * hd), BF16), ((rows, 2 * hd), F32)]
    return pl.pallas_call(
        functools.partial(_attn_kernel, q_per_kv=gw // hd, head_dim=hd),
        grid=(b, n_kv, t // tq),
        in_specs=[
            pl.BlockSpec((1, tq, gw), lambda bi, h, qi: (bi, qi, h)),
            pl.BlockSpec((1, s, hd), lambda bi, h, qi: (bi, 0, h)),
            pl.BlockSpec((1, s, hd), lambda bi, h, qi: (bi, 0, h)),
        ],
        out_specs=pl.BlockSpec((1, tq, gw), lambda bi, h, qi: (bi, qi, h)),
        out_shape=jax.ShapeDtypeStruct((b, t, nq), BF16),
        compiler_params=_params(("parallel", "parallel", "parallel"), _vmem_limit(blocks, temps=temps)),
        name="attn",
    )(q, k, v)


def _pool_kernel(x_ref, xp_ref, xn_ref, shift_ref, scale_ref, gate_ref, gpre_ref, gpost_ref, w_ref, cs_ref,
                 o_ref, h_sc, ta_sc, tb_sc, m_sc, *, tiles_per_seq):
    tm, d = x_ref.shape
    pad = POOL_HALO
    gdim = d // len(POOL_WINDOWS)
    gpre, scale, shift = gpre_ref[...], scale_ref[0], shift_ref[0]
    t_in_seq = lax.rem(pl.program_id(0), tiles_per_seq)
    keep_prev = (t_in_seq > 0).astype(F32)
    keep_next = (t_in_seq < tiles_per_seq - 1).astype(F32)
    zeros = jnp.zeros((pad, d), F32)
    h_sc[0:pad, :] = zeros
    h_sc[pad:2 * pad, :] = _norm_mod(xp_ref[...], gpre, scale, shift) * keep_prev
    h_sc[2 * pad:2 * pad + tm, :] = _norm_mod(x_ref[...], gpre, scale, shift)
    h_sc[2 * pad + tm:3 * pad + tm, :] = _norm_mod(xn_ref[...], gpre, scale, shift) * keep_next
    h_sc[3 * pad + tm:, :] = zeros
    for t_sc in (ta_sc, tb_sc):
        t_sc[0:pad, :] = zeros[:, :gdim]
        t_sc[3 * pad + tm:, :] = zeros[:, :gdim]

    ext = tm + 2 * pad

    def shifted_sum(src, cols, lo, hi):
        return src[pl.ds(pad + lo, ext), cols] + src[pl.ds(pad + hi, ext), cols]

    pos = t_in_seq * tm + lax.broadcasted_iota(jnp.int32, (tm, 1), 0)
    seq = tiles_per_seq * tm
    all_cols = pl.ds(0, gdim)
    for j, w in enumerate(POOL_WINDOWS):
        half = w // 2
        cols = pl.ds(j * gdim, gdim)
        src, src_cols, span = h_sc, cols, 1
        for dst in (ta_sc, tb_sc, ta_sc):
            if span * 2 >= w:
                break
            lo, hi = (-1, 0) if span == 1 else (-(span // 2), span // 2)
            dst[pl.ds(pad, ext), :] = shifted_sum(src, src_cols, lo, hi)
            src, src_cols, span = dst, all_cols, span * 2
        lo, hi = (-1, 0) if span == 1 else (-(span // 2), span // 2)
        win = src[pl.ds(2 * pad + lo, tm), src_cols] + src[pl.ds(2 * pad + hi, tm), src_cols]
        cnt = (jnp.minimum(pos + half, seq) - jnp.maximum(pos - half, 0)).astype(F32)
        p = win * (1.0 / cnt) - h_sc[pl.ds(2 * pad, tm), cols]
        y = jnp.dot(p.astype(BF16), w_ref[j], preferred_element_type=F32)
        m_sc[:, cols] = y * cs_ref[:, cols]
    o_ref[...] = x_ref[...] + gate_ref[0] * _rms(m_sc[...], gpost_ref[...])


def _pool_call(x, mod_l, g_pre, g_post, layer, w_pool, c_scale, c_layer, trunk):
    rows, d = x.shape
    tm = min(_row_tile(trunk), trunk.seq)
    assert trunk.seq % tm == 0 and tm % POOL_HALO == 0 and max(POOL_WINDOWS) // 2 <= POOL_HALO
    groups, gdim = w_pool.shape[1], w_pool.shape[2]
    assert groups == len(POOL_WINDOWS) and POOL_WINDOWS == (2, 4, 8, 16)
    halo_per_tile = tm // POOL_HALO
    last_halo = rows // POOL_HALO - 1
    blocks = [((tm, d), F32), ((tm, d), F32), ((groups, gdim, gdim), BF16)]
    ext_rows = tm + 4 * POOL_HALO
    scratch = [((ext_rows, d), F32), ((ext_rows, gdim), F32), ((ext_rows, gdim), F32), ((tm, d), F32)]
    return pl.pallas_call(
        functools.partial(_pool_kernel, tiles_per_seq=trunk.seq // tm),
        grid=(rows // tm,),
        in_specs=[
            pl.BlockSpec((tm, d), lambda i: (i, 0)),
            pl.BlockSpec((POOL_HALO, d), lambda i: (jnp.maximum(i * halo_per_tile - 1, 0), 0)),
            pl.BlockSpec((POOL_HALO, d), lambda i: (jnp.minimum((i + 1) * halo_per_tile, last_halo), 0)),
            pl.BlockSpec((1, 1, d), _mod_spec(trunk, tm, 0)),
            pl.BlockSpec((1, 1, d), _mod_spec(trunk, tm, 1)),
            pl.BlockSpec((1, 1, d), _mod_spec(trunk, tm, 2)),
            pl.BlockSpec((None, 1, d), lambda i: (layer, 0, 0)),
            pl.BlockSpec((None, 1, d), lambda i: (layer, 0, 0)),
            pl.BlockSpec((None, groups, gdim, gdim), lambda i: (c_layer, 0, 0, 0)),
            pl.BlockSpec((None, 1, d), lambda i: (c_layer, 0, 0)),
        ],
        out_specs=pl.BlockSpec((tm, d), lambda i: (i, 0)),
        out_shape=jax.ShapeDtypeStruct((rows, d), F32),
        scratch_shapes=[pltpu.VMEM(s, t) for s, t in scratch],
        compiler_params=_params(("parallel",), _vmem_limit(blocks, scratch, temps=[((tm, d), F32)] * 2)),
        name="pool",
    )(x, x, x, mod_l, mod_l, mod_l, g_pre, g_post, w_pool, c_scale)


def _trunk(x3, trunk, mod, cache_kv, p):
    b, t, d = x3.shape
    x = x3.reshape(b * t, d)
    depth = p["norm_mix_pre"].shape[0]
    n_kv = p["n_kv"]
    ia = ib = ic = 0
    new_k, new_v = [], []
    for l in range(depth):
        mod_l = mod[l]
        g_pre, g_post = p["norm_mix_pre"], p["norm_mix_post"]
        kind = l % 3
        if kind == 0:
            y = _mixa_call(x, mod_l, g_pre, l, p["a_w_in"], p["a_norm_v"], p["a_w_s"], p["a_b_s"], ia, trunk)
            x = _proj_call(y, p["a_w_out"], ia, x, mod_l, g_post, l, trunk)
            ia += 1
        elif kind == 1:
            kv_dtype = BF16 if trunk.latent else F32
            q, k, v = _qkv_call(x, mod_l, g_pre, l, p["b_w_qkv"], p["b_q_norm"], p["b_k_norm"], ib, n_kv, trunk,
                                kv_dtype)
            k3, v3 = k.reshape(b, t, -1), v.reshape(b, t, -1)
            if trunk.latent:
                ck, cv = cache_kv
                past = ck.shape[2]
                k3 = jnp.concatenate([k3, ck[:, ib].reshape(b, past, -1).astype(BF16)], axis=1)
                v3 = jnp.concatenate([v3, cv[:, ib].reshape(b, past, -1).astype(BF16)], axis=1)
            else:
                new_k.append(k3.reshape(b, t, n_kv, -1))
                new_v.append(v3.reshape(b, t, n_kv, -1))
            o = _attn_call(q.reshape(b, t, -1), k3, v3, n_kv)
            x = _proj_call(o.reshape(b * t, -1), p["b_w_o"], ib, x, mod_l, g_post, l, trunk)
            ib += 1
        else:
            x = _pool_call(x, mod_l, g_pre, g_post, l, p["c_w_pool"], p["c_scale"], ic, trunk)
            ic += 1
        x = _ffn_call(x, mod_l, p["norm_ffn_pre"], p["norm_ffn_post"], p["f_w_gu"], p["f_w_down"], l, trunk)
    return x.reshape(b, t, d), new_k, new_v


def kernel(x_prompt, x_sample, cache_k, cache_v, c, c_ctx, w_mod, b_mod, norm_mix_pre, norm_mix_post, norm_ffn_pre, norm_ffn_post, a_w_in, a_norm_v, a_w_s, a_b_s, a_w_out, b_w_qkv, b_q_norm, b_k_norm, b_w_o, c_w_pool, c_scale, f_w_gu, f_w_down):
    batch, seq, d = x_prompt.shape
    dec_batch, dec_seq, _ = x_sample.shape
    depth = w_mod.shape[0]
    assert 1 + dec_batch <= MOD_ROWS

    cond = jnp.concatenate([c_ctx[None, :], c, jnp.zeros((MOD_ROWS - 1 - dec_batch, d), F32)], axis=0)
    mod = _mod_call(cond, w_mod, b_mod).reshape(depth, MOD_ROWS * N_MOD, 1, d)

    rows3 = lambda a: a.reshape(a.shape[0], 1, a.shape[-1])
    p = {
        "n_kv": cache_k.shape[3],
        "norm_mix_pre": rows3(norm_mix_pre), "norm_mix_post": rows3(norm_mix_post),
        "norm_ffn_pre": rows3(norm_ffn_pre), "norm_ffn_post": rows3(norm_ffn_post),
        "a_w_in": a_w_in.astype(BF16), "a_norm_v": rows3(a_norm_v), "a_w_s": a_w_s.astype(BF16),
        "a_b_s": a_b_s[..., None], "a_w_out": a_w_out.astype(BF16),
        "b_w_qkv": b_w_qkv.astype(BF16), "b_q_norm": rows3(b_q_norm), "b_k_norm": rows3(b_k_norm),
        "b_w_o": b_w_o.astype(BF16),
        "c_w_pool": c_w_pool.astype(BF16), "c_scale": rows3(c_scale),
        "f_w_gu": f_w_gu.astype(BF16), "f_w_down": f_w_down.astype(BF16),
    }
    ctx = Trunk(batch=batch, seq=seq, mod_base=0, rows_per_cond=batch * seq, latent=False)
    lat = Trunk(batch=dec_batch, seq=dec_seq, mod_base=1, rows_per_cond=dec_seq, latent=True)
    y_prompt, new_k, new_v = _trunk(x_prompt, ctx, mod, None, p)
    y_sample, _, _ = _trunk(x_sample, lat, mod, (cache_k, cache_v), p)
    return (y_prompt, y_sample, jnp.stack(new_k, axis=1), jnp.stack(new_v, axis=1))
```

```python
import collections
import functools

import jax
import jax.numpy as jnp
from jax import lax
from jax.experimental import pallas as pl
from jax.experimental.pallas import tpu as pltpu

F32 = jnp.float32
BF16 = jnp.bfloat16

EPS = 1e-6
N_MOD = 6
GRID_W = 64
ROPE_THETA = 10000.0
POOL_WINDOWS = (2, 4, 8, 16)
POOL_HALO = 8
MOD_ROWS = 16
LOG2_E = 1.4426950408889634
V7X_LANES = 128

V7X_VMEM_BUDGET = 58 * 1024 * 1024
ROW_TILE = 512
FFN_HIDDEN_TILE = 512
COL_TILE = 512
ATTN_Q_TILE = 256
ATTN_MAX_KEYS = 4608
ATTN_HEAD_GROUPS = 2
MOD_COL_TILE = 1024

Trunk = collections.namedtuple("Trunk", "batch seq mod_base rows_per_cond latent")


def _nbytes(shape, dtype):
    n = 1
    for s in shape:
        n *= s
    return n * jnp.dtype(dtype).itemsize


def _vmem_limit(pipelined, scratch=(), temps=()):
    total = 2 * sum(_nbytes(s, d) for s, d in pipelined)
    total += sum(_nbytes(s, d) for s, d in scratch)
    total += sum(_nbytes(s, d) for s, d in temps)
    return int(min(max(total + (4 << 20), 16 << 20), V7X_VMEM_BUDGET))


def _params(semantics, limit):
    return pltpu.CompilerParams(dimension_semantics=semantics, vmem_limit_bytes=limit)


def _rms(xf, g):
    ms = jnp.mean(xf * xf, axis=-1, keepdims=True)
    return (xf * lax.rsqrt(ms + EPS)) * g


def _norm_mod(xf, g, scale, shift):
    return _rms(xf, g) * (1.0 + scale) + shift


def _row_tile(trunk):
    rows = trunk.batch * trunk.seq
    tm = min(ROW_TILE, rows)
    assert rows % tm == 0 and (trunk.rows_per_cond % tm == 0)
    return tm


def _mod_spec(trunk, tm, m, ahead=0):
    last_tile = trunk.batch * trunk.seq // tm - 1

    def index(i, *_):
        tile = jnp.minimum(i + ahead, last_tile)
        return ((trunk.mod_base + (tile * tm) // trunk.rows_per_cond) * N_MOD + m, 0, 0)
    return index


def _col_blocks(w, tn):
    n, d, cols = w.shape
    tn = min(tn, cols)
    assert cols % tn == 0
    return w.astype(BF16).reshape(n, d, cols // tn, tn).transpose(0, 2, 1, 3)


def _next_tile(trunk, tm):
    last_tile = trunk.batch * trunk.seq // tm - 1
    return lambda i, *_: (jnp.minimum(i + 1, last_tile), 0)


def _mod_kernel(c_ref, w_ref, b_ref, o_ref):
    c = c_ref[...]
    s = (c * jax.nn.sigmoid(c)).astype(BF16)
    o_ref[0] = jnp.dot(s, w_ref[0].astype(BF16), preferred_element_type=F32) + b_ref[0]


def _mod_call(cond, w_mod, b_mod):
    depth, d, n = w_mod.shape
    tn = min(MOD_COL_TILE, n)
    assert n % tn == 0
    blocks = [((MOD_ROWS, d), F32), ((1, d, tn), F32), ((1, 1, tn), F32), ((1, MOD_ROWS, tn), F32)]
    return pl.pallas_call(
        _mod_kernel,
        grid=(depth, n // tn),
        in_specs=[
            pl.BlockSpec((MOD_ROWS, d), lambda l, j: (0, 0)),
            pl.BlockSpec((1, d, tn), lambda l, j: (l, 0, j)),
            pl.BlockSpec((1, 1, tn), lambda l, j: (l, 0, j)),
        ],
        out_specs=pl.BlockSpec((1, MOD_ROWS, tn), lambda l, j: (l, 0, j)),
        out_shape=jax.ShapeDtypeStruct((depth, MOD_ROWS, n), F32),
        compiler_params=_params(("parallel", "parallel"), _vmem_limit(blocks, temps=[((d, tn), BF16)])),
        name="mod",
    )(cond, w_mod, b_mod.reshape(depth, 1, n))


def _ffn_kernel(x_ref, xn_ref, shift_ref, scale_ref, shiftn_ref, scalen_ref, gate_ref, gpre_ref, gpost_ref,
                wg_ref, wu_ref, wd_ref, o_ref, h_even, h_odd, acc_sc, *, n_chunks):
    i, c = pl.program_id(0), pl.program_id(1)

    @pl.when(jnp.logical_and(i == 0, c == 0))
    def _():
        h_even[...] = _norm_mod(x_ref[...], gpre_ref[...], scale_ref[0], shift_ref[0]).astype(BF16)

    @pl.when(c == 0)
    def _():
        acc_sc[...] = jnp.zeros_like(acc_sc)

    def chunk(h_ref):
        h = h_ref[...]
        g = jnp.dot(h, wg_ref[...], preferred_element_type=F32)
        u = jnp.dot(h, wu_ref[...], preferred_element_type=F32)
        a = (g * jax.nn.sigmoid(g)) * u
        return jnp.dot(a.astype(BF16), wd_ref[...], preferred_element_type=F32)

    def step(h_cur, h_nxt):
        @pl.when(c < n_chunks - 1)
        def _():
            acc_sc[...] += chunk(h_cur)

        @pl.when(c == n_chunks - 1)
        def _():
            h_nxt[...] = _norm_mod(xn_ref[...], gpre_ref[...], scalen_ref[0], shiftn_ref[0]).astype(BF16)
            f = acc_sc[...] + chunk(h_cur)
            o_ref[...] = x_ref[...] + gate_ref[0] * _rms(f, gpost_ref[...])

    @pl.when(i % 2 == 0)
    def _():
        step(h_even, h_odd)

    @pl.when(i % 2 == 1)
    def _():
        step(h_odd, h_even)


def _ffn_call(x, mod_l, g_pre, g_post, w_gu, w_down, layer, trunk):
    rows, d = x.shape
    hidden = w_down.shape[1]
    tm = _row_tile(trunk)
    th = w_gu.shape[3]
    nc = hidden // th
    assert hidden % th == 0 and w_gu.shape[1] == 2 * nc
    blocks = [((tm, d), F32)] * 3 + [((d, th), BF16), ((d, th), BF16), ((th, d), BF16)]
    scratch = [((tm, d), BF16), ((tm, d), BF16), ((tm, d), F32)]
    temps = [((tm, th), F32)] * 4 + [((tm, d), F32)]
    return pl.pallas_call(
        functools.partial(_ffn_kernel, n_chunks=nc),
        grid=(rows // tm, nc),
        in_specs=[
            pl.BlockSpec((tm, d), lambda i, c: (i, 0)),
            pl.BlockSpec((tm, d), _next_tile(trunk, tm)),
            pl.BlockSpec((1, 1, d), _mod_spec(trunk, tm, 3)),
            pl.BlockSpec((1, 1, d), _mod_spec(trunk, tm, 4)),
            pl.BlockSpec((1, 1, d), _mod_spec(trunk, tm, 3, ahead=1)),
            pl.BlockSpec((1, 1, d), _mod_spec(trunk, tm, 4, ahead=1)),
            pl.BlockSpec((1, 1, d), _mod_spec(trunk, tm, 5)),
            pl.BlockSpec((None, 1, d), lambda i, c: (layer, 0, 0)),
            pl.BlockSpec((None, 1, d), lambda i, c: (layer, 0, 0)),
            pl.BlockSpec((None, None, d, th), lambda i, c: (layer, c, 0, 0)),
            pl.BlockSpec((None, None, d, th), lambda i, c: (layer, nc + c, 0, 0)),
            pl.BlockSpec((None, th, d), lambda i, c: (layer, c, 0)),
        ],
        out_specs=pl.BlockSpec((tm, d), lambda i, c: (i, 0)),
        out_shape=jax.ShapeDtypeStruct((rows, d), F32),
        scratch_shapes=[pltpu.VMEM(s, t) for s, t in scratch],
        compiler_params=_params(("arbitrary", "arbitrary"), _vmem_limit(blocks, scratch, temps)),
        name="ffn",
    )(x, x, mod_l, mod_l, mod_l, mod_l, mod_l, g_pre, g_post, w_gu, w_gu, w_down)


def _proj_kernel(y_ref, w_ref, x_ref, gate_ref, gpost_ref, o_ref):
    m = jnp.dot(y_ref[...], w_ref[...], preferred_element_type=F32)
    o_ref[...] = x_ref[...] + gate_ref[0] * _rms(m, gpost_ref[...])


def _proj_call(y, w, w_layer, x, mod_l, g_post, layer, trunk):
    rows, d = x.shape
    k = y.shape[1]
    tm = _row_tile(trunk)
    blocks = [((tm, k), BF16), ((k, d), BF16), ((tm, d), F32), ((tm, d), F32)]
    return pl.pallas_call(
        _proj_kernel,
        grid=(rows // tm,),
        in_specs=[
            pl.BlockSpec((tm, k), lambda i: (i, 0)),
            pl.BlockSpec((None, k, d), lambda i: (w_layer, 0, 0)),
            pl.BlockSpec((tm, d), lambda i: (i, 0)),
            pl.BlockSpec((1, 1, d), _mod_spec(trunk, tm, 2)),
            pl.BlockSpec((None, 1, d), lambda i: (layer, 0, 0)),
        ],
        out_specs=pl.BlockSpec((tm, d), lambda i: (i, 0)),
        out_shape=jax.ShapeDtypeStruct((rows, d), F32),
        compiler_params=_params(("parallel",), _vmem_limit(blocks, temps=[((tm, d), F32)] * 2)),
        name="proj",
    )(y, w, x, mod_l, g_post)


def _mixa_kernel(x_ref, shift_ref, scale_ref, gpre_ref, win_ref, gv_ref, ws_ref, bs_ref,
                 y_ref, h_sc, a_sc, *, n_blocks, n_u_blocks, chunk, groups):
    j = pl.program_id(1)

    @pl.when(j == 0)
    def _():
        h = _norm_mod(x_ref[...], gpre_ref[...], scale_ref[0], shift_ref[0])
        h_sc[...] = h.astype(BF16)

    a_sc[j] = jnp.dot(h_sc[...], win_ref[...], preferred_element_type=F32)

    @pl.when(j == n_blocks - 1)
    def _():
        tm, tn = a_sc.shape[1], a_sc.shape[2]
        width = n_u_blocks * tn
        gdim = width // groups
        ss = jnp.zeros((tm, 1), F32)
        for b in range(n_u_blocks, n_blocks):
            vb = a_sc[b]
            ss = ss + jnp.sum(vb * vb, axis=-1, keepdims=True)
        inv = lax.rsqrt(ss / width + EPS)
        for g in range(groups):
            col = g * gdim
            b, off = col // tn, col % tn
            gv = gv_ref[:, col:col + gdim]
            for c in range(tm // chunk):
                r = c * chunk
                v = (a_sc[n_u_blocks + b, r:r + chunk, off:off + gdim] * inv[r:r + chunk]) * gv
                s = jnp.dot(ws_ref[g], v.astype(BF16), preferred_element_type=F32) + bs_ref[g]
                u = a_sc[b, r:r + chunk, off:off + gdim]
                y_ref[r:r + chunk, col:col + gdim] = (u * s).astype(y_ref.dtype)


def _mixa_call(x, mod_l, g_pre, layer, w_in, g_v, w_s, b_s, a_layer, trunk):
    rows, d = x.shape
    nb, tn = w_in.shape[1], w_in.shape[3]
    width = nb * tn // 2
    groups, chunk = w_s.shape[1], w_s.shape[2]
    tm = _row_tile(trunk)
    gdim = width // groups
    assert width % tn == 0 and tn % gdim == 0 and tm % chunk == 0 and trunk.seq % chunk == 0
    blocks = [((tm, d), F32), ((d, tn), BF16), ((tm, width), BF16), ((groups, chunk, chunk), BF16),
              ((groups, chunk, V7X_LANES), F32)]
    scratch = [((tm, d), BF16), ((nb, tm, tn), F32)]
    return pl.pallas_call(
        functools.partial(_mixa_kernel, n_blocks=nb, n_u_blocks=nb // 2, chunk=chunk, groups=groups),
        grid=(rows // tm, nb),
        in_specs=[
            pl.BlockSpec((tm, d), lambda i, j: (i, 0)),
            pl.BlockSpec((1, 1, d), _mod_spec(trunk, tm, 0)),
            pl.BlockSpec((1, 1, d), _mod_spec(trunk, tm, 1)),
            pl.BlockSpec((None, 1, d), lambda i, j: (layer, 0, 0)),
            pl.BlockSpec((None, None, d, tn), lambda i, j: (a_layer, j, 0, 0)),
            pl.BlockSpec((None, 1, width), lambda i, j: (a_layer, 0, 0)),
            pl.BlockSpec((None, groups, chunk, chunk), lambda i, j: (a_layer, 0, 0, 0)),
            pl.BlockSpec((None, groups, chunk, 1), lambda i, j: (a_layer, 0, 0, 0)),
        ],
        out_specs=pl.BlockSpec((tm, width), lambda i, j: (i, 0)),
        out_shape=jax.ShapeDtypeStruct((rows, width), BF16),
        scratch_shapes=[pltpu.VMEM(s, t) for s, t in scratch],
        compiler_params=_params(("parallel", "arbitrary"),
                                _vmem_limit(blocks, scratch, temps=[((tm, d), F32), ((tm, tn), F32)])),
        name="mixa",
    )(x, mod_l, mod_l, g_pre, w_in, g_v, w_s, b_s)


def _rope_tables(seq, head_dim):
    axis_dim = head_dim // 2
    t = jnp.arange(seq)
    n_rows = seq // GRID_W
    row = jnp.minimum(t // GRID_W, n_rows - 1).astype(F32)
    col = (t % GRID_W).astype(F32)
    inv = jnp.power(ROPE_THETA, -jnp.arange(0, axis_dim, 2, dtype=F32) / axis_dim)
    ang = jnp.concatenate([row[:, None] * inv, col[:, None] * inv], axis=-1)
    cos, sin = jnp.cos(ang), jnp.sin(ang)
    zero = jnp.zeros_like(sin)
    cos2 = jnp.stack([cos, cos], axis=-1).reshape(seq, head_dim)
    sin_a = jnp.stack([-sin, zero], axis=-1).reshape(seq, head_dim)
    sin_b = jnp.stack([zero, sin], axis=-1).reshape(seq, head_dim)
    return cos2, sin_a, sin_b


def _qkv_kernel(*refs, n_q_blocks, head_dim, latent, q_scale):
    if latent:
        (x_ref, shift_ref, scale_ref, gpre_ref, w_ref, qg_ref, kg_ref, cos_ref, sa_ref, sb_ref,
         q_ref, k_ref, v_ref, h_sc) = refs
    else:
        (x_ref, shift_ref, scale_ref, gpre_ref, w_ref, qg_ref, kg_ref, q_ref, k_ref, v_ref, h_sc) = refs
    j = pl.program_id(1)

    @pl.when(j == 0)
    def _():
        h = _norm_mod(x_ref[...], gpre_ref[...], scale_ref[0], shift_ref[0])
        h_sc[...] = h.astype(BF16)

    a = jnp.dot(h_sc[...], w_ref[...], preferred_element_type=F32)
    n_heads = a.shape[1] // head_dim

    def head(hh, gain):
        blk = _rms(a[:, hh * head_dim:(hh + 1) * head_dim], gain)
        if latent:
            nxt = pltpu.roll(blk, head_dim - 1, 1)
            prv = pltpu.roll(blk, 1, 1)
            blk = blk * cos_ref[...] + nxt * sa_ref[...] + prv * sb_ref[...]
        return blk

    @pl.when(j < n_q_blocks)
    def _():
        for hh in range(n_heads):
            q_ref[:, hh * head_dim:(hh + 1) * head_dim] = (head(hh, qg_ref[...]) * q_scale).astype(q_ref.dtype)

    @pl.when(j == n_q_blocks)
    def _():
        for hh in range(n_heads):
            k_ref[:, hh * head_dim:(hh + 1) * head_dim] = head(hh, kg_ref[...]).astype(k_ref.dtype)

    @pl.when(j == n_q_blocks + 1)
    def _():
        v_ref[...] = a.astype(v_ref.dtype)


def _qkv_call(x, mod_l, g_pre, layer, w_qkv, q_gain, k_gain, b_layer, n_kv, trunk, kv_dtype):
    rows, d = x.shape
    hd = q_gain.shape[-1]
    tn = n_kv * hd
    assert w_qkv.shape[3] == tn and w_qkv.shape[1] > 2
    nqb = w_qkv.shape[1] - 2
    nq = nqb * tn
    tm = _row_tile(trunk)
    in_specs = [
        pl.BlockSpec((tm, d), lambda i, j: (i, 0)),
        pl.BlockSpec((1, 1, d), _mod_spec(trunk, tm, 0)),
        pl.BlockSpec((1, 1, d), _mod_spec(trunk, tm, 1)),
        pl.BlockSpec((None, 1, d), lambda i, j: (layer, 0, 0)),
        pl.BlockSpec((None, None, d, tn), lambda i, j: (b_layer, j, 0, 0)),
        pl.BlockSpec((None, 1, hd), lambda i, j: (b_layer, 0, 0)),
        pl.BlockSpec((None, 1, hd), lambda i, j: (b_layer, 0, 0)),
    ]
    args = [x, mod_l, mod_l, g_pre, w_qkv, q_gain, k_gain]
    if trunk.latent:
        assert trunk.seq % tm == 0 and trunk.seq % GRID_W == 0
        tiles_per_seq = trunk.seq // tm
        in_specs += [pl.BlockSpec((tm, hd), lambda i, j: (i % tiles_per_seq, 0))] * 3
        args += list(_rope_tables(trunk.seq, hd))
    blocks = [((tm, d), F32), ((d, tn), BF16), ((tm, tn), BF16), ((tm, tn), kv_dtype), ((tm, tn), kv_dtype),
              ((tm, hd), F32), ((tm, hd), F32), ((tm, hd), F32)]
    scratch = [((tm, d), BF16)]
    return pl.pallas_call(
        functools.partial(_qkv_kernel, n_q_blocks=nqb, head_dim=hd, latent=trunk.latent,
                          q_scale=hd ** -0.5 * LOG2_E),
        grid=(rows // tm, nqb + 2),
        in_specs=in_specs,
        out_specs=[
            pl.BlockSpec((tm, tn), lambda i, j: (i, jnp.minimum(j, nqb - 1))),
            pl.BlockSpec((tm, tn), lambda i, j: (i, 0)),
            pl.BlockSpec((tm, tn), lambda i, j: (i, 0)),
        ],
        out_shape=[
            jax.ShapeDtypeStruct((rows, nq), BF16),
            jax.ShapeDtypeStruct((rows, tn), kv_dtype),
            jax.ShapeDtypeStruct((rows, tn), kv_dtype),
        ],
        scratch_shapes=[pltpu.VMEM(s, t) for s, t in scratch],
        compiler_params=_params(("parallel", "arbitrary"),
                                _vmem_limit(blocks, scratch, temps=[((tm, d), F32), ((tm, tn), F32)])),
        name="qkv",
    )(*args)


def _attn_kernel(q_ref, k_ref, v_ref, o_ref, *, q_per_kv, head_dim):
    hd = head_dim
    tq, tk = q_ref.shape[1], k_ref.shape[1]
    n_groups = ATTN_HEAD_GROUPS if q_per_kv % ATTN_HEAD_GROUPS == 0 else 1
    per = q_per_kv // n_groups
    k = k_ref[0].astype(BF16)
    v1 = jnp.concatenate([v_ref[0].astype(BF16), jnp.ones((tk, hd), BF16)], axis=1)
    qs = [jnp.concatenate([q_ref[0, :, g * hd:(g + 1) * hd] for g in range(i * per, (i + 1) * per)], axis=0)
          for i in range(n_groups)]
    ss = [lax.dot_general(q, k, (((1,), (1,)), ((), ())), preferred_element_type=F32) for q in qs]
    m_curs = [jnp.max(s, axis=-1, keepdims=True) for s in ss]

    def write(i, acc):
        out = acc[:, :hd] / acc[:, hd:]
        for j in range(per):
            g = i * per + j
            o_ref[0, :, g * hd:(g + 1) * hd] = out[j * tq:(j + 1) * tq].astype(o_ref.dtype)

    ps = [jnp.exp2(s - m).astype(BF16) for s, m in zip(ss, m_curs)]
    accs = [jnp.dot(p, v1, preferred_element_type=F32) for p in ps]
    for i, acc in enumerate(accs):
        write(i, acc)


def _attn_call(q, k, v, n_kv):
    b, t, nq = q.shape
    s = k.shape[1]
    hd = k.shape[2] // n_kv
    gw = nq // n_kv
    tq = min(ATTN_Q_TILE, t)
    assert t % tq == 0 and s <= ATTN_MAX_KEYS and s % V7X_LANES == 0 and hd % V7X_LANES == 0
    rows = (gw // hd) * tq
    blocks = [((1, tq, gw), BF16), ((1, s, hd), k.dtype), ((1, s, hd), v.dtype), ((1, tq, gw), BF16)]
    temps = [((rows, s), F32)] * 2 + [((rows, s), BF16), ((s, 2 * hd), BF16), ((rows, 2 * hd), F32)]
    return pl.pallas_call(
        functools.partial(_attn_kernel, q_per_kv=gw // hd, head_dim=hd),
        grid=(b, n_kv, t // tq),
        in_specs=[
            pl.BlockSpec((1, tq, gw), lambda bi, h, qi: (bi, qi, h)),
            pl.BlockSpec((1, s, hd), lambda bi, h, qi: (bi, 0, h)),
            pl.BlockSpec((1, s, hd), lambda bi, h, qi: (bi, 0, h)),
        ],
        out_specs=pl.BlockSpec((1, tq, gw), lambda bi, h, qi: (bi, qi, h)),
        out_shape=jax.ShapeDtypeStruct((b, t, nq), BF16),
        compiler_params=_params(("parallel", "parallel", "parallel"), _vmem_limit(blocks, temps=temps)),
        name="attn",
    )(q, k, v)


def _pool_kernel(x_ref, xp_ref, xn_ref, shift_ref, scale_ref, gate_ref, gpre_ref, gpost_ref, w_ref, cs_ref,
                 o_ref, h_sc, ta_sc, tb_sc, m_sc, *, tiles_per_seq):
    tm, d = x_ref.shape
    pad = POOL_HALO
    gdim = d // len(POOL_WINDOWS)
    gpre, scale, shift = gpre_ref[...], scale_ref[0], shift_ref[0]
    t_in_seq = lax.rem(pl.program_id(0), tiles_per_seq)
    keep_prev = (t_in_seq > 0).astype(F32)
    keep_next = (t_in_seq < tiles_per_seq - 1).astype(F32)
    zeros = jnp.zeros((pad, d), F32)
    h_sc[0:pad, :] = zeros
    h_sc[pad:2 * pad, :] = _norm_mod(xp_ref[...], gpre, scale, shift) * keep_prev
    h_sc[2 * pad:2 * pad + tm, :] = _norm_mod(x_ref[...], gpre, scale, shift)
    h_sc[2 * pad + tm:3 * pad + tm, :] = _norm_mod(xn_ref[...], gpre, scale, shift) * keep_next
    h_sc[3 * pad + tm:, :] = zeros
    for t_sc in (ta_sc, tb_sc):
        t_sc[0:pad, :] = zeros[:, :gdim]
        t_sc[3 * pad + tm:, :] = zeros[:, :gdim]

    ext = tm + 2 * pad

    def shifted_sum(src, cols, lo, hi):
        return src[pl.ds(pad + lo, ext), cols] + src[pl.ds(pad + hi, ext), cols]

    pos = t_in_seq * tm + lax.broadcasted_iota(jnp.int32, (tm, 1), 0)
    seq = tiles_per_seq * tm
    all_cols = pl.ds(0, gdim)
    for j, w in enumerate(POOL_WINDOWS):
        half = w // 2
        cols = pl.ds(j * gdim, gdim)
        src, src_cols, span = h_sc, cols, 1
        for dst in (ta_sc, tb_sc, ta_sc):
            if span * 2 >= w:
                break
            lo, hi = (-1, 0) if span == 1 else (-(span // 2), span // 2)
            dst[pl.ds(pad, ext), :] = shifted_sum(src, src_cols, lo, hi)
            src, src_cols, span = dst, all_cols, span * 2
        lo, hi = (-1, 0) if span == 1 else (-(span // 2), span // 2)
        win = src[pl.ds(2 * pad + lo, tm), src_cols] + src[pl.ds(2 * pad + hi, tm), src_cols]
        cnt = (jnp.minimum(pos + half, seq) - jnp.maximum(pos - half, 0)).astype(F32)
        p = win * (1.0 / cnt) - h_sc[pl.ds(2 * pad, tm), cols]
        y = jnp.dot(p.astype(BF16), w_ref[j], preferred_element_type=F32)
        m_sc[:, cols] = y * cs_ref[:, cols]
    o_ref[...] = x_ref[...] + gate_ref[0] * _rms(m_sc[...], gpost_ref[...])


def _pool_call(x, mod_l, g_pre, g_post, layer, w_pool, c_scale, c_layer, trunk):
    rows, d = x.shape
    tm = min(_row_tile(trunk), trunk.seq)
    assert trunk.seq % tm == 0 and tm % POOL_HALO == 0 and max(POOL_WINDOWS) // 2 <= POOL_HALO
    groups, gdim = w_pool.shape[1], w_pool.shape[2]
    assert groups == len(POOL_WINDOWS) and POOL_WINDOWS == (2, 4, 8, 16)
    halo_per_tile = tm // POOL_HALO
    last_halo = rows // POOL_HALO - 1
    blocks = [((tm, d), F32), ((tm, d), F32), ((groups, gdim, gdim), BF16)]
    ext_rows = tm + 4 * POOL_HALO
    scratch = [((ext_rows, d), F32), ((ext_rows, gdim), F32), ((ext_rows, gdim), F32), ((tm, d), F32)]
    return pl.pallas_call(
        functools.partial(_pool_kernel, tiles_per_seq=trunk.seq // tm),
        grid=(rows // tm,),
        in_specs=[
            pl.BlockSpec((tm, d), lambda i: (i, 0)),
            pl.BlockSpec((POOL_HALO, d), lambda i: (jnp.maximum(i * halo_per_tile - 1, 0), 0)),
            pl.BlockSpec((POOL_HALO, d), lambda i: (jnp.minimum((i + 1) * halo_per_tile, last_halo), 0)),
            pl.BlockSpec((1, 1, d), _mod_spec(trunk, tm, 0)),
            pl.BlockSpec((1, 1, d), _mod_spec(trunk, tm, 1)),
            pl.BlockSpec((1, 1, d), _mod_spec(trunk, tm, 2)),
            pl.BlockSpec((None, 1, d), lambda i: (layer, 0, 0)),
            pl.BlockSpec((None, 1, d), lambda i: (layer, 0, 0)),
            pl.BlockSpec((None, groups, gdim, gdim), lambda i: (c_layer, 0, 0, 0)),
            pl.BlockSpec((None, 1, d), lambda i: (c_layer, 0, 0)),
        ],
        out_specs=pl.BlockSpec((tm, d), lambda i: (i, 0)),
        out_shape=jax.ShapeDtypeStruct((rows, d), F32),
        scratch_shapes=[pltpu.VMEM(s, t) for s, t in scratch],
        compiler_params=_params(("parallel",), _vmem_limit(blocks, scratch, temps=[((tm, d), F32)] * 2)),
        name="pool",
    )(x, x, x, mod_l, mod_l, mod_l, g_pre, g_post, w_pool, c_scale)


def _trunk(x3, trunk, mod, cache_kv, p):
    b, t, d = x3.shape
    x = x3.reshape(b * t, d)
    depth = p["norm_mix_pre"].shape[0]
    n_kv = p["n_kv"]
    ia = ib = ic = 0
    new_k, new_v = [], []
    for l in range(depth):
        mod_l = mod[l]
        g_pre, g_post = p["norm_mix_pre"], p["norm_mix_post"]
        kind = l % 3
        if kind == 0:
            y = _mixa_call(x, mod_l, g_pre, l, p["a_w_in"], p["a_norm_v"], p["a_w_s"], p["a_b_s"], ia, trunk)
            x = _proj_call(y, p["a_w_out"], ia, x, mod_l, g_post, l, trunk)
            ia += 1
        elif kind == 1:
            kv_dtype = BF16 if trunk.latent else F32
            q, k, v = _qkv_call(x, mod_l, g_pre, l, p["b_w_qkv"], p["b_q_norm"], p["b_k_norm"], ib, n_kv, trunk,
                                kv_dtype)
            k3, v3 = k.reshape(b, t, -1), v.reshape(b, t, -1)
            if trunk.latent:
                ck, cv = cache_kv
                past = ck.shape[2]
                k3 = jnp.concatenate([k3, ck[:, ib].reshape(b, past, -1).astype(BF16)], axis=1)
                v3 = jnp.concatenate([v3, cv[:, ib].reshape(b, past, -1).astype(BF16)], axis=1)
            else:
                new_k.append(k3.reshape(b, t, n_kv, -1))
                new_v.append(v3.reshape(b, t, n_kv, -1))
            o = _attn_call(q.reshape(b, t, -1), k3, v3, n_kv)
            x = _proj_call(o.reshape(b * t, -1), p["b_w_o"], ib, x, mod_l, g_post, l, trunk)
            ib += 1
        else:
            x = _pool_call(x, mod_l, g_pre, g_post, l, p["c_w_pool"], p["c_scale"], ic, trunk)
            ic += 1
        x = _ffn_call(x, mod_l, p["norm_ffn_pre"], p["norm_ffn_post"], p["f_w_gu"], p["f_w_down"], l, trunk)
    return x.reshape(b, t, d), new_k, new_v


def kernel(x_prompt, x_sample, cache_k, cache_v, c, c_ctx, w_mod, b_mod, norm_mix_pre, norm_mix_post, norm_ffn_pre, norm_ffn_post, a_w_in, a_norm_v, a_w_s, a_b_s, a_w_out, b_w_qkv, b_q_norm, b_k_norm, b_w_o, c_w_pool, c_scale, f_w_gu, f_w_down):
    batch, seq, d = x_prompt.shape
    dec_batch, dec_seq, _ = x_sample.shape
    depth = w_mod.shape[0]
    assert 1 + dec_batch <= MOD_ROWS

    cond = jnp.concatenate([c_ctx[None, :], c, jnp.zeros((MOD_ROWS - 1 - dec_batch, d), F32)], axis=0)
    mod = _mod_call(cond, w_mod, b_mod).reshape(depth, MOD_ROWS * N_MOD, 1, d)

    rows3 = lambda a: a.reshape(a.shape[0], 1, a.shape[-1])
    n_kv, head_dim = cache_k.shape[3], cache_k.shape[4]
    p = {
        "n_kv": n_kv,
        "norm_mix_pre": rows3(norm_mix_pre), "norm_mix_post": rows3(norm_mix_post),
        "norm_ffn_pre": rows3(norm_ffn_pre), "norm_ffn_post": rows3(norm_ffn_post),
        "a_w_in": _col_blocks(a_w_in, COL_TILE), "a_norm_v": rows3(a_norm_v), "a_w_s": a_w_s.astype(BF16),
        "a_b_s": a_b_s[..., None], "a_w_out": a_w_out.astype(BF16),
        "b_w_qkv": _col_blocks(b_w_qkv, n_kv * head_dim), "b_q_norm": rows3(b_q_norm), "b_k_norm": rows3(b_k_norm),
        "b_w_o": b_w_o.astype(BF16),
        "c_w_pool": c_w_pool.astype(BF16), "c_scale": rows3(c_scale),
        "f_w_gu": _col_blocks(f_w_gu, FFN_HIDDEN_TILE), "f_w_down": f_w_down.astype(BF16),
    }
    ctx = Trunk(batch=batch, seq=seq, mod_base=0, rows_per_cond=batch * seq, latent=False)
    lat = Trunk(batch=dec_batch, seq=dec_seq, mod_base=1, rows_per_cond=dec_seq, latent=True)
    y_prompt, new_k, new_v = _trunk(x_prompt, ctx, mod, None, p)
    y_sample, _, _ = _trunk(x_sample, lat, mod, (cache_k, cache_v), p)
    return (y_prompt, y_sample, jnp.stack(new_k, axis=1), jnp.stack(new_v, axis=1))
```

```python
import collections
import functools

import jax
import jax.numpy as jnp
from jax import lax
from jax.experimental import pallas as pl
from jax.experimental.pallas import tpu as pltpu

F32 = jnp.float32
BF16 = jnp.bfloat16

EPS = 1e-6
N_MOD = 6
GRID_W = 64
ROPE_THETA = 10000.0
POOL_WINDOWS = (2, 4, 8, 16)
POOL_HALO = 8
MOD_ROWS = 16
LOG2_E = 1.4426950408889634
V7X_LANES = 128

V7X_VMEM_BUDGET = 58 * 1024 * 1024
ROW_TILE = 512
FFN_HIDDEN_TILE = 512
EARLY_FETCH_STEP = 3
FFN_XN_FETCH_STEP = 2
FFN_X_FETCH_STEP = 4
COL_TILE = 512
ATTN_Q_TILE = 256
ATTN_MAX_KEYS = 4608
ATTN_HEAD_GROUPS = 2
MOD_COL_TILE = 1024

Trunk = collections.namedtuple("Trunk", "batch seq mod_base rows_per_cond latent")


def _nbytes(shape, dtype):
    n = 1
    for s in shape:
        n *= s
    return n * jnp.dtype(dtype).itemsize


def _vmem_limit(pipelined, scratch=(), temps=()):
    total = 2 * sum(_nbytes(s, d) for s, d in pipelined)
    total += sum(_nbytes(s, d) for s, d in scratch)
    total += sum(_nbytes(s, d) for s, d in temps)
    return int(min(max(total + (4 << 20), 16 << 20), V7X_VMEM_BUDGET))


def _params(semantics, limit):
    return pltpu.CompilerParams(dimension_semantics=semantics, vmem_limit_bytes=limit)


def _rms(xf, g):
    ms = jnp.mean(xf * xf, axis=-1, keepdims=True)
    return (xf * lax.rsqrt(ms + EPS)) * g


def _norm_mod(xf, g, scale, shift):
    return _rms(xf, g) * (1.0 + scale) + shift


def _row_tile(trunk):
    rows = trunk.batch * trunk.seq
    tm = min(ROW_TILE, rows)
    assert rows % tm == 0 and (trunk.rows_per_cond % tm == 0)
    return tm


def _mod_spec(trunk, tm, m, ahead=0):
    last_tile = trunk.batch * trunk.seq // tm - 1

    def index(i, *_):
        tile = jnp.minimum(i + ahead, last_tile)
        return ((trunk.mod_base + (tile * tm) // trunk.rows_per_cond) * N_MOD + m, 0, 0)
    return index


def _rows_read_at_step0(trunk, tm, n_steps):
    last_tile = trunk.batch * trunk.seq // tm - 1
    switch = min(EARLY_FETCH_STEP, n_steps - 1)
    if switch < 1:
        return lambda i, j: (i, 0)
    return lambda i, j: (jnp.minimum(jnp.where(j >= switch, i + 1, i), last_tile), 0)


def _mod_kernel(c_ref, w_ref, b_ref, o_ref):
    c = c_ref[...]
    s = (c * jax.nn.sigmoid(c)).astype(BF16)
    o_ref[0] = jnp.dot(s, w_ref[0].astype(BF16), preferred_element_type=F32) + b_ref[0]


def _mod_call(cond, w_mod, b_mod):
    depth, d, n = w_mod.shape
    tn = min(MOD_COL_TILE, n)
    assert n % tn == 0
    blocks = [((MOD_ROWS, d), F32), ((1, d, tn), F32), ((1, 1, tn), F32), ((1, MOD_ROWS, tn), F32)]
    return pl.pallas_call(
        _mod_kernel,
        grid=(depth, n // tn),
        in_specs=[
            pl.BlockSpec((MOD_ROWS, d), lambda l, j: (0, 0)),
            pl.BlockSpec((1, d, tn), lambda l, j: (l, 0, j)),
            pl.BlockSpec((1, 1, tn), lambda l, j: (l, 0, j)),
        ],
        out_specs=pl.BlockSpec((1, MOD_ROWS, tn), lambda l, j: (l, 0, j)),
        out_shape=jax.ShapeDtypeStruct((depth, MOD_ROWS, n), F32),
        compiler_params=_params(("parallel", "parallel"), _vmem_limit(blocks, temps=[((d, tn), BF16)])),
        name="mod",
    )(cond, w_mod, b_mod.reshape(depth, 1, n))


def _ffn_kernel(x_ref, xn_ref, shift_ref, scale_ref, shiftn_ref, scalen_ref, gate_ref, gpre_ref, gpost_ref,
                wg_ref, wu_ref, wd_ref, o_ref, h_sc, hn_sc, acc_sc, *, n_chunks):
    i, c = pl.program_id(0), pl.program_id(1)

    @pl.when(c == 0)
    def _():
        @pl.when(i == 0)
        def _():
            h_sc[...] = _norm_mod(x_ref[...], gpre_ref[...], scale_ref[0], shift_ref[0]).astype(BF16)

        @pl.when(i > 0)
        def _():
            h_sc[...] = hn_sc[...]

        acc_sc[...] = jnp.zeros_like(acc_sc)

    def chunk():
        h = h_sc[...]
        g = jnp.dot(h, wg_ref[...], preferred_element_type=F32)
        u = jnp.dot(h, wu_ref[...], preferred_element_type=F32)
        a = (g * jax.nn.sigmoid(g)) * u
        return jnp.dot(a.astype(BF16), wd_ref[...], preferred_element_type=F32)

    @pl.when(c < n_chunks - 1)
    def _():
        acc_sc[...] += chunk()

    @pl.when(c == n_chunks - 1)
    def _():
        hn_sc[...] = _norm_mod(xn_ref[...], gpre_ref[...], scalen_ref[0], shiftn_ref[0]).astype(BF16)
        f = acc_sc[...] + chunk()
        o_ref[...] = x_ref[...] + gate_ref[0] * _rms(f, gpost_ref[...])


def _ffn_call(x, mod_l, g_pre, g_post, w_gu, w_down, layer, trunk):
    rows, d = x.shape
    hidden = w_down.shape[1]
    tm = _row_tile(trunk)
    th = min(FFN_HIDDEN_TILE, hidden)
    assert hidden % th == 0
    nc = hidden // th
    blocks = [((tm, d), F32)] * 3 + [((d, th), BF16), ((d, th), BF16), ((th, d), BF16)]
    scratch = [((tm, d), BF16), ((tm, d), BF16), ((tm, d), F32)]
    temps = [((tm, th), F32)] * 4 + [((tm, d), F32)]
    last_tile = rows // tm - 1
    x_step, xn_step = min(FFN_X_FETCH_STEP, nc - 1), min(FFN_XN_FETCH_STEP, nc - 1)

    def x_index(i, c):
        return (jnp.where(c >= x_step, i, jnp.maximum(i - 1, 0)), 0)

    def xn_index(i, c):
        return (jnp.minimum(jnp.where(c >= xn_step, i + 1, i), last_tile), 0)
    return pl.pallas_call(
        functools.partial(_ffn_kernel, n_chunks=nc),
        grid=(rows // tm, nc),
        in_specs=[
            pl.BlockSpec((tm, d), x_index),
            pl.BlockSpec((tm, d), xn_index),
            pl.BlockSpec((1, 1, d), _mod_spec(trunk, tm, 3)),
            pl.BlockSpec((1, 1, d), _mod_spec(trunk, tm, 4)),
            pl.BlockSpec((1, 1, d), _mod_spec(trunk, tm, 3, ahead=1)),
            pl.BlockSpec((1, 1, d), _mod_spec(trunk, tm, 4, ahead=1)),
            pl.BlockSpec((1, 1, d), _mod_spec(trunk, tm, 5)),
            pl.BlockSpec((None, 1, d), lambda i, c: (layer, 0, 0)),
            pl.BlockSpec((None, 1, d), lambda i, c: (layer, 0, 0)),
            pl.BlockSpec((None, d, th), lambda i, c: (layer, 0, c)),
            pl.BlockSpec((None, d, th), lambda i, c: (layer, 0, nc + c)),
            pl.BlockSpec((None, th, d), lambda i, c: (layer, c, 0)),
        ],
        out_specs=pl.BlockSpec((tm, d), lambda i, c: (i, 0)),
        out_shape=jax.ShapeDtypeStruct((rows, d), F32),
        scratch_shapes=[pltpu.VMEM(s, t) for s, t in scratch],
        compiler_params=_params(("arbitrary", "arbitrary"), _vmem_limit(blocks, scratch, temps)),
        name="ffn",
    )(x, x, mod_l, mod_l, mod_l, mod_l, mod_l, g_pre, g_post, w_gu, w_gu, w_down)


def _proj_kernel(y_ref, w_ref, x_ref, gate_ref, gpost_ref, o_ref):
    m = jnp.dot(y_ref[...], w_ref[...], preferred_element_type=F32)
    o_ref[...] = x_ref[...] + gate_ref[0] * _rms(m, gpost_ref[...])


def _proj_call(y, w, w_layer, x, mod_l, g_post, layer, trunk):
    rows, d = x.shape
    k = y.shape[1]
    tm = _row_tile(trunk)
    blocks = [((tm, k), BF16), ((k, d), BF16), ((tm, d), F32), ((tm, d), F32)]
    return pl.pallas_call(
        _proj_kernel,
        grid=(rows // tm,),
        in_specs=[
            pl.BlockSpec((tm, k), lambda i: (i, 0)),
            pl.BlockSpec((None, k, d), lambda i: (w_layer, 0, 0)),
            pl.BlockSpec((tm, d), lambda i: (i, 0)),
            pl.BlockSpec((1, 1, d), _mod_spec(trunk, tm, 2)),
            pl.BlockSpec((None, 1, d), lambda i: (layer, 0, 0)),
        ],
        out_specs=pl.BlockSpec((tm, d), lambda i: (i, 0)),
        out_shape=jax.ShapeDtypeStruct((rows, d), F32),
        compiler_params=_params(("parallel",), _vmem_limit(blocks, temps=[((tm, d), F32)] * 2)),
        name="proj",
    )(y, w, x, mod_l, g_post)


def _mixa_kernel(x_ref, shift_ref, scale_ref, gpre_ref, win_ref, gv_ref, ws_ref, bs_ref,
                 y_ref, h_sc, a_sc, *, n_blocks, n_u_blocks, chunk, groups):
    j = pl.program_id(1)

    @pl.when(j == 0)
    def _():
        h = _norm_mod(x_ref[...], gpre_ref[...], scale_ref[0], shift_ref[0])
        h_sc[...] = h.astype(BF16)

    a_sc[j] = jnp.dot(h_sc[...], win_ref[...], preferred_element_type=F32)

    @pl.when(j == n_blocks - 1)
    def _():
        tm, tn = a_sc.shape[1], a_sc.shape[2]
        width = n_u_blocks * tn
        gdim = width // groups
        ss = jnp.zeros((tm, 1), F32)
        for b in range(n_u_blocks, n_blocks):
            vb = a_sc[b]
            ss = ss + jnp.sum(vb * vb, axis=-1, keepdims=True)
        inv = lax.rsqrt(ss / width + EPS)
        for g in range(groups):
            col = g * gdim
            b, off = col // tn, col % tn
            gv = gv_ref[:, col:col + gdim]
            for c in range(tm // chunk):
                r = c * chunk
                v = (a_sc[n_u_blocks + b, r:r + chunk, off:off + gdim] * inv[r:r + chunk]) * gv
                s = jnp.dot(ws_ref[g], v.astype(BF16), preferred_element_type=F32) + bs_ref[g]
                u = a_sc[b, r:r + chunk, off:off + gdim]
                y_ref[r:r + chunk, col:col + gdim] = (u * s).astype(y_ref.dtype)


def _mixa_call(x, mod_l, g_pre, layer, w_in, g_v, w_s, b_s, a_layer, trunk):
    rows, d = x.shape
    width = w_in.shape[2] // 2
    groups, chunk = w_s.shape[1], w_s.shape[2]
    tm = _row_tile(trunk)
    tn = min(COL_TILE, width)
    gdim = width // groups
    assert width % tn == 0 and tn % gdim == 0 and tm % chunk == 0 and trunk.seq % chunk == 0
    nb = 2 * width // tn
    blocks = [((tm, d), F32), ((d, tn), BF16), ((tm, width), BF16), ((groups, chunk, chunk), BF16),
              ((groups, chunk, V7X_LANES), F32)]
    scratch = [((tm, d), BF16), ((nb, tm, tn), F32)]
    return pl.pallas_call(
        functools.partial(_mixa_kernel, n_blocks=nb, n_u_blocks=nb // 2, chunk=chunk, groups=groups),
        grid=(rows // tm, nb),
        in_specs=[
            pl.BlockSpec((tm, d), _rows_read_at_step0(trunk, tm, nb)),
            pl.BlockSpec((1, 1, d), _mod_spec(trunk, tm, 0)),
            pl.BlockSpec((1, 1, d), _mod_spec(trunk, tm, 1)),
            pl.BlockSpec((None, 1, d), lambda i, j: (layer, 0, 0)),
            pl.BlockSpec((None, d, tn), lambda i, j: (a_layer, 0, j)),
            pl.BlockSpec((None, 1, width), lambda i, j: (a_layer, 0, 0)),
            pl.BlockSpec((None, groups, chunk, chunk), lambda i, j: (a_layer, 0, 0, 0)),
            pl.BlockSpec((None, groups, chunk, 1), lambda i, j: (a_layer, 0, 0, 0)),
        ],
        out_specs=pl.BlockSpec((tm, width), lambda i, j: (i, 0)),
        out_shape=jax.ShapeDtypeStruct((rows, width), BF16),
        scratch_shapes=[pltpu.VMEM(s, t) for s, t in scratch],
        compiler_params=_params(("parallel", "arbitrary"),
                                _vmem_limit(blocks, scratch, temps=[((tm, d), F32), ((tm, tn), F32)])),
        name="mixa",
    )(x, mod_l, mod_l, g_pre, w_in, g_v, w_s, b_s)


def _rope_tables(seq, head_dim):
    axis_dim = head_dim // 2
    t = jnp.arange(seq)
    n_rows = seq // GRID_W
    row = jnp.minimum(t // GRID_W, n_rows - 1).astype(F32)
    col = (t % GRID_W).astype(F32)
    inv = jnp.power(ROPE_THETA, -jnp.arange(0, axis_dim, 2, dtype=F32) / axis_dim)
    ang = jnp.concatenate([row[:, None] * inv, col[:, None] * inv], axis=-1)
    cos, sin = jnp.cos(ang), jnp.sin(ang)
    zero = jnp.zeros_like(sin)
    cos2 = jnp.stack([cos, cos], axis=-1).reshape(seq, head_dim)
    sin_a = jnp.stack([-sin, zero], axis=-1).reshape(seq, head_dim)
    sin_b = jnp.stack([zero, sin], axis=-1).reshape(seq, head_dim)
    return cos2, sin_a, sin_b


def _qkv_kernel(*refs, n_q_blocks, head_dim, latent, q_scale):
    if latent:
        (x_ref, shift_ref, scale_ref, gpre_ref, w_ref, qg_ref, kg_ref, cos_ref, sa_ref, sb_ref,
         q_ref, k_ref, v_ref, h_sc) = refs
    else:
        (x_ref, shift_ref, scale_ref, gpre_ref, w_ref, qg_ref, kg_ref, q_ref, k_ref, v_ref, h_sc) = refs
    j = pl.program_id(1)

    @pl.when(j == 0)
    def _():
        h = _norm_mod(x_ref[...], gpre_ref[...], scale_ref[0], shift_ref[0])
        h_sc[...] = h.astype(BF16)

    a = jnp.dot(h_sc[...], w_ref[...], preferred_element_type=F32)
    n_heads = a.shape[1] // head_dim

    def head(hh, gain):
        blk = _rms(a[:, hh * head_dim:(hh + 1) * head_dim], gain)
        if latent:
            nxt = pltpu.roll(blk, head_dim - 1, 1)
            prv = pltpu.roll(blk, 1, 1)
            blk = blk * cos_ref[...] + nxt * sa_ref[...] + prv * sb_ref[...]
        return blk

    @pl.when(j < n_q_blocks)
    def _():
        for hh in range(n_heads):
            q_ref[:, hh * head_dim:(hh + 1) * head_dim] = (head(hh, qg_ref[...]) * q_scale).astype(q_ref.dtype)

    @pl.when(j == n_q_blocks)
    def _():
        for hh in range(n_heads):
            k_ref[:, hh * head_dim:(hh + 1) * head_dim] = head(hh, kg_ref[...]).astype(k_ref.dtype)

    @pl.when(j == n_q_blocks + 1)
    def _():
        v_ref[...] = a.astype(v_ref.dtype)


def _qkv_call(x, mod_l, g_pre, layer, w_qkv, q_gain, k_gain, b_layer, n_kv, trunk, kv_dtype):
    rows, d = x.shape
    hd = q_gain.shape[-1]
    tn = n_kv * hd
    nq = w_qkv.shape[2] - 2 * tn
    assert nq % tn == 0
    nqb = nq // tn
    tm = _row_tile(trunk)
    in_specs = [
        pl.BlockSpec((tm, d), _rows_read_at_step0(trunk, tm, nqb + 2)),
        pl.BlockSpec((1, 1, d), _mod_spec(trunk, tm, 0)),
        pl.BlockSpec((1, 1, d), _mod_spec(trunk, tm, 1)),
        pl.BlockSpec((None, 1, d), lambda i, j: (layer, 0, 0)),
        pl.BlockSpec((None, d, tn), lambda i, j: (b_layer, 0, j)),
        pl.BlockSpec((None, 1, hd), lambda i, j: (b_layer, 0, 0)),
        pl.BlockSpec((None, 1, hd), lambda i, j: (b_layer, 0, 0)),
    ]
    args = [x, mod_l, mod_l, g_pre, w_qkv, q_gain, k_gain]
    if trunk.latent:
        assert trunk.seq % tm == 0 and trunk.seq % GRID_W == 0
        tiles_per_seq = trunk.seq // tm
        in_specs += [pl.BlockSpec((tm, hd), lambda i, j: (i % tiles_per_seq, 0))] * 3
        args += list(_rope_tables(trunk.seq, hd))
    blocks = [((tm, d), F32), ((d, tn), BF16), ((tm, tn), BF16), ((tm, tn), kv_dtype), ((tm, tn), kv_dtype),
              ((tm, hd), F32), ((tm, hd), F32), ((tm, hd), F32)]
    scratch = [((tm, d), BF16)]
    return pl.pallas_call(
        functools.partial(_qkv_kernel, n_q_blocks=nqb, head_dim=hd, latent=trunk.latent,
                          q_scale=hd ** -0.5 * LOG2_E),
        grid=(rows // tm, nqb + 2),
        in_specs=in_specs,
        out_specs=[
            pl.BlockSpec((tm, tn), lambda i, j: (i, jnp.minimum(j, nqb - 1))),
            pl.BlockSpec((tm, tn), lambda i, j: (i, 0)),
            pl.BlockSpec((tm, tn), lambda i, j: (i, 0)),
        ],
        out_shape=[
            jax.ShapeDtypeStruct((rows, nq), BF16),
            jax.ShapeDtypeStruct((rows, tn), kv_dtype),
            jax.ShapeDtypeStruct((rows, tn), kv_dtype),
        ],
        scratch_shapes=[pltpu.VMEM(s, t) for s, t in scratch],
        compiler_params=_params(("parallel", "arbitrary"),
                                _vmem_limit(blocks, scratch, temps=[((tm, d), F32), ((tm, tn), F32)])),
        name="qkv",
    )(*args)


def _attn_kernel(q_ref, k_ref, v_ref, o_ref, *, q_per_kv, head_dim):
    hd = head_dim
    tq, tk = q_ref.shape[1], k_ref.shape[1]
    n_groups = ATTN_HEAD_GROUPS if q_per_kv % ATTN_HEAD_GROUPS == 0 else 1
    per = q_per_kv // n_groups
    k = k_ref[0].astype(BF16)
    v1 = jnp.concatenate([v_ref[0].astype(BF16), jnp.ones((tk, hd), BF16)], axis=1)
    qs = [jnp.concatenate([q_ref[0, :, g * hd:(g + 1) * hd] for g in range(i * per, (i + 1) * per)], axis=0)
          for i in range(n_groups)]
    ss = [lax.dot_general(q, k, (((1,), (1,)), ((), ())), preferred_element_type=F32) for q in qs]
    m_curs = [jnp.max(s, axis=-1, keepdims=True) for s in ss]

    def write(i, acc):
        out = acc[:, :hd] / acc[:, hd:]
        for j in range(per):
            g = i * per + j
            o_ref[0, :, g * hd:(g + 1) * hd] = out[j * tq:(j + 1) * tq].astype(o_ref.dtype)

    ps = [jnp.exp2(s - m).astype(BF16) for s, m in zip(ss, m_curs)]
    accs = [jnp.dot(p, v1, preferred_element_type=F32) for p in ps]
    for i, acc in enumerate(accs):
        write(i, acc)


def _attn_call(q, k, v, n_kv):
    b, t, nq = q.shape
    s = k.shape[1]
    hd = k.shape[2] // n_kv
    gw = nq // n_kv
    tq = min(ATTN_Q_TILE, t)
    assert t % tq == 0 and s <= ATTN_MAX_KEYS and s % V7X_LANES == 0 and hd % V7X_LANES == 0
    rows = (gw // hd) * tq
    blocks = [((1, tq, gw), BF16), ((1, s, hd), k.dtype), ((1, s, hd), v.dtype), ((1, tq, gw), BF16)]
    temps = [((rows, s), F32)] * 2 + [((rows, s), BF16), ((s, 2 * hd), BF16), ((rows, 2 * hd), F32)]
    return pl.pallas_call(
        functools.partial(_attn_kernel, q_per_kv=gw // hd, head_dim=hd),
        grid=(b, n_kv, t // tq),
        in_specs=[
            pl.BlockSpec((1, tq, gw), lambda bi, h, qi: (bi, qi, h)),
            pl.BlockSpec((1, s, hd), lambda bi, h, qi: (bi, 0, h)),
            pl.BlockSpec((1, s, hd), lambda bi, h, qi: (bi, 0, h)),
        ],
        out_specs=pl.BlockSpec((1, tq, gw), lambda bi, h, qi: (bi, qi, h)),
        out_shape=jax.ShapeDtypeStruct((b, t, nq), BF16),
        compiler_params=_params(("parallel", "parallel", "parallel"), _vmem_limit(blocks, temps=temps)),
        name="attn",
    )(q, k, v)


def _pool_kernel(x_ref, xp_ref, xn_ref, shift_ref, scale_ref, gate_ref, gpre_ref, gpost_ref, w_ref, cs_ref,
                 o_ref, h_sc, ta_sc, tb_sc, m_sc, *, tiles_per_seq):
    tm, d = x_ref.shape
    pad = POOL_HALO
    gdim = d // len(POOL_WINDOWS)
    gpre, scale, shift = gpre_ref[...], scale_ref[0], shift_ref[0]
    t_in_seq = lax.rem(pl.program_id(0), tiles_per_seq)
    keep_prev = (t_in_seq > 0).astype(F32)
    keep_next = (t_in_seq < tiles_per_seq - 1).astype(F32)
    zeros = jnp.zeros((pad, d), F32)
    h_sc[0:pad, :] = zeros
    h_sc[pad:2 * pad, :] = _norm_mod(xp_ref[...], gpre, scale, shift) * keep_prev
    h_sc[2 * pad:2 * pad + tm, :] = _norm_mod(x_ref[...], gpre, scale, shift)
    h_sc[2 * pad + tm:3 * pad + tm, :] = _norm_mod(xn_ref[...], gpre, scale, shift) * keep_next
    h_sc[3 * pad + tm:, :] = zeros
    for t_sc in (ta_sc, tb_sc):
        t_sc[0:pad, :] = zeros[:, :gdim]
        t_sc[3 * pad + tm:, :] = zeros[:, :gdim]

    ext = tm + 2 * pad

    def shifted_sum(src, cols, lo, hi):
        return src[pl.ds(pad + lo, ext), cols] + src[pl.ds(pad + hi, ext), cols]

    pos = t_in_seq * tm + lax.broadcasted_iota(jnp.int32, (tm, 1), 0)
    seq = tiles_per_seq * tm
    all_cols = pl.ds(0, gdim)
    for j, w in enumerate(POOL_WINDOWS):
        half = w // 2
        cols = pl.ds(j * gdim, gdim)
        src, src_cols, span = h_sc, cols, 1
        for dst in (ta_sc, tb_sc, ta_sc):
            if span * 2 >= w:
                break
            lo, hi = (-1, 0) if span == 1 else (-(span // 2), span // 2)
            dst[pl.ds(pad, ext), :] = shifted_sum(src, src_cols, lo, hi)
            src, src_cols, span = dst, all_cols, span * 2
        lo, hi = (-1, 0) if span == 1 else (-(span // 2), span // 2)
        win = src[pl.ds(2 * pad + lo, tm), src_cols] + src[pl.ds(2 * pad + hi, tm), src_cols]
        cnt = (jnp.minimum(pos + half, seq) - jnp.maximum(pos - half, 0)).astype(F32)
        p = win * (1.0 / cnt) - h_sc[pl.ds(2 * pad, tm), cols]
        y = jnp.dot(p.astype(BF16), w_ref[j], preferred_element_type=F32)
        m_sc[:, cols] = y * cs_ref[:, cols]
    o_ref[...] = x_ref[...] + gate_ref[0] * _rms(m_sc[...], gpost_ref[...])


def _pool_call(x, mod_l, g_pre, g_post, layer, w_pool, c_scale, c_layer, trunk):
    rows, d = x.shape
    tm = min(_row_tile(trunk), trunk.seq)
    assert trunk.seq % tm == 0 and tm % POOL_HALO == 0 and max(POOL_WINDOWS) // 2 <= POOL_HALO
    groups, gdim = w_pool.shape[1], w_pool.shape[2]
    assert groups == len(POOL_WINDOWS) and POOL_WINDOWS == (2, 4, 8, 16)
    halo_per_tile = tm // POOL_HALO
    last_halo = rows // POOL_HALO - 1
    blocks = [((tm, d), F32), ((tm, d), F32), ((groups, gdim, gdim), BF16)]
    ext_rows = tm + 4 * POOL_HALO
    scratch = [((ext_rows, d), F32), ((ext_rows, gdim), F32), ((ext_rows, gdim), F32), ((tm, d), F32)]
    return pl.pallas_call(
        functools.partial(_pool_kernel, tiles_per_seq=trunk.seq // tm),
        grid=(rows // tm,),
        in_specs=[
            pl.BlockSpec((tm, d), lambda i: (i, 0)),
            pl.BlockSpec((POOL_HALO, d), lambda i: (jnp.maximum(i * halo_per_tile - 1, 0), 0)),
            pl.BlockSpec((POOL_HALO, d), lambda i: (jnp.minimum((i + 1) * halo_per_tile, last_halo), 0)),
            pl.BlockSpec((1, 1, d), _mod_spec(trunk, tm, 0)),
            pl.BlockSpec((1, 1, d), _mod_spec(trunk, tm, 1)),
            pl.BlockSpec((1, 1, d), _mod_spec(trunk, tm, 2)),
            pl.BlockSpec((None, 1, d), lambda i: (layer, 0, 0)),
            pl.BlockSpec((None, 1, d), lambda i: (layer, 0, 0)),
            pl.BlockSpec((None, groups, gdim, gdim), lambda i: (c_layer, 0, 0, 0)),
            pl.BlockSpec((None, 1, d), lambda i: (c_layer, 0, 0)),
        ],
        out_specs=pl.BlockSpec((tm, d), lambda i: (i, 0)),
        out_shape=jax.ShapeDtypeStruct((rows, d), F32),
        scratch_shapes=[pltpu.VMEM(s, t) for s, t in scratch],
        compiler_params=_params(("parallel",), _vmem_limit(blocks, scratch, temps=[((tm, d), F32)] * 2)),
        name="pool",
    )(x, x, x, mod_l, mod_l, mod_l, g_pre, g_post, w_pool, c_scale)


def _trunk(x3, trunk, mod, cache_kv, p):
    b, t, d = x3.shape
    x = x3.reshape(b * t, d)
    depth = p["norm_mix_pre"].shape[0]
    n_kv = p["n_kv"]
    ia = ib = ic = 0
    new_k, new_v = [], []
    for l in range(depth):
        mod_l = mod[l]
        g_pre, g_post = p["norm_mix_pre"], p["norm_mix_post"]
        kind = l % 3
        if kind == 0:
            y = _mixa_call(x, mod_l, g_pre, l, p["a_w_in"], p["a_norm_v"], p["a_w_s"], p["a_b_s"], ia, trunk)
            x = _proj_call(y, p["a_w_out"], ia, x, mod_l, g_post, l, trunk)
            ia += 1
        elif kind == 1:
            kv_dtype = BF16 if trunk.latent else F32
            q, k, v = _qkv_call(x, mod_l, g_pre, l, p["b_w_qkv"], p["b_q_norm"], p["b_k_norm"], ib, n_kv, trunk,
                                kv_dtype)
            k3, v3 = k.reshape(b, t, -1), v.reshape(b, t, -1)
            if trunk.latent:
                ck, cv = cache_kv
                past = ck.shape[2]
                k3 = jnp.concatenate([k3, ck[:, ib].reshape(b, past, -1).astype(BF16)], axis=1)
                v3 = jnp.concatenate([v3, cv[:, ib].reshape(b, past, -1).astype(BF16)], axis=1)
            else:
                new_k.append(k3.reshape(b, t, n_kv, -1))
                new_v.append(v3.reshape(b, t, n_kv, -1))
            o = _attn_call(q.reshape(b, t, -1), k3, v3, n_kv)
            x = _proj_call(o.reshape(b * t, -1), p["b_w_o"], ib, x, mod_l, g_post, l, trunk)
            ib += 1
        else:
            x = _pool_call(x, mod_l, g_pre, g_post, l, p["c_w_pool"], p["c_scale"], ic, trunk)
            ic += 1
        x = _ffn_call(x, mod_l, p["norm_ffn_pre"], p["norm_ffn_post"], p["f_w_gu"], p["f_w_down"], l, trunk)
    return x.reshape(b, t, d), new_k, new_v


def kernel(x_prompt, x_sample, cache_k, cache_v, c, c_ctx, w_mod, b_mod, norm_mix_pre, norm_mix_post, norm_ffn_pre, norm_ffn_post, a_w_in, a_norm_v, a_w_s, a_b_s, a_w_out, b_w_qkv, b_q_norm, b_k_norm, b_w_o, c_w_pool, c_scale, f_w_gu, f_w_down):
    batch, seq, d = x_prompt.shape
    dec_batch, dec_seq, _ = x_sample.shape
    depth = w_mod.shape[0]
    assert 1 + dec_batch <= MOD_ROWS

    cond = jnp.concatenate([c_ctx[None, :], c, jnp.zeros((MOD_ROWS - 1 - dec_batch, d), F32)], axis=0)
    mod = _mod_call(cond, w_mod, b_mod).reshape(depth, MOD_ROWS * N_MOD, 1, d)

    rows3 = lambda a: a.reshape(a.shape[0], 1, a.shape[-1])
    p = {
        "n_kv": cache_k.shape[3],
        "norm_mix_pre": rows3(norm_mix_pre), "norm_mix_post": rows3(norm_mix_post),
        "norm_ffn_pre": rows3(norm_ffn_pre), "norm_ffn_post": rows3(norm_ffn_post),
        "a_w_in": a_w_in.astype(BF16), "a_norm_v": rows3(a_norm_v), "a_w_s": a_w_s.astype(BF16),
        "a_b_s": a_b_s[..., None], "a_w_out": a_w_out.astype(BF16),
        "b_w_qkv": b_w_qkv.astype(BF16), "b_q_norm": rows3(b_q_norm), "b_k_norm": rows3(b_k_norm),
        "b_w_o": b_w_o.astype(BF16),
        "c_w_pool": c_w_pool.astype(BF16), "c_scale": rows3(c_scale),
        "f_w_gu": f_w_gu.astype(BF16), "f_w_down": f_w_down.astype(BF16),
    }
    ctx = Trunk(batch=batch, seq=seq, mod_base=0, rows_per_cond=batch * seq, latent=False)
    lat = Trunk(batch=dec_batch, seq=dec_seq, mod_base=1, rows_per_cond=dec_seq, latent=True)
    y_prompt, new_k, new_v = _trunk(x_prompt, ctx, mod, None, p)
    y_sample, _, _ = _trunk(x_sample, lat, mod, (cache_k, cache_v), p)
    return (y_prompt, y_sample, jnp.stack(new_k, axis=1), jnp.stack(new_v, axis=1))
```

```python
import collections
import functools

import jax
import jax.numpy as jnp
from jax import lax
from jax.experimental import pallas as pl
from jax.experimental.pallas import tpu as pltpu

F32 = jnp.float32
BF16 = jnp.bfloat16

EPS = 1e-6
N_MOD = 6
GRID_W = 64
ROPE_THETA = 10000.0
POOL_WINDOWS = (2, 4, 8, 16)
POOL_HALO = 8
MOD_ROWS = 16
LOG2_E = 1.4426950408889634
V7X_LANES = 128

V7X_VMEM_BUDGET = 58 * 1024 * 1024
ROW_TILE = 512
ROW_GROUP = 16
ROW_GROUP_UNROLL = 8
FFN_HIDDEN_TILE = 512
COL_TILE = 512
ATTN_Q_TILE = 256
ATTN_MAX_KEYS = 4608
ATTN_HEAD_GROUPS = 2
MOD_COL_TILE = 1024

Trunk = collections.namedtuple("Trunk", "batch seq mod_base rows_per_cond latent")


def _nbytes(shape, dtype):
    n = 1
    for s in shape:
        n *= s
    return n * jnp.dtype(dtype).itemsize


def _vmem_limit(pipelined, scratch=(), temps=()):
    total = 2 * sum(_nbytes(s, d) for s, d in pipelined)
    total += sum(_nbytes(s, d) for s, d in scratch)
    total += sum(_nbytes(s, d) for s, d in temps)
    return int(min(max(total + (4 << 20), 16 << 20), V7X_VMEM_BUDGET))


def _params(semantics, limit):
    return pltpu.CompilerParams(dimension_semantics=semantics, vmem_limit_bytes=limit)


def _rms(xf, g):
    ms = jnp.mean(xf * xf, axis=-1, keepdims=True)
    return (xf * lax.rsqrt(ms + EPS)) * g


def _norm_mod(xf, g, scale, shift):
    return _rms(xf, g) * (1.0 + scale) + shift


def _row_groups(n_rows, body):
    assert n_rows % ROW_GROUP == 0

    def step(r, carry):
        body(pl.ds(pl.multiple_of(r * ROW_GROUP, ROW_GROUP), ROW_GROUP))
        return carry
    lax.fori_loop(0, n_rows // ROW_GROUP, step, 0, unroll=ROW_GROUP_UNROLL)


def _norm_mod_rows(x_ref, dst_ref, gpre_ref, scale_ref, shift_ref, dst_row0=0):
    assert dst_row0 % ROW_GROUP == 0
    d = x_ref.shape[1]
    gain = jnp.broadcast_to(gpre_ref[...] * (1.0 + scale_ref[0]), (ROW_GROUP, d))
    shift = jnp.broadcast_to(shift_ref[0], (ROW_GROUP, d))

    def body(rows):
        xr = x_ref[rows, :]
        ms = jnp.mean(xr * xr, axis=-1, keepdims=True)
        h = (xr * lax.rsqrt(ms + EPS)) * gain + shift
        dst_rows = pl.ds(pl.multiple_of(dst_row0 + rows.start, ROW_GROUP), ROW_GROUP)
        dst_ref[dst_rows, :] = h.astype(dst_ref.dtype)
    _row_groups(x_ref.shape[0], body)


def _gated_residual_rows(f_ref, x_ref, o_ref, gate_ref, gpost_ref):
    d = x_ref.shape[1]
    gain = jnp.broadcast_to(gate_ref[0] * gpost_ref[...], (ROW_GROUP, d))

    def body(rows):
        f = f_ref[rows, :]
        ms = jnp.mean(f * f, axis=-1, keepdims=True)
        o_ref[rows, :] = x_ref[rows, :] + (f * lax.rsqrt(ms + EPS)) * gain
    _row_groups(x_ref.shape[0], body)


def _row_tile(trunk):
    rows = trunk.batch * trunk.seq
    tm = min(ROW_TILE, rows)
    assert rows % tm == 0 and (trunk.rows_per_cond % tm == 0)
    return tm


def _mod_spec(trunk, tm, m, ahead=0):
    last_tile = trunk.batch * trunk.seq // tm - 1

    def index(i, *_):
        tile = jnp.minimum(i + ahead, last_tile)
        return ((trunk.mod_base + (tile * tm) // trunk.rows_per_cond) * N_MOD + m, 0, 0)
    return index


def _mod_kernel(c_ref, w_ref, b_ref, o_ref):
    c = c_ref[...]
    s = (c * jax.nn.sigmoid(c)).astype(BF16)
    o_ref[0] = jnp.dot(s, w_ref[0].astype(BF16), preferred_element_type=F32) + b_ref[0]


def _mod_call(cond, w_mod, b_mod):
    depth, d, n = w_mod.shape
    tn = min(MOD_COL_TILE, n)
    assert n % tn == 0
    blocks = [((MOD_ROWS, d), F32), ((1, d, tn), F32), ((1, 1, tn), F32), ((1, MOD_ROWS, tn), F32)]
    return pl.pallas_call(
        _mod_kernel,
        grid=(depth, n // tn),
        in_specs=[
            pl.BlockSpec((MOD_ROWS, d), lambda l, j: (0, 0)),
            pl.BlockSpec((1, d, tn), lambda l, j: (l, 0, j)),
            pl.BlockSpec((1, 1, tn), lambda l, j: (l, 0, j)),
        ],
        out_specs=pl.BlockSpec((1, MOD_ROWS, tn), lambda l, j: (l, 0, j)),
        out_shape=jax.ShapeDtypeStruct((depth, MOD_ROWS, n), F32),
        compiler_params=_params(("parallel", "parallel"), _vmem_limit(blocks, temps=[((d, tn), BF16)])),
        name="mod",
    )(cond, w_mod, b_mod.reshape(depth, 1, n))


def _ffn_kernel(x_ref, shift_ref, scale_ref, gate_ref, gpre_ref, gpost_ref, wg_ref, wu_ref, wd_ref,
                o_ref, h_sc, acc_sc, *, n_chunks):
    c = pl.program_id(1)

    @pl.when(c == 0)
    def _():
        _norm_mod_rows(x_ref, h_sc, gpre_ref, scale_ref, shift_ref)
        acc_sc[...] = jnp.zeros_like(acc_sc)

    h = h_sc[...]
    g = jnp.dot(h, wg_ref[...], preferred_element_type=F32)
    u = jnp.dot(h, wu_ref[...], preferred_element_type=F32)
    a = (g * jax.nn.sigmoid(g)) * u
    acc_sc[...] += jnp.dot(a.astype(BF16), wd_ref[...], preferred_element_type=F32)

    @pl.when(c == n_chunks - 1)
    def _():
        _gated_residual_rows(acc_sc, x_ref, o_ref, gate_ref, gpost_ref)


def _ffn_call(x, mod_l, g_pre, g_post, w_gu, w_down, layer, trunk):
    rows, d = x.shape
    hidden = w_down.shape[1]
    tm = _row_tile(trunk)
    th = min(FFN_HIDDEN_TILE, hidden)
    assert hidden % th == 0
    nc = hidden // th
    blocks = [((tm, d), F32)] * 2 + [((d, th), BF16), ((d, th), BF16), ((th, d), BF16)]
    scratch = [((tm, d), BF16), ((tm, d), F32)]
    temps = [((tm, th), F32)] * 4 + [((tm, d), F32)]
    return pl.pallas_call(
        functools.partial(_ffn_kernel, n_chunks=nc),
        grid=(rows // tm, nc),
        in_specs=[
            pl.BlockSpec((tm, d), lambda i, c: (i, 0)),
            pl.BlockSpec((1, 1, d), _mod_spec(trunk, tm, 3)),
            pl.BlockSpec((1, 1, d), _mod_spec(trunk, tm, 4)),
            pl.BlockSpec((1, 1, d), _mod_spec(trunk, tm, 5)),
            pl.BlockSpec((None, 1, d), lambda i, c: (layer, 0, 0)),
            pl.BlockSpec((None, 1, d), lambda i, c: (layer, 0, 0)),
            pl.BlockSpec((None, d, th), lambda i, c: (layer, 0, c)),
            pl.BlockSpec((None, d, th), lambda i, c: (layer, 0, nc + c)),
            pl.BlockSpec((None, th, d), lambda i, c: (layer, c, 0)),
        ],
        out_specs=pl.BlockSpec((tm, d), lambda i, c: (i, 0)),
        out_shape=jax.ShapeDtypeStruct((rows, d), F32),
        scratch_shapes=[pltpu.VMEM(s, t) for s, t in scratch],
        compiler_params=_params(("parallel", "arbitrary"), _vmem_limit(blocks, scratch, temps)),
        name="ffn",
    )(x, mod_l, mod_l, mod_l, g_pre, g_post, w_gu, w_gu, w_down)


def _proj_kernel(y_ref, w_ref, x_ref, gate_ref, gpost_ref, o_ref):
    m = jnp.dot(y_ref[...], w_ref[...], preferred_element_type=F32)
    o_ref[...] = x_ref[...] + gate_ref[0] * _rms(m, gpost_ref[...])


def _proj_call(y, w, w_layer, x, mod_l, g_post, layer, trunk):
    rows, d = x.shape
    k = y.shape[1]
    tm = _row_tile(trunk)
    blocks = [((tm, k), BF16), ((k, d), BF16), ((tm, d), F32), ((tm, d), F32)]
    return pl.pallas_call(
        _proj_kernel,
        grid=(rows // tm,),
        in_specs=[
            pl.BlockSpec((tm, k), lambda i: (i, 0)),
            pl.BlockSpec((None, k, d), lambda i: (w_layer, 0, 0)),
            pl.BlockSpec((tm, d), lambda i: (i, 0)),
            pl.BlockSpec((1, 1, d), _mod_spec(trunk, tm, 2)),
            pl.BlockSpec((None, 1, d), lambda i: (layer, 0, 0)),
        ],
        out_specs=pl.BlockSpec((tm, d), lambda i: (i, 0)),
        out_shape=jax.ShapeDtypeStruct((rows, d), F32),
        compiler_params=_params(("parallel",), _vmem_limit(blocks, temps=[((tm, d), F32)] * 2)),
        name="proj",
    )(y, w, x, mod_l, g_post)


def _mixa_kernel(x_ref, shift_ref, scale_ref, gpre_ref, win_ref, gv_ref, ws_ref, bs_ref,
                 y_ref, h_sc, a_sc, *, n_blocks, n_u_blocks, chunk, groups):
    j = pl.program_id(1)

    @pl.when(j == 0)
    def _():
        _norm_mod_rows(x_ref, h_sc, gpre_ref, scale_ref, shift_ref)

    a_sc[j] = jnp.dot(h_sc[...], win_ref[...], preferred_element_type=F32)

    @pl.when(j == n_blocks - 1)
    def _():
        tm, tn = a_sc.shape[1], a_sc.shape[2]
        width = n_u_blocks * tn
        gdim = width // groups
        ss = jnp.zeros((tm, 1), F32)
        for b in range(n_u_blocks, n_blocks):
            vb = a_sc[b]
            ss = ss + jnp.sum(vb * vb, axis=-1, keepdims=True)
        inv = lax.rsqrt(ss / width + EPS)
        for g in range(groups):
            col = g * gdim
            b, off = col // tn, col % tn
            gv = gv_ref[:, col:col + gdim]
            for c in range(tm // chunk):
                r = c * chunk
                v = (a_sc[n_u_blocks + b, r:r + chunk, off:off + gdim] * inv[r:r + chunk]) * gv
                s = jnp.dot(ws_ref[g], v.astype(BF16), preferred_element_type=F32) + bs_ref[g]
                u = a_sc[b, r:r + chunk, off:off + gdim]
                y_ref[r:r + chunk, col:col + gdim] = (u * s).astype(y_ref.dtype)


def _mixa_call(x, mod_l, g_pre, layer, w_in, g_v, w_s, b_s, a_layer, trunk):
    rows, d = x.shape
    width = w_in.shape[2] // 2
    groups, chunk = w_s.shape[1], w_s.shape[2]
    tm = _row_tile(trunk)
    tn = min(COL_TILE, width)
    gdim = width // groups
    assert width % tn == 0 and tn % gdim == 0 and tm % chunk == 0 and trunk.seq % chunk == 0
    nb = 2 * width // tn
    blocks = [((tm, d), F32), ((d, tn), BF16), ((tm, width), BF16), ((groups, chunk, chunk), BF16),
              ((groups, chunk, V7X_LANES), F32)]
    scratch = [((tm, d), BF16), ((nb, tm, tn), F32)]
    return pl.pallas_call(
        functools.partial(_mixa_kernel, n_blocks=nb, n_u_blocks=nb // 2, chunk=chunk, groups=groups),
        grid=(rows // tm, nb),
        in_specs=[
            pl.BlockSpec((tm, d), lambda i, j: (i, 0)),
            pl.BlockSpec((1, 1, d), _mod_spec(trunk, tm, 0)),
            pl.BlockSpec((1, 1, d), _mod_spec(trunk, tm, 1)),
            pl.BlockSpec((None, 1, d), lambda i, j: (layer, 0, 0)),
            pl.BlockSpec((None, d, tn), lambda i, j: (a_layer, 0, j)),
            pl.BlockSpec((None, 1, width), lambda i, j: (a_layer, 0, 0)),
            pl.BlockSpec((None, groups, chunk, chunk), lambda i, j: (a_layer, 0, 0, 0)),
            pl.BlockSpec((None, groups, chunk, 1), lambda i, j: (a_layer, 0, 0, 0)),
        ],
        out_specs=pl.BlockSpec((tm, width), lambda i, j: (i, 0)),
        out_shape=jax.ShapeDtypeStruct((rows, width), BF16),
        scratch_shapes=[pltpu.VMEM(s, t) for s, t in scratch],
        compiler_params=_params(("parallel", "arbitrary"),
                                _vmem_limit(blocks, scratch, temps=[((tm, d), F32), ((tm, tn), F32)])),
        name="mixa",
    )(x, mod_l, mod_l, g_pre, w_in, g_v, w_s, b_s)


def _rope_tables(seq, head_dim):
    axis_dim = head_dim // 2
    t = jnp.arange(seq)
    n_rows = seq // GRID_W
    row = jnp.minimum(t // GRID_W, n_rows - 1).astype(F32)
    col = (t % GRID_W).astype(F32)
    inv = jnp.power(ROPE_THETA, -jnp.arange(0, axis_dim, 2, dtype=F32) / axis_dim)
    ang = jnp.concatenate([row[:, None] * inv, col[:, None] * inv], axis=-1)
    cos, sin = jnp.cos(ang), jnp.sin(ang)
    zero = jnp.zeros_like(sin)
    cos2 = jnp.stack([cos, cos], axis=-1).reshape(seq, head_dim)
    sin_a = jnp.stack([-sin, zero], axis=-1).reshape(seq, head_dim)
    sin_b = jnp.stack([zero, sin], axis=-1).reshape(seq, head_dim)
    return cos2, sin_a, sin_b


def _qkv_kernel(*refs, n_q_blocks, head_dim, latent, q_scale):
    if latent:
        (x_ref, shift_ref, scale_ref, gpre_ref, w_ref, qg_ref, kg_ref, cos_ref, sa_ref, sb_ref,
         q_ref, k_ref, v_ref, h_sc) = refs
    else:
        (x_ref, shift_ref, scale_ref, gpre_ref, w_ref, qg_ref, kg_ref, q_ref, k_ref, v_ref, h_sc) = refs
    j = pl.program_id(1)

    @pl.when(j == 0)
    def _():
        _norm_mod_rows(x_ref, h_sc, gpre_ref, scale_ref, shift_ref)

    a = jnp.dot(h_sc[...], w_ref[...], preferred_element_type=F32)
    n_heads = a.shape[1] // head_dim

    def head(hh, gain):
        blk = _rms(a[:, hh * head_dim:(hh + 1) * head_dim], gain)
        if latent:
            nxt = pltpu.roll(blk, head_dim - 1, 1)
            prv = pltpu.roll(blk, 1, 1)
            blk = blk * cos_ref[...] + nxt * sa_ref[...] + prv * sb_ref[...]
        return blk

    @pl.when(j < n_q_blocks)
    def _():
        for hh in range(n_heads):
            q_ref[:, hh * head_dim:(hh + 1) * head_dim] = (head(hh, qg_ref[...]) * q_scale).astype(q_ref.dtype)

    @pl.when(j == n_q_blocks)
    def _():
        for hh in range(n_heads):
            k_ref[:, hh * head_dim:(hh + 1) * head_dim] = head(hh, kg_ref[...]).astype(k_ref.dtype)

    @pl.when(j == n_q_blocks + 1)
    def _():
        v_ref[...] = a.astype(v_ref.dtype)


def _qkv_call(x, mod_l, g_pre, layer, w_qkv, q_gain, k_gain, b_layer, n_kv, trunk, kv_dtype):
    rows, d = x.shape
    hd = q_gain.shape[-1]
    tn = n_kv * hd
    nq = w_qkv.shape[2] - 2 * tn
    assert nq % tn == 0
    nqb = nq // tn
    tm = _row_tile(trunk)
    in_specs = [
        pl.BlockSpec((tm, d), lambda i, j: (i, 0)),
        pl.BlockSpec((1, 1, d), _mod_spec(trunk, tm, 0)),
        pl.BlockSpec((1, 1, d), _mod_spec(trunk, tm, 1)),
        pl.BlockSpec((None, 1, d), lambda i, j: (layer, 0, 0)),
        pl.BlockSpec((None, d, tn), lambda i, j: (b_layer, 0, j)),
        pl.BlockSpec((None, 1, hd), lambda i, j: (b_layer, 0, 0)),
        pl.BlockSpec((None, 1, hd), lambda i, j: (b_layer, 0, 0)),
    ]
    args = [x, mod_l, mod_l, g_pre, w_qkv, q_gain, k_gain]
    if trunk.latent:
        assert trunk.seq % tm == 0 and trunk.seq % GRID_W == 0
        tiles_per_seq = trunk.seq // tm
        in_specs += [pl.BlockSpec((tm, hd), lambda i, j: (i % tiles_per_seq, 0))] * 3
        args += list(_rope_tables(trunk.seq, hd))
    blocks = [((tm, d), F32), ((d, tn), BF16), ((tm, tn), BF16), ((tm, tn), kv_dtype), ((tm, tn), kv_dtype),
              ((tm, hd), F32), ((tm, hd), F32), ((tm, hd), F32)]
    scratch = [((tm, d), BF16)]
    return pl.pallas_call(
        functools.partial(_qkv_kernel, n_q_blocks=nqb, head_dim=hd, latent=trunk.latent,
                          q_scale=hd ** -0.5 * LOG2_E),
        grid=(rows // tm, nqb + 2),
        in_specs=in_specs,
        out_specs=[
            pl.BlockSpec((tm, tn), lambda i, j: (i, jnp.minimum(j, nqb - 1))),
            pl.BlockSpec((tm, tn), lambda i, j: (i, 0)),
            pl.BlockSpec((tm, tn), lambda i, j: (i, 0)),
        ],
        out_shape=[
            jax.ShapeDtypeStruct((rows, nq), BF16),
            jax.ShapeDtypeStruct((rows, tn), kv_dtype),
            jax.ShapeDtypeStruct((rows, tn), kv_dtype),
        ],
        scratch_shapes=[pltpu.VMEM(s, t) for s, t in scratch],
        compiler_params=_params(("parallel", "arbitrary"),
                                _vmem_limit(blocks, scratch, temps=[((tm, d), F32), ((tm, tn), F32)])),
        name="qkv",
    )(*args)


def _attn_kernel(q_ref, k_ref, v_ref, o_ref, *, q_per_kv, head_dim):
    hd = head_dim
    tq, tk = q_ref.shape[1], k_ref.shape[1]
    n_groups = ATTN_HEAD_GROUPS if q_per_kv % ATTN_HEAD_GROUPS == 0 else 1
    per = q_per_kv // n_groups
    k = k_ref[0].astype(BF16)
    v1 = jnp.concatenate([v_ref[0].astype(BF16), jnp.ones((tk, hd), BF16)], axis=1)
    qs = [jnp.concatenate([q_ref[0, :, g * hd:(g + 1) * hd] for g in range(i * per, (i + 1) * per)], axis=0)
          for i in range(n_groups)]
    ss = [lax.dot_general(q, k, (((1,), (1,)), ((), ())), preferred_element_type=F32) for q in qs]
    m_curs = [jnp.max(s, axis=-1, keepdims=True) for s in ss]

    def write(i, acc):
        out = acc[:, :hd] / acc[:, hd:]
        for j in range(per):
            g = i * per + j
            o_ref[0, :, g * hd:(g + 1) * hd] = out[j * tq:(j + 1) * tq].astype(o_ref.dtype)

    ps = [jnp.exp2(s - m).astype(BF16) for s, m in zip(ss, m_curs)]
    accs = [jnp.dot(p, v1, preferred_element_type=F32) for p in ps]
    for i, acc in enumerate(accs):
        write(i, acc)


def _attn_call(q, k, v, n_kv):
    b, t, nq = q.shape
    s = k.shape[1]
    hd = k.shape[2] // n_kv
    gw = nq // n_kv
    tq = min(ATTN_Q_TILE, t)
    assert t % tq == 0 and s <= ATTN_MAX_KEYS and s % V7X_LANES == 0 and hd % V7X_LANES == 0
    rows = (gw // hd) * tq
    blocks = [((1, tq, gw), BF16), ((1, s, hd), k.dtype), ((1, s, hd), v.dtype), ((1, tq, gw), BF16)]
    temps = [((rows, s), F32)] * 2 + [((rows, s), BF16), ((s, 2 * hd), BF16), ((rows, 2 * hd), F32)]
    return pl.pallas_call(
        functools.partial(_attn_kernel, q_per_kv=gw // hd, head_dim=hd),
        grid=(b, n_kv, t // tq),
        in_specs=[
            pl.BlockSpec((1, tq, gw), lambda bi, h, qi: (bi, qi, h)),
            pl.BlockSpec((1, s, hd), lambda bi, h, qi: (bi, 0, h)),
            pl.BlockSpec((1, s, hd), lambda bi, h, qi: (bi, 0, h)),
        ],
        out_specs=pl.BlockSpec((1, tq, gw), lambda bi, h, qi: (bi, qi, h)),
        out_shape=jax.ShapeDtypeStruct((b, t, nq), BF16),
        compiler_params=_params(("parallel", "parallel", "parallel"), _vmem_limit(blocks, temps=temps)),
        name="attn",
    )(q, k, v)


def _pool_kernel(x_ref, xp_ref, xn_ref, shift_ref, scale_ref, gate_ref, gpre_ref, gpost_ref, w_ref, cs_ref,
                 o_ref, h_sc, ta_sc, tb_sc, m_sc, *, tiles_per_seq):
    tm, d = x_ref.shape
    pad = POOL_HALO
    gdim = d // len(POOL_WINDOWS)
    gpre, scale, shift = gpre_ref[...], scale_ref[0], shift_ref[0]
    t_in_seq = lax.rem(pl.program_id(0), tiles_per_seq)
    keep_prev = (t_in_seq > 0).astype(F32)
    keep_next = (t_in_seq < tiles_per_seq - 1).astype(F32)
    zeros = jnp.zeros((pad, d), F32)
    h_sc[0:pad, :] = zeros
    h_sc[pad:2 * pad, :] = _norm_mod(xp_ref[...], gpre, scale, shift) * keep_prev
    _norm_mod_rows(x_ref, h_sc, gpre_ref, scale_ref, shift_ref, dst_row0=2 * pad)
    h_sc[2 * pad + tm:3 * pad + tm, :] = _norm_mod(xn_ref[...], gpre, scale, shift) * keep_next
    h_sc[3 * pad + tm:, :] = zeros
    for t_sc in (ta_sc, tb_sc):
        t_sc[0:pad, :] = zeros[:, :gdim]
        t_sc[3 * pad + tm:, :] = zeros[:, :gdim]

    ext = tm + 2 * pad

    def shifted_sum(src, cols, lo, hi):
        return src[pl.ds(pad + lo, ext), cols] + src[pl.ds(pad + hi, ext), cols]

    pos = t_in_seq * tm + lax.broadcasted_iota(jnp.int32, (tm, 1), 0)
    seq = tiles_per_seq * tm
    all_cols = pl.ds(0, gdim)
    for j, w in enumerate(POOL_WINDOWS):
        half = w // 2
        cols = pl.ds(j * gdim, gdim)
        src, src_cols, span = h_sc, cols, 1
        for dst in (ta_sc, tb_sc, ta_sc):
            if span * 2 >= w:
                break
            lo, hi = (-1, 0) if span == 1 else (-(span // 2), span // 2)
            dst[pl.ds(pad, ext), :] = shifted_sum(src, src_cols, lo, hi)
            src, src_cols, span = dst, all_cols, span * 2
        lo, hi = (-1, 0) if span == 1 else (-(span // 2), span // 2)
        win = src[pl.ds(2 * pad + lo, tm), src_cols] + src[pl.ds(2 * pad + hi, tm), src_cols]
        cnt = (jnp.minimum(pos + half, seq) - jnp.maximum(pos - half, 0)).astype(F32)
        p = win * (1.0 / cnt) - h_sc[pl.ds(2 * pad, tm), cols]
        y = jnp.dot(p.astype(BF16), w_ref[j], preferred_element_type=F32)
        m_sc[:, cols] = y * cs_ref[:, cols]
    _gated_residual_rows(m_sc, x_ref, o_ref, gate_ref, gpost_ref)


def _pool_call(x, mod_l, g_pre, g_post, layer, w_pool, c_scale, c_layer, trunk):
    rows, d = x.shape
    tm = min(_row_tile(trunk), trunk.seq)
    assert trunk.seq % tm == 0 and tm % POOL_HALO == 0 and max(POOL_WINDOWS) // 2 <= POOL_HALO
    groups, gdim = w_pool.shape[1], w_pool.shape[2]
    assert groups == len(POOL_WINDOWS) and POOL_WINDOWS == (2, 4, 8, 16)
    halo_per_tile = tm // POOL_HALO
    last_halo = rows // POOL_HALO - 1
    blocks = [((tm, d), F32), ((tm, d), F32), ((groups, gdim, gdim), BF16)]
    ext_rows = tm + 4 * POOL_HALO
    scratch = [((ext_rows, d), F32), ((ext_rows, gdim), F32), ((ext_rows, gdim), F32), ((tm, d), F32)]
    return pl.pallas_call(
        functools.partial(_pool_kernel, tiles_per_seq=trunk.seq // tm),
        grid=(rows // tm,),
        in_specs=[
            pl.BlockSpec((tm, d), lambda i: (i, 0)),
            pl.BlockSpec((POOL_HALO, d), lambda i: (jnp.maximum(i * halo_per_tile - 1, 0), 0)),
            pl.BlockSpec((POOL_HALO, d), lambda i: (jnp.minimum((i + 1) * halo_per_tile, last_halo), 0)),
            pl.BlockSpec((1, 1, d), _mod_spec(trunk, tm, 0)),
            pl.BlockSpec((1, 1, d), _mod_spec(trunk, tm, 1)),
            pl.BlockSpec((1, 1, d), _mod_spec(trunk, tm, 2)),
            pl.BlockSpec((None, 1, d), lambda i: (layer, 0, 0)),
            pl.BlockSpec((None, 1, d), lambda i: (layer, 0, 0)),
            pl.BlockSpec((None, groups, gdim, gdim), lambda i: (c_layer, 0, 0, 0)),
            pl.BlockSpec((None, 1, d), lambda i: (c_layer, 0, 0)),
        ],
        out_specs=pl.BlockSpec((tm, d), lambda i: (i, 0)),
        out_shape=jax.ShapeDtypeStruct((rows, d), F32),
        scratch_shapes=[pltpu.VMEM(s, t) for s, t in scratch],
        compiler_params=_params(("parallel",), _vmem_limit(blocks, scratch, temps=[((tm, d), F32)] * 2)),
        name="pool",
    )(x, x, x, mod_l, mod_l, mod_l, g_pre, g_post, w_pool, c_scale)


def _trunk(x3, trunk, mod, cache_kv, p):
    b, t, d = x3.shape
    x = x3.reshape(b * t, d)
    depth = p["norm_mix_pre"].shape[0]
    n_kv = p["n_kv"]
    ia = ib = ic = 0
    new_k, new_v = [], []
    for l in range(depth):
        mod_l = mod[l]
        g_pre, g_post = p["norm_mix_pre"], p["norm_mix_post"]
        kind = l % 3
        if kind == 0:
            y = _mixa_call(x, mod_l, g_pre, l, p["a_w_in"], p["a_norm_v"], p["a_w_s"], p["a_b_s"], ia, trunk)
            x = _proj_call(y, p["a_w_out"], ia, x, mod_l, g_post, l, trunk)
            ia += 1
        elif kind == 1:
            kv_dtype = BF16 if trunk.latent else F32
            q, k, v = _qkv_call(x, mod_l, g_pre, l, p["b_w_qkv"], p["b_q_norm"], p["b_k_norm"], ib, n_kv, trunk,
                                kv_dtype)
            k3, v3 = k.reshape(b, t, -1), v.reshape(b, t, -1)
            if trunk.latent:
                ck, cv = cache_kv
                past = ck.shape[2]
                k3 = jnp.concatenate([k3, ck[:, ib].reshape(b, past, -1).astype(BF16)], axis=1)
                v3 = jnp.concatenate([v3, cv[:, ib].reshape(b, past, -1).astype(BF16)], axis=1)
            else:
                new_k.append(k3.reshape(b, t, n_kv, -1))
                new_v.append(v3.reshape(b, t, n_kv, -1))
            o = _attn_call(q.reshape(b, t, -1), k3, v3, n_kv)
            x = _proj_call(o.reshape(b * t, -1), p["b_w_o"], ib, x, mod_l, g_post, l, trunk)
            ib += 1
        else:
            x = _pool_call(x, mod_l, g_pre, g_post, l, p["c_w_pool"], p["c_scale"], ic, trunk)
            ic += 1
        x = _ffn_call(x, mod_l, p["norm_ffn_pre"], p["norm_ffn_post"], p["f_w_gu"], p["f_w_down"], l, trunk)
    return x.reshape(b, t, d), new_k, new_v


def kernel(x_prompt, x_sample, cache_k, cache_v, c, c_ctx, w_mod, b_mod, norm_mix_pre, norm_mix_post, norm_ffn_pre, norm_ffn_post, a_w_in, a_norm_v, a_w_s, a_b_s, a_w_out, b_w_qkv, b_q_norm, b_k_norm, b_w_o, c_w_pool, c_scale, f_w_gu, f_w_down):
    batch, seq, d = x_prompt.shape
    dec_batch, dec_seq, _ = x_sample.shape
    depth = w_mod.shape[0]
    assert 1 + dec_batch <= MOD_ROWS

    cond = jnp.concatenate([c_ctx[None, :], c, jnp.zeros((MOD_ROWS - 1 - dec_batch, d), F32)], axis=0)
    mod = _mod_call(cond, w_mod, b_mod).reshape(depth, MOD_ROWS * N_MOD, 1, d)

    rows3 = lambda a: a.reshape(a.shape[0], 1, a.shape[-1])
    p = {
        "n_kv": cache_k.shape[3],
        "norm_mix_pre": rows3(norm_mix_pre), "norm_mix_post": rows3(norm_mix_post),
        "norm_ffn_pre": rows3(norm_ffn_pre), "norm_ffn_post": rows3(norm_ffn_post),
        "a_w_in": a_w_in.astype(BF16), "a_norm_v": rows3(a_norm_v), "a_w_s": a_w_s.astype(BF16),
        "a_b_s": a_b_s[..., None], "a_w_out": a_w_out.astype(BF16),
        "b_w_qkv": b_w_qkv.astype(BF16), "b_q_norm": rows3(b_q_norm), "b_k_norm": rows3(b_k_norm),
        "b_w_o": b_w_o.astype(BF16),
        "c_w_pool": c_w_pool.astype(BF16), "c_scale": rows3(c_scale),
        "f_w_gu": f_w_gu.astype(BF16), "f_w_down": f_w_down.astype(BF16),
    }
    ctx = Trunk(batch=batch, seq=seq, mod_base=0, rows_per_cond=batch * seq, latent=False)
    lat = Trunk(batch=dec_batch, seq=dec_seq, mod_base=1, rows_per_cond=dec_seq, latent=True)
    y_prompt, new_k, new_v = _trunk(x_prompt, ctx, mod, None, p)
    y_sample, _, _ = _trunk(x_sample, lat, mod, (cache_k, cache_v), p)
    return (y_prompt, y_sample, jnp.stack(new_k, axis=1), jnp.stack(new_v, axis=1))
```

```python
import collections
import functools

import jax
import jax.numpy as jnp
from jax import lax
from jax.experimental import pallas as pl
from jax.experimental.pallas import tpu as pltpu

F32 = jnp.float32
BF16 = jnp.bfloat16

EPS = 1e-6
N_MOD = 6
GRID_W = 64
ROPE_THETA = 10000.0
POOL_WINDOWS = (2, 4, 8, 16)
POOL_HALO = 8
MOD_ROWS = 16
LOG2_E = 1.4426950408889634
V7X_LANES = 128

V7X_VMEM_BUDGET = 58 * 1024 * 1024
ROW_TILE = 512
ROW_GROUP = 16
ROW_GROUP_UNROLL = 8
FFN_HIDDEN_TILE = 512
COL_TILE = 512
ATTN_Q_TILE = 256
ATTN_MAX_KEYS = 4608
ATTN_HEAD_GROUPS = 2
MOD_COL_TILE = 1024

Trunk = collections.namedtuple("Trunk", "batch seq mod_base rows_per_cond latent")


def _nbytes(shape, dtype):
    n = 1
    for s in shape:
        n *= s
    return n * jnp.dtype(dtype).itemsize


def _vmem_limit(pipelined, scratch=(), temps=()):
    total = 2 * sum(_nbytes(s, d) for s, d in pipelined)
    total += sum(_nbytes(s, d) for s, d in scratch)
    total += sum(_nbytes(s, d) for s, d in temps)
    return int(min(max(total + (4 << 20), 16 << 20), V7X_VMEM_BUDGET))


def _params(semantics, limit):
    return pltpu.CompilerParams(dimension_semantics=semantics, vmem_limit_bytes=limit)


def _rms(xf, g):
    ms = jnp.mean(xf * xf, axis=-1, keepdims=True)
    return (xf * lax.rsqrt(ms + EPS)) * g


def _norm_mod(xf, g, scale, shift):
    return _rms(xf, g) * (1.0 + scale) + shift


def _row_groups(n_rows, body):
    assert n_rows % ROW_GROUP == 0

    def step(r, carry):
        body(pl.ds(pl.multiple_of(r * ROW_GROUP, ROW_GROUP), ROW_GROUP))
        return carry
    lax.fori_loop(0, n_rows // ROW_GROUP, step, 0, unroll=ROW_GROUP_UNROLL)


def _norm_mod_rows(x_ref, dst_ref, gpre_ref, scale_ref, shift_ref):
    d = x_ref.shape[1]
    gain = jnp.broadcast_to(gpre_ref[...] * (1.0 + scale_ref[0]), (ROW_GROUP, d))
    shift = jnp.broadcast_to(shift_ref[0], (ROW_GROUP, d))

    def body(rows):
        xr = x_ref[rows, :]
        ms = jnp.mean(xr * xr, axis=-1, keepdims=True)
        h = (xr * lax.rsqrt(ms + EPS)) * gain + shift
        dst_ref[rows, :] = h.astype(dst_ref.dtype)
    _row_groups(x_ref.shape[0], body)


def _gated_residual_rows(f_ref, x_ref, o_ref, gate_ref, gpost_ref):
    d = x_ref.shape[1]
    gain = jnp.broadcast_to(gate_ref[0] * gpost_ref[...], (ROW_GROUP, d))

    def body(rows):
        f = f_ref[rows, :]
        ms = jnp.mean(f * f, axis=-1, keepdims=True)
        o_ref[rows, :] = x_ref[rows, :] + (f * lax.rsqrt(ms + EPS)) * gain
    _row_groups(x_ref.shape[0], body)


def _row_tile(trunk):
    rows = trunk.batch * trunk.seq
    tm = min(ROW_TILE, rows)
    assert rows % tm == 0 and (trunk.rows_per_cond % tm == 0)
    return tm


def _mod_spec(trunk, tm, m, ahead=0):
    last_tile = trunk.batch * trunk.seq // tm - 1

    def index(i, *_):
        tile = jnp.minimum(i + ahead, last_tile)
        return ((trunk.mod_base + (tile * tm) // trunk.rows_per_cond) * N_MOD + m, 0, 0)
    return index


def _mod_kernel(c_ref, w_ref, b_ref, o_ref):
    c = c_ref[...]
    s = (c * jax.nn.sigmoid(c)).astype(BF16)
    o_ref[0] = jnp.dot(s, w_ref[0].astype(BF16), preferred_element_type=F32) + b_ref[0]


def _mod_call(cond, w_mod, b_mod):
    depth, d, n = w_mod.shape
    tn = min(MOD_COL_TILE, n)
    assert n % tn == 0
    blocks = [((MOD_ROWS, d), F32), ((1, d, tn), F32), ((1, 1, tn), F32), ((1, MOD_ROWS, tn), F32)]
    return pl.pallas_call(
        _mod_kernel,
        grid=(depth, n // tn),
        in_specs=[
            pl.BlockSpec((MOD_ROWS, d), lambda l, j: (0, 0)),
            pl.BlockSpec((1, d, tn), lambda l, j: (l, 0, j)),
            pl.BlockSpec((1, 1, tn), lambda l, j: (l, 0, j)),
        ],
        out_specs=pl.BlockSpec((1, MOD_ROWS, tn), lambda l, j: (l, 0, j)),
        out_shape=jax.ShapeDtypeStruct((depth, MOD_ROWS, n), F32),
        compiler_params=_params(("parallel", "parallel"), _vmem_limit(blocks, temps=[((d, tn), BF16)])),
        name="mod",
    )(cond, w_mod, b_mod.reshape(depth, 1, n))


def _ffn_kernel(x_ref, shift_ref, scale_ref, gate_ref, gpre_ref, gpost_ref, wg_ref, wu_ref, wd_ref,
                o_ref, h_sc, acc_sc, *, n_chunks):
    c = pl.program_id(1)

    @pl.when(c == 0)
    def _():
        _norm_mod_rows(x_ref, h_sc, gpre_ref, scale_ref, shift_ref)
        acc_sc[...] = jnp.zeros_like(acc_sc)

    h = h_sc[...]
    g = jnp.dot(h, wg_ref[...], preferred_element_type=F32)
    u = jnp.dot(h, wu_ref[...], preferred_element_type=F32)
    a = (g * jax.nn.sigmoid(g)) * u
    acc_sc[...] += jnp.dot(a.astype(BF16), wd_ref[...], preferred_element_type=F32)

    @pl.when(c == n_chunks - 1)
    def _():
        _gated_residual_rows(acc_sc, x_ref, o_ref, gate_ref, gpost_ref)


def _ffn_call(x, mod_l, g_pre, g_post, w_gu, w_down, layer, trunk):
    rows, d = x.shape
    hidden = w_down.shape[1]
    tm = _row_tile(trunk)
    th = min(FFN_HIDDEN_TILE, hidden)
    assert hidden % th == 0
    nc = hidden // th
    blocks = [((tm, d), F32)] * 2 + [((d, th), BF16), ((d, th), BF16), ((th, d), BF16)]
    scratch = [((tm, d), BF16), ((tm, d), F32)]
    temps = [((tm, th), F32)] * 4 + [((tm, d), F32)]
    return pl.pallas_call(
        functools.partial(_ffn_kernel, n_chunks=nc),
        grid=(rows // tm, nc),
        in_specs=[
            pl.BlockSpec((tm, d), lambda i, c: (i, 0)),
            pl.BlockSpec((1, 1, d), _mod_spec(trunk, tm, 3)),
            pl.BlockSpec((1, 1, d), _mod_spec(trunk, tm, 4)),
            pl.BlockSpec((1, 1, d), _mod_spec(trunk, tm, 5)),
            pl.BlockSpec((None, 1, d), lambda i, c: (layer, 0, 0)),
            pl.BlockSpec((None, 1, d), lambda i, c: (layer, 0, 0)),
            pl.BlockSpec((None, d, th), lambda i, c: (layer, 0, c)),
            pl.BlockSpec((None, d, th), lambda i, c: (layer, 0, nc + c)),
            pl.BlockSpec((None, th, d), lambda i, c: (layer, c, 0)),
        ],
        out_specs=pl.BlockSpec((tm, d), lambda i, c: (i, 0)),
        out_shape=jax.ShapeDtypeStruct((rows, d), F32),
        scratch_shapes=[pltpu.VMEM(s, t) for s, t in scratch],
        compiler_params=_params(("parallel", "arbitrary"), _vmem_limit(blocks, scratch, temps)),
        name="ffn",
    )(x, mod_l, mod_l, mod_l, g_pre, g_post, w_gu, w_gu, w_down)


def _proj_kernel(y_ref, w_ref, x_ref, gate_ref, gpost_ref, o_ref):
    m = jnp.dot(y_ref[...], w_ref[...], preferred_element_type=F32)
    o_ref[...] = x_ref[...] + gate_ref[0] * _rms(m, gpost_ref[...])


def _proj_call(y, w, w_layer, x, mod_l, g_post, layer, trunk):
    rows, d = x.shape
    k = y.shape[1]
    tm = _row_tile(trunk)
    blocks = [((tm, k), BF16), ((k, d), BF16), ((tm, d), F32), ((tm, d), F32)]
    return pl.pallas_call(
        _proj_kernel,
        grid=(rows // tm,),
        in_specs=[
            pl.BlockSpec((tm, k), lambda i: (i, 0)),
            pl.BlockSpec((None, k, d), lambda i: (w_layer, 0, 0)),
            pl.BlockSpec((tm, d), lambda i: (i, 0)),
            pl.BlockSpec((1, 1, d), _mod_spec(trunk, tm, 2)),
            pl.BlockSpec((None, 1, d), lambda i: (layer, 0, 0)),
        ],
        out_specs=pl.BlockSpec((tm, d), lambda i: (i, 0)),
        out_shape=jax.ShapeDtypeStruct((rows, d), F32),
        compiler_params=_params(("parallel",), _vmem_limit(blocks, temps=[((tm, d), F32)] * 2)),
        name="proj",
    )(y, w, x, mod_l, g_post)


def _mixa_kernel(x_ref, shift_ref, scale_ref, gpre_ref, win_ref, gv_ref, ws_ref, bs_ref,
                 y_ref, h_sc, a_sc, *, n_blocks, n_u_blocks, chunk, groups):
    j = pl.program_id(1)

    @pl.when(j == 0)
    def _():
        _norm_mod_rows(x_ref, h_sc, gpre_ref, scale_ref, shift_ref)

    a_sc[j] = jnp.dot(h_sc[...], win_ref[...], preferred_element_type=F32)

    @pl.when(j == n_blocks - 1)
    def _():
        tm, tn = a_sc.shape[1], a_sc.shape[2]
        width = n_u_blocks * tn
        gdim = width // groups
        ss = jnp.zeros((tm, 1), F32)
        for b in range(n_u_blocks, n_blocks):
            vb = a_sc[b]
            ss = ss + jnp.sum(vb * vb, axis=-1, keepdims=True)
        inv = lax.rsqrt(ss / width + EPS)
        for g in range(groups):
            col = g * gdim
            b, off = col // tn, col % tn
            gv = gv_ref[:, col:col + gdim]
            for c in range(tm // chunk):
                r = c * chunk
                v = (a_sc[n_u_blocks + b, r:r + chunk, off:off + gdim] * inv[r:r + chunk]) * gv
                s = jnp.dot(ws_ref[g], v.astype(BF16), preferred_element_type=F32) + bs_ref[g]
                u = a_sc[b, r:r + chunk, off:off + gdim]
                y_ref[r:r + chunk, col:col + gdim] = (u * s).astype(y_ref.dtype)


def _mixa_call(x, mod_l, g_pre, layer, w_in, g_v, w_s, b_s, a_layer, trunk):
    rows, d = x.shape
    width = w_in.shape[2] // 2
    groups, chunk = w_s.shape[1], w_s.shape[2]
    tm = _row_tile(trunk)
    tn = min(COL_TILE, width)
    gdim = width // groups
    assert width % tn == 0 and tn % gdim == 0 and tm % chunk == 0 and trunk.seq % chunk == 0
    nb = 2 * width // tn
    blocks = [((tm, d), F32), ((d, tn), BF16), ((tm, width), BF16), ((groups, chunk, chunk), BF16),
              ((groups, chunk, V7X_LANES), F32)]
    scratch = [((tm, d), BF16), ((nb, tm, tn), F32)]
    return pl.pallas_call(
        functools.partial(_mixa_kernel, n_blocks=nb, n_u_blocks=nb // 2, chunk=chunk, groups=groups),
        grid=(rows // tm, nb),
        in_specs=[
            pl.BlockSpec((tm, d), lambda i, j: (i, 0)),
            pl.BlockSpec((1, 1, d), _mod_spec(trunk, tm, 0)),
            pl.BlockSpec((1, 1, d), _mod_spec(trunk, tm, 1)),
            pl.BlockSpec((None, 1, d), lambda i, j: (layer, 0, 0)),
            pl.BlockSpec((None, d, tn), lambda i, j: (a_layer, 0, j)),
            pl.BlockSpec((None, 1, width), lambda i, j: (a_layer, 0, 0)),
            pl.BlockSpec((None, groups, chunk, chunk), lambda i, j: (a_layer, 0, 0, 0)),
            pl.BlockSpec((None, groups, chunk, 1), lambda i, j: (a_layer, 0, 0, 0)),
        ],
        out_specs=pl.BlockSpec((tm, width), lambda i, j: (i, 0)),
        out_shape=jax.ShapeDtypeStruct((rows, width), BF16),
        scratch_shapes=[pltpu.VMEM(s, t) for s, t in scratch],
        compiler_params=_params(("parallel", "arbitrary"),
                                _vmem_limit(blocks, scratch, temps=[((tm, d), F32), ((tm, tn), F32)])),
        name="mixa",
    )(x, mod_l, mod_l, g_pre, w_in, g_v, w_s, b_s)


def _rope_tables(seq, head_dim):
    axis_dim = head_dim // 2
    t = jnp.arange(seq)
    n_rows = seq // GRID_W
    row = jnp.minimum(t // GRID_W, n_rows - 1).astype(F32)
    col = (t % GRID_W).astype(F32)
    inv = jnp.power(ROPE_THETA, -jnp.arange(0, axis_dim, 2, dtype=F32) / axis_dim)
    ang = jnp.concatenate([row[:, None] * inv, col[:, None] * inv], axis=-1)
    cos, sin = jnp.cos(ang), jnp.sin(ang)
    zero = jnp.zeros_like(sin)
    cos2 = jnp.stack([cos, cos], axis=-1).reshape(seq, head_dim)
    sin_a = jnp.stack([-sin, zero], axis=-1).reshape(seq, head_dim)
    sin_b = jnp.stack([zero, sin], axis=-1).reshape(seq, head_dim)
    return cos2, sin_a, sin_b


def _qkv_kernel(*refs, n_q_blocks, head_dim, latent, q_scale):
    if latent:
        (x_ref, shift_ref, scale_ref, gpre_ref, w_ref, qg_ref, kg_ref, cos_ref, sa_ref, sb_ref,
         q_ref, k_ref, v_ref, h_sc, a_sc) = refs
    else:
        (x_ref, shift_ref, scale_ref, gpre_ref, w_ref, qg_ref, kg_ref, q_ref, k_ref, v_ref, h_sc, a_sc) = refs
    j = pl.program_id(1)
    n_blocks = n_q_blocks + 2
    n_heads = a_sc.shape[2] // head_dim

    def head(slot, hh, gain):
        blk = _rms(a_sc[slot, :, hh * head_dim:(hh + 1) * head_dim], gain)
        if latent:
            nxt = pltpu.roll(blk, head_dim - 1, 1)
            prv = pltpu.roll(blk, 1, 1)
            blk = blk * cos_ref[...] + nxt * sa_ref[...] + prv * sb_ref[...]
        return blk

    def finish(block):
        slot = block % 2
        if block < n_q_blocks:
            for hh in range(n_heads):
                q_ref[:, hh * head_dim:(hh + 1) * head_dim] = (head(slot, hh, qg_ref[...]) * q_scale).astype(q_ref.dtype)
        elif block == n_q_blocks:
            for hh in range(n_heads):
                k_ref[:, hh * head_dim:(hh + 1) * head_dim] = head(slot, hh, kg_ref[...]).astype(k_ref.dtype)
        else:
            v_ref[...] = a_sc[slot].astype(v_ref.dtype)

    for step in range(n_blocks + 1):
        @pl.when(j == step)
        def _(step=step):
            if step == 0:
                _norm_mod_rows(x_ref, h_sc, gpre_ref, scale_ref, shift_ref)
            else:
                finish(step - 1)
            if step < n_blocks:
                a_sc[step % 2] = jnp.dot(h_sc[...], w_ref[...], preferred_element_type=F32)


def _qkv_call(x, mod_l, g_pre, layer, w_qkv, q_gain, k_gain, b_layer, n_kv, trunk, kv_dtype):
    rows, d = x.shape
    hd = q_gain.shape[-1]
    tn = n_kv * hd
    nq = w_qkv.shape[2] - 2 * tn
    assert nq % tn == 0
    nqb = nq // tn
    tm = _row_tile(trunk)
    in_specs = [
        pl.BlockSpec((tm, d), lambda i, j: (i, 0)),
        pl.BlockSpec((1, 1, d), _mod_spec(trunk, tm, 0)),
        pl.BlockSpec((1, 1, d), _mod_spec(trunk, tm, 1)),
        pl.BlockSpec((None, 1, d), lambda i, j: (layer, 0, 0)),
        pl.BlockSpec((None, d, tn), lambda i, j: (b_layer, 0, jnp.minimum(j, nqb + 1))),
        pl.BlockSpec((None, 1, hd), lambda i, j: (b_layer, 0, 0)),
        pl.BlockSpec((None, 1, hd), lambda i, j: (b_layer, 0, 0)),
    ]
    args = [x, mod_l, mod_l, g_pre, w_qkv, q_gain, k_gain]
    if trunk.latent:
        assert trunk.seq % tm == 0 and trunk.seq % GRID_W == 0
        tiles_per_seq = trunk.seq // tm
        in_specs += [pl.BlockSpec((tm, hd), lambda i, j: (i % tiles_per_seq, 0))] * 3
        args += list(_rope_tables(trunk.seq, hd))
    blocks = [((tm, d), F32), ((d, tn), BF16), ((tm, tn), BF16), ((tm, tn), kv_dtype), ((tm, tn), kv_dtype),
              ((tm, hd), F32), ((tm, hd), F32), ((tm, hd), F32)]
    scratch = [((tm, d), BF16), ((2, tm, tn), F32)]
    return pl.pallas_call(
        functools.partial(_qkv_kernel, n_q_blocks=nqb, head_dim=hd, latent=trunk.latent,
                          q_scale=hd ** -0.5 * LOG2_E),
        grid=(rows // tm, nqb + 3),
        in_specs=in_specs,
        out_specs=[
            pl.BlockSpec((tm, tn), lambda i, j: (i, jnp.clip(j - 1, 0, nqb - 1))),
            pl.BlockSpec((tm, tn), lambda i, j: (i, 0)),
            pl.BlockSpec((tm, tn), lambda i, j: (i, 0)),
        ],
        out_shape=[
            jax.ShapeDtypeStruct((rows, nq), BF16),
            jax.ShapeDtypeStruct((rows, tn), kv_dtype),
            jax.ShapeDtypeStruct((rows, tn), kv_dtype),
        ],
        scratch_shapes=[pltpu.VMEM(s, t) for s, t in scratch],
        compiler_params=_params(("parallel", "arbitrary"),
                                _vmem_limit(blocks, scratch, temps=[((tm, d), F32), ((tm, tn), F32)])),
        name="qkv",
    )(*args)


def _attn_kernel(q_ref, k_ref, v_ref, o_ref, *, q_per_kv, head_dim):
    hd = head_dim
    tq, tk = q_ref.shape[1], k_ref.shape[1]
    n_groups = ATTN_HEAD_GROUPS if q_per_kv % ATTN_HEAD_GROUPS == 0 else 1
    per = q_per_kv // n_groups
    k = k_ref[0].astype(BF16)
    v1 = jnp.concatenate([v_ref[0].astype(BF16), jnp.ones((tk, hd), BF16)], axis=1)
    qs = [jnp.concatenate([q_ref[0, :, g * hd:(g + 1) * hd] for g in range(i * per, (i + 1) * per)], axis=0)
          for i in range(n_groups)]
    ss = [lax.dot_general(q, k, (((1,), (1,)), ((), ())), preferred_element_type=F32) for q in qs]
    m_curs = [jnp.max(s, axis=-1, keepdims=True) for s in ss]

    def write(i, acc):
        out = acc[:, :hd] / acc[:, hd:]
        for j in range(per):
            g = i * per + j
            o_ref[0, :, g * hd:(g + 1) * hd] = out[j * tq:(j + 1) * tq].astype(o_ref.dtype)

    ps = [jnp.exp2(s - m).astype(BF16) for s, m in zip(ss, m_curs)]
    accs = [jnp.dot(p, v1, preferred_element_type=F32) for p in ps]
    for i, acc in enumerate(accs):
        write(i, acc)


def _attn_call(q, k, v, n_kv):
    b, t, nq = q.shape
    s = k.shape[1]
    hd = k.shape[2] // n_kv
    gw = nq // n_kv
    tq = min(ATTN_Q_TILE, t)
    assert t % tq == 0 and s <= ATTN_MAX_KEYS and s % V7X_LANES == 0 and hd % V7X_LANES == 0
    rows = (gw // hd) * tq
    blocks = [((1, tq, gw), BF16), ((1, s, hd), k.dtype), ((1, s, hd), v.dtype), ((1, tq, gw), BF16)]
    temps = [((rows, s), F32)] * 2 + [((rows, s), BF16), ((s, 2 * hd), BF16), ((rows, 2 * hd), F32)]
    return pl.pallas_call(
        functools.partial(_attn_kernel, q_per_kv=gw // hd, head_dim=hd),
        grid=(b, n_kv, t // tq),
        in_specs=[
            pl.BlockSpec((1, tq, gw), lambda bi, h, qi: (bi, qi, h)),
            pl.BlockSpec((1, s, hd), lambda bi, h, qi: (bi, 0, h)),
            pl.BlockSpec((1, s, hd), lambda bi, h, qi: (bi, 0, h)),
        ],
        out_specs=pl.BlockSpec((1, tq, gw), lambda bi, h, qi: (bi, qi, h)),
        out_shape=jax.ShapeDtypeStruct((b, t, nq), BF16),
        compiler_params=_params(("parallel", "parallel", "parallel"), _vmem_limit(blocks, temps=temps)),
        name="attn",
    )(q, k, v)


def _pool_kernel(x_ref, xp_ref, xn_ref, shift_ref, scale_ref, gate_ref, gpre_ref, gpost_ref, w_ref, cs_ref,
                 o_ref, h_sc, ta_sc, tb_sc, m_sc, *, tiles_per_seq):
    tm, d = x_ref.shape
    pad = POOL_HALO
    gdim = d // len(POOL_WINDOWS)
    gpre, scale, shift = gpre_ref[...], scale_ref[0], shift_ref[0]
    t_in_seq = lax.rem(pl.program_id(0), tiles_per_seq)
    keep_prev = (t_in_seq > 0).astype(F32)
    keep_next = (t_in_seq < tiles_per_seq - 1).astype(F32)
    zeros = jnp.zeros((pad, d), F32)
    h_sc[0:pad, :] = zeros
    h_sc[pad:2 * pad, :] = _norm_mod(xp_ref[...], gpre, scale, shift) * keep_prev
    h_sc[2 * pad:2 * pad + tm, :] = _norm_mod(x_ref[...], gpre, scale, shift)
    h_sc[2 * pad + tm:3 * pad + tm, :] = _norm_mod(xn_ref[...], gpre, scale, shift) * keep_next
    h_sc[3 * pad + tm:, :] = zeros
    for t_sc in (ta_sc, tb_sc):
        t_sc[0:pad, :] = zeros[:, :gdim]
        t_sc[3 * pad + tm:, :] = zeros[:, :gdim]

    ext = tm + 2 * pad

    def shifted_sum(src, cols, lo, hi):
        return src[pl.ds(pad + lo, ext), cols] + src[pl.ds(pad + hi, ext), cols]

    pos = t_in_seq * tm + lax.broadcasted_iota(jnp.int32, (tm, 1), 0)
    seq = tiles_per_seq * tm
    all_cols = pl.ds(0, gdim)
    for j, w in enumerate(POOL_WINDOWS):
        half = w // 2
        cols = pl.ds(j * gdim, gdim)
        src, src_cols, span = h_sc, cols, 1
        for dst in (ta_sc, tb_sc, ta_sc):
            if span * 2 >= w:
                break
            lo, hi = (-1, 0) if span == 1 else (-(span // 2), span // 2)
            dst[pl.ds(pad, ext), :] = shifted_sum(src, src_cols, lo, hi)
            src, src_cols, span = dst, all_cols, span * 2
        lo, hi = (-1, 0) if span == 1 else (-(span // 2), span // 2)
        win = src[pl.ds(2 * pad + lo, tm), src_cols] + src[pl.ds(2 * pad + hi, tm), src_cols]
        cnt = (jnp.minimum(pos + half, seq) - jnp.maximum(pos - half, 0)).astype(F32)
        p = win * (1.0 / cnt) - h_sc[pl.ds(2 * pad, tm), cols]
        y = jnp.dot(p.astype(BF16), w_ref[j], preferred_element_type=F32)
        m_sc[:, cols] = y * cs_ref[:, cols]
    o_ref[...] = x_ref[...] + gate_ref[0] * _rms(m_sc[...], gpost_ref[...])


def _pool_call(x, mod_l, g_pre, g_post, layer, w_pool, c_scale, c_layer, trunk):
    rows, d = x.shape
    tm = min(_row_tile(trunk), trunk.seq)
    assert trunk.seq % tm == 0 and tm % POOL_HALO == 0 and max(POOL_WINDOWS) // 2 <= POOL_HALO
    groups, gdim = w_pool.shape[1], w_pool.shape[2]
    assert groups == len(POOL_WINDOWS) and POOL_WINDOWS == (2, 4, 8, 16)
    halo_per_tile = tm // POOL_HALO
    last_halo = rows // POOL_HALO - 1
    blocks = [((tm, d), F32), ((tm, d), F32), ((groups, gdim, gdim), BF16)]
    ext_rows = tm + 4 * POOL_HALO
    scratch = [((ext_rows, d), F32), ((ext_rows, gdim), F32), ((ext_rows, gdim), F32), ((tm, d), F32)]
    return pl.pallas_call(
        functools.partial(_pool_kernel, tiles_per_seq=trunk.seq // tm),
        grid=(rows // tm,),
        in_specs=[
            pl.BlockSpec((tm, d), lambda i: (i, 0)),
            pl.BlockSpec((POOL_HALO, d), lambda i: (jnp.maximum(i * halo_per_tile - 1, 0), 0)),
            pl.BlockSpec((POOL_HALO, d), lambda i: (jnp.minimum((i + 1) * halo_per_tile, last_halo), 0)),
            pl.BlockSpec((1, 1, d), _mod_spec(trunk, tm, 0)),
            pl.BlockSpec((1, 1, d), _mod_spec(trunk, tm, 1)),
            pl.BlockSpec((1, 1, d), _mod_spec(trunk, tm, 2)),
            pl.BlockSpec((None, 1, d), lambda i: (layer, 0, 0)),
            pl.BlockSpec((None, 1, d), lambda i: (layer, 0, 0)),
            pl.BlockSpec((None, groups, gdim, gdim), lambda i: (c_layer, 0, 0, 0)),
            pl.BlockSpec((None, 1, d), lambda i: (c_layer, 0, 0)),
        ],
        out_specs=pl.BlockSpec((tm, d), lambda i: (i, 0)),
        out_shape=jax.ShapeDtypeStruct((rows, d), F32),
        scratch_shapes=[pltpu.VMEM(s, t) for s, t in scratch],
        compiler_params=_params(("parallel",), _vmem_limit(blocks, scratch, temps=[((tm, d), F32)] * 2)),
        name="pool",
    )(x, x, x, mod_l, mod_l, mod_l, g_pre, g_post, w_pool, c_scale)


def _trunk(x3, trunk, mod, cache_kv, p):
    b, t, d = x3.shape
    x = x3.reshape(b * t, d)
    depth = p["norm_mix_pre"].shape[0]
    n_kv = p["n_kv"]
    ia = ib = ic = 0
    new_k, new_v = [], []
    for l in range(depth):
        mod_l = mod[l]
        g_pre, g_post = p["norm_mix_pre"], p["norm_mix_post"]
        kind = l % 3
        if kind == 0:
            y = _mixa_call(x, mod_l, g_pre, l, p["a_w_in"], p["a_norm_v"], p["a_w_s"], p["a_b_s"], ia, trunk)
            x = _proj_call(y, p["a_w_out"], ia, x, mod_l, g_post, l, trunk)
            ia += 1
        elif kind == 1:
            kv_dtype = BF16 if trunk.latent else F32
            q, k, v = _qkv_call(x, mod_l, g_pre, l, p["b_w_qkv"], p["b_q_norm"], p["b_k_norm"], ib, n_kv, trunk,
                                kv_dtype)
            k3, v3 = k.reshape(b, t, -1), v.reshape(b, t, -1)
            if trunk.latent:
                ck, cv = cache_kv
                past = ck.shape[2]
                k3 = jnp.concatenate([k3, ck[:, ib].reshape(b, past, -1).astype(BF16)], axis=1)
                v3 = jnp.concatenate([v3, cv[:, ib].reshape(b, past, -1).astype(BF16)], axis=1)
            else:
                new_k.append(k3.reshape(b, t, n_kv, -1))
                new_v.append(v3.reshape(b, t, n_kv, -1))
            o = _attn_call(q.reshape(b, t, -1), k3, v3, n_kv)
            x = _proj_call(o.reshape(b * t, -1), p["b_w_o"], ib, x, mod_l, g_post, l, trunk)
            ib += 1
        else:
            x = _pool_call(x, mod_l, g_pre, g_post, l, p["c_w_pool"], p["c_scale"], ic, trunk)
            ic += 1
        x = _ffn_call(x, mod_l, p["norm_ffn_pre"], p["norm_ffn_post"], p["f_w_gu"], p["f_w_down"], l, trunk)
    return x.reshape(b, t, d), new_k, new_v


def kernel(x_prompt, x_sample, cache_k, cache_v, c, c_ctx, w_mod, b_mod, norm_mix_pre, norm_mix_post, norm_ffn_pre, norm_ffn_post, a_w_in, a_norm_v, a_w_s, a_b_s, a_w_out, b_w_qkv, b_q_norm, b_k_norm, b_w_o, c_w_pool, c_scale, f_w_gu, f_w_down):
    batch, seq, d = x_prompt.shape
    dec_batch, dec_seq, _ = x_sample.shape
    depth = w_mod.shape[0]
    assert 1 + dec_batch <= MOD_ROWS

    cond = jnp.concatenate([c_ctx[None, :], c, jnp.zeros((MOD_ROWS - 1 - dec_batch, d), F32)], axis=0)
    mod = _mod_call(cond, w_mod, b_mod).reshape(depth, MOD_ROWS * N_MOD, 1, d)

    rows3 = lambda a: a.reshape(a.shape[0], 1, a.shape[-1])
    p = {
        "n_kv": cache_k.shape[3],
        "norm_mix_pre": rows3(norm_mix_pre), "norm_mix_post": rows3(norm_mix_post),
        "norm_ffn_pre": rows3(norm_ffn_pre), "norm_ffn_post": rows3(norm_ffn_post),
        "a_w_in": a_w_in.astype(BF16), "a_norm_v": rows3(a_norm_v), "a_w_s": a_w_s.astype(BF16),
        "a_b_s": a_b_s[..., None], "a_w_out": a_w_out.astype(BF16),
        "b_w_qkv": b_w_qkv.astype(BF16), "b_q_norm": rows3(b_q_norm), "b_k_norm": rows3(b_k_norm),
        "b_w_o": b_w_o.astype(BF16),
        "c_w_pool": c_w_pool.astype(BF16), "c_scale": rows3(c_scale),
        "f_w_gu": f_w_gu.astype(BF16), "f_w_down": f_w_down.astype(BF16),
    }
    ctx = Trunk(batch=batch, seq=seq, mod_base=0, rows_per_cond=batch * seq, latent=False)
    lat = Trunk(batch=dec_batch, seq=dec_seq, mod_base=1, rows_per_cond=dec_seq, latent=True)
    y_prompt, new_k, new_v = _trunk(x_prompt, ctx, mod, None, p)
    y_sample, _, _ = _trunk(x_sample, lat, mod, (cache_k, cache_v), p)
    return (y_prompt, y_sample, jnp.stack(new_k, axis=1), jnp.stack(new_v, axis=1))
```

```python
import collections
import functools

import jax
import jax.numpy as jnp
from jax import lax
from jax.experimental import pallas as pl
from jax.experimental.pallas import tpu as pltpu

F32 = jnp.float32
BF16 = jnp.bfloat16

EPS = 1e-6
N_MOD = 6
GRID_W = 64
ROPE_THETA = 10000.0
POOL_WINDOWS = (2, 4, 8, 16)
POOL_HALO = 8
MOD_ROWS = 16
LOG2_E = 1.4426950408889634
V7X_LANES = 128

V7X_VMEM_BUDGET = 58 * 1024 * 1024
ROW_TILE = 512
ROW_GROUP = 16
ROW_GROUP_UNROLL = 8
FFN_HIDDEN_TILE = 512
COL_TILE = 512
ATTN_Q_TILE = 256
ATTN_MAX_KEYS = 4608
ATTN_HEAD_GROUPS = 2
MOD_COL_TILE = 1024

Trunk = collections.namedtuple("Trunk", "batch seq mod_base rows_per_cond latent")


def _nbytes(shape, dtype):
    n = 1
    for s in shape:
        n *= s
    return n * jnp.dtype(dtype).itemsize


def _vmem_limit(pipelined, scratch=(), temps=()):
    total = 2 * sum(_nbytes(s, d) for s, d in pipelined)
    total += sum(_nbytes(s, d) for s, d in scratch)
    total += sum(_nbytes(s, d) for s, d in temps)
    return int(min(max(total + (4 << 20), 16 << 20), V7X_VMEM_BUDGET))


def _params(semantics, limit):
    return pltpu.CompilerParams(dimension_semantics=semantics, vmem_limit_bytes=limit)


def _rms(xf, g):
    ms = jnp.mean(xf * xf, axis=-1, keepdims=True)
    return (xf * lax.rsqrt(ms + EPS)) * g


def _norm_mod(xf, g, scale, shift):
    return _rms(xf, g) * (1.0 + scale) + shift


def _row_groups(n_rows, body):
    assert n_rows % ROW_GROUP == 0

    def step(r, carry):
        body(pl.ds(pl.multiple_of(r * ROW_GROUP, ROW_GROUP), ROW_GROUP))
        return carry
    lax.fori_loop(0, n_rows // ROW_GROUP, step, 0, unroll=ROW_GROUP_UNROLL)


def _norm_mod_rows(x_ref, dst_ref, gpre_ref, scale_ref, shift_ref):
    d = x_ref.shape[1]
    gain = jnp.broadcast_to(gpre_ref[...] * (1.0 + scale_ref[0]), (ROW_GROUP, d))
    shift = jnp.broadcast_to(shift_ref[0], (ROW_GROUP, d))

    def body(rows):
        xr = x_ref[rows, :]
        ms = jnp.mean(xr * xr, axis=-1, keepdims=True)
        h = (xr * lax.rsqrt(ms + EPS)) * gain + shift
        dst_ref[rows, :] = h.astype(dst_ref.dtype)
    _row_groups(x_ref.shape[0], body)


def _gated_residual_rows(f_ref, x_ref, o_ref, gate_ref, gpost_ref):
    d = x_ref.shape[1]
    gain = jnp.broadcast_to(gate_ref[0] * gpost_ref[...], (ROW_GROUP, d))

    def body(rows):
        f = f_ref[rows, :]
        ms = jnp.mean(f * f, axis=-1, keepdims=True)
        o_ref[rows, :] = x_ref[rows, :] + (f * lax.rsqrt(ms + EPS)) * gain
    _row_groups(x_ref.shape[0], body)


def _col_blocks(w, tn):
    n, d, cols = w.shape
    tn = min(tn, cols)
    assert cols % tn == 0
    return w.astype(BF16).reshape(n, d, cols // tn, tn).transpose(0, 2, 1, 3)


def _row_tile(trunk):
    rows = trunk.batch * trunk.seq
    tm = min(ROW_TILE, rows)
    assert rows % tm == 0 and (trunk.rows_per_cond % tm == 0)
    return tm


def _mod_spec(trunk, tm, m, ahead=0):
    last_tile = trunk.batch * trunk.seq // tm - 1

    def index(i, *_):
        tile = jnp.minimum(i + ahead, last_tile)
        return ((trunk.mod_base + (tile * tm) // trunk.rows_per_cond) * N_MOD + m, 0, 0)
    return index


def _mod_kernel(c_ref, w_ref, b_ref, o_ref):
    c = c_ref[...]
    s = (c * jax.nn.sigmoid(c)).astype(BF16)
    o_ref[0] = jnp.dot(s, w_ref[0].astype(BF16), preferred_element_type=F32) + b_ref[0]


def _mod_call(cond, w_mod, b_mod):
    depth, d, n = w_mod.shape
    tn = min(MOD_COL_TILE, n)
    assert n % tn == 0
    blocks = [((MOD_ROWS, d), F32), ((1, d, tn), F32), ((1, 1, tn), F32), ((1, MOD_ROWS, tn), F32)]
    return pl.pallas_call(
        _mod_kernel,
        grid=(depth, n // tn),
        in_specs=[
            pl.BlockSpec((MOD_ROWS, d), lambda l, j: (0, 0)),
            pl.BlockSpec((1, d, tn), lambda l, j: (l, 0, j)),
            pl.BlockSpec((1, 1, tn), lambda l, j: (l, 0, j)),
        ],
        out_specs=pl.BlockSpec((1, MOD_ROWS, tn), lambda l, j: (l, 0, j)),
        out_shape=jax.ShapeDtypeStruct((depth, MOD_ROWS, n), F32),
        compiler_params=_params(("parallel", "parallel"), _vmem_limit(blocks, temps=[((d, tn), BF16)])),
        name="mod",
    )(cond, w_mod, b_mod.reshape(depth, 1, n))


def _ffn_kernel(x_ref, shift_ref, scale_ref, gate_ref, gpre_ref, gpost_ref, wg_ref, wu_ref, wd_ref,
                o_ref, h_sc, acc_sc, *, n_chunks):
    c = pl.program_id(1)

    @pl.when(c == 0)
    def _():
        _norm_mod_rows(x_ref, h_sc, gpre_ref, scale_ref, shift_ref)
        acc_sc[...] = jnp.zeros_like(acc_sc)

    h = h_sc[...]
    g = jnp.dot(h, wg_ref[...], preferred_element_type=F32)
    u = jnp.dot(h, wu_ref[...], preferred_element_type=F32)
    a = (g * jax.nn.sigmoid(g)) * u
    acc_sc[...] += jnp.dot(a.astype(BF16), wd_ref[...], preferred_element_type=F32)

    @pl.when(c == n_chunks - 1)
    def _():
        _gated_residual_rows(acc_sc, x_ref, o_ref, gate_ref, gpost_ref)


def _ffn_call(x, mod_l, g_pre, g_post, w_gu, w_down, layer, trunk):
    rows, d = x.shape
    hidden = w_down.shape[1]
    tm = _row_tile(trunk)
    th = min(FFN_HIDDEN_TILE, hidden)
    assert hidden % th == 0
    nc = hidden // th
    blocks = [((tm, d), F32)] * 2 + [((d, th), BF16), ((d, th), BF16), ((th, d), BF16)]
    scratch = [((tm, d), BF16), ((tm, d), F32)]
    temps = [((tm, th), F32)] * 4 + [((tm, d), F32)]
    return pl.pallas_call(
        functools.partial(_ffn_kernel, n_chunks=nc),
        grid=(rows // tm, nc),
        in_specs=[
            pl.BlockSpec((tm, d), lambda i, c: (i, 0)),
            pl.BlockSpec((1, 1, d), _mod_spec(trunk, tm, 3)),
            pl.BlockSpec((1, 1, d), _mod_spec(trunk, tm, 4)),
            pl.BlockSpec((1, 1, d), _mod_spec(trunk, tm, 5)),
            pl.BlockSpec((None, 1, d), lambda i, c: (layer, 0, 0)),
            pl.BlockSpec((None, 1, d), lambda i, c: (layer, 0, 0)),
            pl.BlockSpec((None, d, th), lambda i, c: (layer, 0, c)),
            pl.BlockSpec((None, d, th), lambda i, c: (layer, 0, nc + c)),
            pl.BlockSpec((None, th, d), lambda i, c: (layer, c, 0)),
        ],
        out_specs=pl.BlockSpec((tm, d), lambda i, c: (i, 0)),
        out_shape=jax.ShapeDtypeStruct((rows, d), F32),
        scratch_shapes=[pltpu.VMEM(s, t) for s, t in scratch],
        compiler_params=_params(("parallel", "arbitrary"), _vmem_limit(blocks, scratch, temps)),
        name="ffn",
    )(x, mod_l, mod_l, mod_l, g_pre, g_post, w_gu, w_gu, w_down)


def _proj_kernel(y_ref, w_ref, x_ref, gate_ref, gpost_ref, o_ref):
    m = jnp.dot(y_ref[...], w_ref[...], preferred_element_type=F32)
    o_ref[...] = x_ref[...] + gate_ref[0] * _rms(m, gpost_ref[...])


def _proj_call(y, w, w_layer, x, mod_l, g_post, layer, trunk):
    rows, d = x.shape
    k = y.shape[1]
    tm = _row_tile(trunk)
    blocks = [((tm, k), BF16), ((k, d), BF16), ((tm, d), F32), ((tm, d), F32)]
    return pl.pallas_call(
        _proj_kernel,
        grid=(rows // tm,),
        in_specs=[
            pl.BlockSpec((tm, k), lambda i: (i, 0)),
            pl.BlockSpec((None, k, d), lambda i: (w_layer, 0, 0)),
            pl.BlockSpec((tm, d), lambda i: (i, 0)),
            pl.BlockSpec((1, 1, d), _mod_spec(trunk, tm, 2)),
            pl.BlockSpec((None, 1, d), lambda i: (layer, 0, 0)),
        ],
        out_specs=pl.BlockSpec((tm, d), lambda i: (i, 0)),
        out_shape=jax.ShapeDtypeStruct((rows, d), F32),
        compiler_params=_params(("parallel",), _vmem_limit(blocks, temps=[((tm, d), F32)] * 2)),
        name="proj",
    )(y, w, x, mod_l, g_post)


def _mixa_kernel(x_ref, shift_ref, scale_ref, gpre_ref, win_ref, gv_ref, ws_ref, bs_ref,
                 y_ref, h_sc, a_sc, *, n_blocks, n_u_blocks, chunk, groups):
    j = pl.program_id(1)

    @pl.when(j == 0)
    def _():
        _norm_mod_rows(x_ref, h_sc, gpre_ref, scale_ref, shift_ref)

    a_sc[j] = jnp.dot(h_sc[...], win_ref[j], preferred_element_type=F32)

    @pl.when(j == n_blocks - 1)
    def _():
        tm, tn = a_sc.shape[1], a_sc.shape[2]
        width = n_u_blocks * tn
        gdim = width // groups
        ss = jnp.zeros((tm, 1), F32)
        for b in range(n_u_blocks, n_blocks):
            vb = a_sc[b]
            ss = ss + jnp.sum(vb * vb, axis=-1, keepdims=True)
        inv = lax.rsqrt(ss / width + EPS)
        for g in range(groups):
            col = g * gdim
            b, off = col // tn, col % tn
            gv = gv_ref[:, col:col + gdim]
            for c in range(tm // chunk):
                r = c * chunk
                v = (a_sc[n_u_blocks + b, r:r + chunk, off:off + gdim] * inv[r:r + chunk]) * gv
                s = jnp.dot(ws_ref[g], v.astype(BF16), preferred_element_type=F32) + bs_ref[g]
                u = a_sc[b, r:r + chunk, off:off + gdim]
                y_ref[r:r + chunk, col:col + gdim] = (u * s).astype(y_ref.dtype)


def _mixa_call(x, mod_l, g_pre, layer, w_in, g_v, w_s, b_s, a_layer, trunk):
    rows, d = x.shape
    nb, tn = w_in.shape[1], w_in.shape[3]
    width = nb * tn // 2
    groups, chunk = w_s.shape[1], w_s.shape[2]
    tm = _row_tile(trunk)
    gdim = width // groups
    assert width % tn == 0 and tn % gdim == 0 and tm % chunk == 0 and trunk.seq % chunk == 0
    blocks = [((tm, d), F32), ((tm, width), BF16), ((groups, chunk, chunk), BF16), ((groups, chunk, V7X_LANES), F32)]
    scratch = [((tm, d), BF16), ((nb, tm, tn), F32)]
    resident = [((nb, d, tn), BF16)]
    return pl.pallas_call(
        functools.partial(_mixa_kernel, n_blocks=nb, n_u_blocks=nb // 2, chunk=chunk, groups=groups),
        grid=(rows // tm, nb),
        in_specs=[
            pl.BlockSpec((tm, d), lambda i, j: (i, 0)),
            pl.BlockSpec((1, 1, d), _mod_spec(trunk, tm, 0)),
            pl.BlockSpec((1, 1, d), _mod_spec(trunk, tm, 1)),
            pl.BlockSpec((None, 1, d), lambda i, j: (layer, 0, 0)),
            pl.BlockSpec((None, nb, d, tn), lambda i, j: (a_layer, 0, 0, 0), pipeline_mode=pl.Buffered(1)),
            pl.BlockSpec((None, 1, width), lambda i, j: (a_layer, 0, 0)),
            pl.BlockSpec((None, groups, chunk, chunk), lambda i, j: (a_layer, 0, 0, 0)),
            pl.BlockSpec((None, groups, chunk, 1), lambda i, j: (a_layer, 0, 0, 0)),
        ],
        out_specs=pl.BlockSpec((tm, width), lambda i, j: (i, 0)),
        out_shape=jax.ShapeDtypeStruct((rows, width), BF16),
        scratch_shapes=[pltpu.VMEM(s, t) for s, t in scratch],
        compiler_params=_params(("parallel", "arbitrary"),
                                _vmem_limit(blocks, scratch + resident, temps=[((tm, d), F32), ((tm, tn), F32)])),
        name="mixa",
    )(x, mod_l, mod_l, g_pre, w_in, g_v, w_s, b_s)


def _rope_tables(seq, head_dim):
    axis_dim = head_dim // 2
    t = jnp.arange(seq)
    n_rows = seq // GRID_W
    row = jnp.minimum(t // GRID_W, n_rows - 1).astype(F32)
    col = (t % GRID_W).astype(F32)
    inv = jnp.power(ROPE_THETA, -jnp.arange(0, axis_dim, 2, dtype=F32) / axis_dim)
    ang = jnp.concatenate([row[:, None] * inv, col[:, None] * inv], axis=-1)
    cos, sin = jnp.cos(ang), jnp.sin(ang)
    zero = jnp.zeros_like(sin)
    cos2 = jnp.stack([cos, cos], axis=-1).reshape(seq, head_dim)
    sin_a = jnp.stack([-sin, zero], axis=-1).reshape(seq, head_dim)
    sin_b = jnp.stack([zero, sin], axis=-1).reshape(seq, head_dim)
    return cos2, sin_a, sin_b


def _qkv_kernel(*refs, n_q_blocks, head_dim, latent, q_scale):
    if latent:
        (x_ref, shift_ref, scale_ref, gpre_ref, w_ref, qg_ref, kg_ref, cos_ref, sa_ref, sb_ref,
         q_ref, k_ref, v_ref, h_sc, a_sc) = refs
    else:
        (x_ref, shift_ref, scale_ref, gpre_ref, w_ref, qg_ref, kg_ref, q_ref, k_ref, v_ref, h_sc, a_sc) = refs
    j = pl.program_id(1)
    n_blocks = n_q_blocks + 2
    n_heads = a_sc.shape[2] // head_dim

    def head(slot, hh, gain):
        blk = _rms(a_sc[slot, :, hh * head_dim:(hh + 1) * head_dim], gain)
        if latent:
            nxt = pltpu.roll(blk, head_dim - 1, 1)
            prv = pltpu.roll(blk, 1, 1)
            blk = blk * cos_ref[...] + nxt * sa_ref[...] + prv * sb_ref[...]
        return blk

    def finish(block):
        slot = block % 2
        if block < n_q_blocks:
            for hh in range(n_heads):
                q_ref[:, hh * head_dim:(hh + 1) * head_dim] = (head(slot, hh, qg_ref[...]) * q_scale).astype(q_ref.dtype)
        elif block == n_q_blocks:
            for hh in range(n_heads):
                k_ref[:, hh * head_dim:(hh + 1) * head_dim] = head(slot, hh, kg_ref[...]).astype(k_ref.dtype)
        else:
            v_ref[...] = a_sc[slot].astype(v_ref.dtype)

    for step in range(n_blocks + 1):
        @pl.when(j == step)
        def _(step=step):
            if step == 0:
                _norm_mod_rows(x_ref, h_sc, gpre_ref, scale_ref, shift_ref)
            else:
                finish(step - 1)
            if step < n_blocks:
                w = w_ref[:, step * a_sc.shape[2]:(step + 1) * a_sc.shape[2]]
                a_sc[step % 2] = jnp.dot(h_sc[...], w, preferred_element_type=F32)


def _qkv_call(x, mod_l, g_pre, layer, w_qkv, q_gain, k_gain, b_layer, n_kv, trunk, kv_dtype):
    rows, d = x.shape
    hd = q_gain.shape[-1]
    tn = n_kv * hd
    nq = w_qkv.shape[2] - 2 * tn
    assert nq % tn == 0
    nqb = nq // tn
    tm = _row_tile(trunk)
    in_specs = [
        pl.BlockSpec((tm, d), lambda i, j: (i, 0)),
        pl.BlockSpec((1, 1, d), _mod_spec(trunk, tm, 0)),
        pl.BlockSpec((1, 1, d), _mod_spec(trunk, tm, 1)),
        pl.BlockSpec((None, 1, d), lambda i, j: (layer, 0, 0)),
        pl.BlockSpec((None, d, (nqb + 2) * tn), lambda i, j: (b_layer, 0, 0), pipeline_mode=pl.Buffered(1)),
        pl.BlockSpec((None, 1, hd), lambda i, j: (b_layer, 0, 0)),
        pl.BlockSpec((None, 1, hd), lambda i, j: (b_layer, 0, 0)),
    ]
    args = [x, mod_l, mod_l, g_pre, w_qkv, q_gain, k_gain]
    if trunk.latent:
        assert trunk.seq % tm == 0 and trunk.seq % GRID_W == 0
        tiles_per_seq = trunk.seq // tm
        in_specs += [pl.BlockSpec((tm, hd), lambda i, j: (i % tiles_per_seq, 0))] * 3
        args += list(_rope_tables(trunk.seq, hd))
    blocks = [((tm, d), F32), ((tm, tn), BF16), ((tm, tn), kv_dtype), ((tm, tn), kv_dtype),
              ((tm, hd), F32), ((tm, hd), F32), ((tm, hd), F32)]
    scratch = [((tm, d), BF16), ((2, tm, tn), F32)]
    resident = [((d, (nqb + 2) * tn), BF16)]
    return pl.pallas_call(
        functools.partial(_qkv_kernel, n_q_blocks=nqb, head_dim=hd, latent=trunk.latent,
                          q_scale=hd ** -0.5 * LOG2_E),
        grid=(rows // tm, nqb + 3),
        in_specs=in_specs,
        out_specs=[
            pl.BlockSpec((tm, tn), lambda i, j: (i, jnp.clip(j - 1, 0, nqb - 1))),
            pl.BlockSpec((tm, tn), lambda i, j: (i, 0)),
            pl.BlockSpec((tm, tn), lambda i, j: (i, 0)),
        ],
        out_shape=[
            jax.ShapeDtypeStruct((rows, nq), BF16),
            jax.ShapeDtypeStruct((rows, tn), kv_dtype),
            jax.ShapeDtypeStruct((rows, tn), kv_dtype),
        ],
        scratch_shapes=[pltpu.VMEM(s, t) for s, t in scratch],
        compiler_params=_params(("parallel", "arbitrary"),
                                _vmem_limit(blocks, scratch + resident, temps=[((tm, d), F32), ((tm, tn), F32)])),
        name="qkv",
    )(*args)


def _attn_kernel(q_ref, k_ref, v_ref, o_ref, *, q_per_kv, head_dim):
    hd = head_dim
    tq, tk = q_ref.shape[1], k_ref.shape[1]
    n_groups = ATTN_HEAD_GROUPS if q_per_kv % ATTN_HEAD_GROUPS == 0 else 1
    per = q_per_kv // n_groups
    k = k_ref[0].astype(BF16)
    v1 = jnp.concatenate([v_ref[0].astype(BF16), jnp.ones((tk, hd), BF16)], axis=1)
    qs = [jnp.concatenate([q_ref[0, :, g * hd:(g + 1) * hd] for g in range(i * per, (i + 1) * per)], axis=0)
          for i in range(n_groups)]
    ss = [lax.dot_general(q, k, (((1,), (1,)), ((), ())), preferred_element_type=F32) for q in qs]
    m_curs = [jnp.max(s, axis=-1, keepdims=True) for s in ss]

    def write(i, acc):
        out = acc[:, :hd] / acc[:, hd:]
        for j in range(per):
            g = i * per + j
            o_ref[0, :, g * hd:(g + 1) * hd] = out[j * tq:(j + 1) * tq].astype(o_ref.dtype)

    ps = [jnp.exp2(s - m).astype(BF16) for s, m in zip(ss, m_curs)]
    accs = [jnp.dot(p, v1, preferred_element_type=F32) for p in ps]
    for i, acc in enumerate(accs):
        write(i, acc)


def _attn_call(q, k, v, n_kv):
    b, t, nq = q.shape
    s = k.shape[1]
    hd = k.shape[2] // n_kv
    gw = nq // n_kv
    tq = min(ATTN_Q_TILE, t)
    assert t % tq == 0 and s <= ATTN_MAX_KEYS and s % V7X_LANES == 0 and hd % V7X_LANES == 0
    rows = (gw // hd) * tq
    blocks = [((1, tq, gw), BF16), ((1, s, hd), k.dtype), ((1, s, hd), v.dtype), ((1, tq, gw), BF16)]
    temps = [((rows, s), F32)] * 2 + [((rows, s), BF16), ((s, 2 * hd), BF16), ((rows, 2 * hd), F32)]
    return pl.pallas_call(
        functools.partial(_attn_kernel, q_per_kv=gw // hd, head_dim=hd),
        grid=(b, n_kv, t // tq),
        in_specs=[
            pl.BlockSpec((1, tq, gw), lambda bi, h, qi: (bi, qi, h)),
            pl.BlockSpec((1, s, hd), lambda bi, h, qi: (bi, 0, h)),
            pl.BlockSpec((1, s, hd), lambda bi, h, qi: (bi, 0, h)),
        ],
        out_specs=pl.BlockSpec((1, tq, gw), lambda bi, h, qi: (bi, qi, h)),
        out_shape=jax.ShapeDtypeStruct((b, t, nq), BF16),
        compiler_params=_params(("parallel", "parallel", "parallel"), _vmem_limit(blocks, temps=temps)),
        name="attn",
    )(q, k, v)


def _pool_kernel(x_ref, xp_ref, xn_ref, shift_ref, scale_ref, gate_ref, gpre_ref, gpost_ref, w_ref, cs_ref,
                 o_ref, h_sc, ta_sc, tb_sc, m_sc, *, tiles_per_seq):
    tm, d = x_ref.shape
    pad = POOL_HALO
    gdim = d // len(POOL_WINDOWS)
    gpre, scale, shift = gpre_ref[...], scale_ref[0], shift_ref[0]
    t_in_seq = lax.rem(pl.program_id(0), tiles_per_seq)
    keep_prev = (t_in_seq > 0).astype(F32)
    keep_next = (t_in_seq < tiles_per_seq - 1).astype(F32)
    zeros = jnp.zeros((pad, d), F32)
    h_sc[0:pad, :] = zeros
    h_sc[pad:2 * pad, :] = _norm_mod(xp_ref[...], gpre, scale, shift) * keep_prev
    h_sc[2 * pad:2 * pad + tm, :] = _norm_mod(x_ref[...], gpre, scale, shift)
    h_sc[2 * pad + tm:3 * pad + tm, :] = _norm_mod(xn_ref[...], gpre, scale, shift) * keep_next
    h_sc[3 * pad + tm:, :] = zeros
    for t_sc in (ta_sc, tb_sc):
        t_sc[0:pad, :] = zeros[:, :gdim]
        t_sc[3 * pad + tm:, :] = zeros[:, :gdim]

    ext = tm + 2 * pad

    def shifted_sum(src, cols, lo, hi):
        return src[pl.ds(pad + lo, ext), cols] + src[pl.ds(pad + hi, ext), cols]

    pos = t_in_seq * tm + lax.broadcasted_iota(jnp.int32, (tm, 1), 0)
    seq = tiles_per_seq * tm
    all_cols = pl.ds(0, gdim)
    for j, w in enumerate(POOL_WINDOWS):
        half = w // 2
        cols = pl.ds(j * gdim, gdim)
        src, src_cols, span = h_sc, cols, 1
        for dst in (ta_sc, tb_sc, ta_sc):
            if span * 2 >= w:
                break
            lo, hi = (-1, 0) if span == 1 else (-(span // 2), span // 2)
            dst[pl.ds(pad, ext), :] = shifted_sum(src, src_cols, lo, hi)
            src, src_cols, span = dst, all_cols, span * 2
        lo, hi = (-1, 0) if span == 1 else (-(span // 2), span // 2)
        win = src[pl.ds(2 * pad + lo, tm), src_cols] + src[pl.ds(2 * pad + hi, tm), src_cols]
        cnt = (jnp.minimum(pos + half, seq) - jnp.maximum(pos - half, 0)).astype(F32)
        p = win * (1.0 / cnt) - h_sc[pl.ds(2 * pad, tm), cols]
        y = jnp.dot(p.astype(BF16), w_ref[j], preferred_element_type=F32)
        m_sc[:, cols] = y * cs_ref[:, cols]
    o_ref[...] = x_ref[...] + gate_ref[0] * _rms(m_sc[...], gpost_ref[...])


def _pool_call(x, mod_l, g_pre, g_post, layer, w_pool, c_scale, c_layer, trunk):
    rows, d = x.shape
    tm = min(_row_tile(trunk), trunk.seq)
    assert trunk.seq % tm == 0 and tm % POOL_HALO == 0 and max(POOL_WINDOWS) // 2 <= POOL_HALO
    groups, gdim = w_pool.shape[1], w_pool.shape[2]
    assert groups == len(POOL_WINDOWS) and POOL_WINDOWS == (2, 4, 8, 16)
    halo_per_tile = tm // POOL_HALO
    last_halo = rows // POOL_HALO - 1
    blocks = [((tm, d), F32), ((tm, d), F32), ((groups, gdim, gdim), BF16)]
    ext_rows = tm + 4 * POOL_HALO
    scratch = [((ext_rows, d), F32), ((ext_rows, gdim), F32), ((ext_rows, gdim), F32), ((tm, d), F32)]
    return pl.pallas_call(
        functools.partial(_pool_kernel, tiles_per_seq=trunk.seq // tm),
        grid=(rows // tm,),
        in_specs=[
            pl.BlockSpec((tm, d), lambda i: (i, 0)),
            pl.BlockSpec((POOL_HALO, d), lambda i: (jnp.maximum(i * halo_per_tile - 1, 0), 0)),
            pl.BlockSpec((POOL_HALO, d), lambda i: (jnp.minimum((i + 1) * halo_per_tile, last_halo), 0)),
            pl.BlockSpec((1, 1, d), _mod_spec(trunk, tm, 0)),
            pl.BlockSpec((1, 1, d), _mod_spec(trunk, tm, 1)),
            pl.BlockSpec((1, 1, d), _mod_spec(trunk, tm, 2)),
            pl.BlockSpec((None, 1, d), lambda i: (layer, 0, 0)),
            pl.BlockSpec((None, 1, d), lambda i: (layer, 0, 0)),
            pl.BlockSpec((None, groups, gdim, gdim), lambda i: (c_layer, 0, 0, 0)),
            pl.BlockSpec((None, 1, d), lambda i: (c_layer, 0, 0)),
        ],
        out_specs=pl.BlockSpec((tm, d), lambda i: (i, 0)),
        out_shape=jax.ShapeDtypeStruct((rows, d), F32),
        scratch_shapes=[pltpu.VMEM(s, t) for s, t in scratch],
        compiler_params=_params(("parallel",), _vmem_limit(blocks, scratch, temps=[((tm, d), F32)] * 2)),
        name="pool",
    )(x, x, x, mod_l, mod_l, mod_l, g_pre, g_post, w_pool, c_scale)


def _trunk(x3, trunk, mod, cache_kv, p):
    b, t, d = x3.shape
    x = x3.reshape(b * t, d)
    depth = p["norm_mix_pre"].shape[0]
    n_kv = p["n_kv"]
    ia = ib = ic = 0
    new_k, new_v = [], []
    for l in range(depth):
        mod_l = mod[l]
        g_pre, g_post = p["norm_mix_pre"], p["norm_mix_post"]
        kind = l % 3
        if kind == 0:
            y = _mixa_call(x, mod_l, g_pre, l, p["a_w_in"], p["a_norm_v"], p["a_w_s"], p["a_b_s"], ia, trunk)
            x = _proj_call(y, p["a_w_out"], ia, x, mod_l, g_post, l, trunk)
            ia += 1
        elif kind == 1:
            kv_dtype = BF16 if trunk.latent else F32
            q, k, v = _qkv_call(x, mod_l, g_pre, l, p["b_w_qkv"], p["b_q_norm"], p["b_k_norm"], ib, n_kv, trunk,
                                kv_dtype)
            k3, v3 = k.reshape(b, t, -1), v.reshape(b, t, -1)
            if trunk.latent:
                ck, cv = cache_kv
                past = ck.shape[2]
                k3 = jnp.concatenate([k3, ck[:, ib].reshape(b, past, -1).astype(BF16)], axis=1)
                v3 = jnp.concatenate([v3, cv[:, ib].reshape(b, past, -1).astype(BF16)], axis=1)
            else:
                new_k.append(k3.reshape(b, t, n_kv, -1))
                new_v.append(v3.reshape(b, t, n_kv, -1))
            o = _attn_call(q.reshape(b, t, -1), k3, v3, n_kv)
            x = _proj_call(o.reshape(b * t, -1), p["b_w_o"], ib, x, mod_l, g_post, l, trunk)
            ib += 1
        else:
            x = _pool_call(x, mod_l, g_pre, g_post, l, p["c_w_pool"], p["c_scale"], ic, trunk)
            ic += 1
        x = _ffn_call(x, mod_l, p["norm_ffn_pre"], p["norm_ffn_post"], p["f_w_gu"], p["f_w_down"], l, trunk)
    return x.reshape(b, t, d), new_k, new_v


def kernel(x_prompt, x_sample, cache_k, cache_v, c, c_ctx, w_mod, b_mod, norm_mix_pre, norm_mix_post, norm_ffn_pre, norm_ffn_post, a_w_in, a_norm_v, a_w_s, a_b_s, a_w_out, b_w_qkv, b_q_norm, b_k_norm, b_w_o, c_w_pool, c_scale, f_w_gu, f_w_down):
    batch, seq, d = x_prompt.shape
    dec_batch, dec_seq, _ = x_sample.shape
    depth = w_mod.shape[0]
    assert 1 + dec_batch <= MOD_ROWS

    cond = jnp.concatenate([c_ctx[None, :], c, jnp.zeros((MOD_ROWS - 1 - dec_batch, d), F32)], axis=0)
    mod = _mod_call(cond, w_mod, b_mod).reshape(depth, MOD_ROWS * N_MOD, 1, d)

    rows3 = lambda a: a.reshape(a.shape[0], 1, a.shape[-1])
    p = {
        "n_kv": cache_k.shape[3],
        "norm_mix_pre": rows3(norm_mix_pre), "norm_mix_post": rows3(norm_mix_post),
        "norm_ffn_pre": rows3(norm_ffn_pre), "norm_ffn_post": rows3(norm_ffn_post),
        "a_w_in": _col_blocks(a_w_in, COL_TILE), "a_norm_v": rows3(a_norm_v), "a_w_s": a_w_s.astype(BF16),
        "a_b_s": a_b_s[..., None], "a_w_out": a_w_out.astype(BF16),
        "b_w_qkv": b_w_qkv.astype(BF16), "b_q_norm": rows3(b_q_norm), "b_k_norm": rows3(b_k_norm),
        "b_w_o": b_w_o.astype(BF16),
        "c_w_pool": c_w_pool.astype(BF16), "c_scale": rows3(c_scale),
        "f_w_gu": f_w_gu.astype(BF16), "f_w_down": f_w_down.astype(BF16),
    }
    ctx = Trunk(batch=batch, seq=seq, mod_base=0, rows_per_cond=batch * seq, latent=False)
    lat = Trunk(batch=dec_batch, seq=dec_seq, mod_base=1, rows_per_cond=dec_seq, latent=True)
    y_prompt, new_k, new_v = _trunk(x_prompt, ctx, mod, None, p)
    y_sample, _, _ = _trunk(x_sample, lat, mod, (cache_k, cache_v), p)
    return (y_prompt, y_sample, jnp.stack(new_k, axis=1), jnp.stack(new_v, axis=1))
```

```python
import collections
import functools

import jax
import jax.numpy as jnp
from jax import lax
from jax.experimental import pallas as pl
from jax.experimental.pallas import tpu as pltpu

F32 = jnp.float32
BF16 = jnp.bfloat16

EPS = 1e-6
N_MOD = 6
GRID_W = 64
ROPE_THETA = 10000.0
POOL_WINDOWS = (2, 4, 8, 16)
POOL_HALO = 8
MOD_ROWS = 16
LOG2_E = 1.4426950408889634
V7X_LANES = 128

V7X_VMEM_BUDGET = 58 * 1024 * 1024
ROW_TILE = 512
ROW_GROUP = 16
ROW_GROUP_UNROLL = 8
FFN_HIDDEN_TILE = 512
COL_TILE = 512
ATTN_Q_TILE = 256
ATTN_MAX_KEYS = 4608
ATTN_HEAD_GROUPS = 2
MOD_COL_TILE = 1024

Trunk = collections.namedtuple("Trunk", "batch seq mod_base rows_per_cond latent")


def _nbytes(shape, dtype):
    n = 1
    for s in shape:
        n *= s
    return n * jnp.dtype(dtype).itemsize


def _vmem_limit(pipelined, scratch=(), temps=()):
    total = 2 * sum(_nbytes(s, d) for s, d in pipelined)
    total += sum(_nbytes(s, d) for s, d in scratch)
    total += sum(_nbytes(s, d) for s, d in temps)
    return int(min(max(total + (4 << 20), 16 << 20), V7X_VMEM_BUDGET))


def _params(semantics, limit):
    return pltpu.CompilerParams(dimension_semantics=semantics, vmem_limit_bytes=limit)


def _rms(xf, g):
    ms = jnp.mean(xf * xf, axis=-1, keepdims=True)
    return (xf * lax.rsqrt(ms + EPS)) * g


def _norm_mod(xf, g, scale, shift):
    return _rms(xf, g) * (1.0 + scale) + shift


def _row_groups(n_rows, body):
    assert n_rows % ROW_GROUP == 0

    def step(r, carry):
        body(pl.ds(pl.multiple_of(r * ROW_GROUP, ROW_GROUP), ROW_GROUP))
        return carry
    lax.fori_loop(0, n_rows // ROW_GROUP, step, 0, unroll=ROW_GROUP_UNROLL)


def _norm_mod_rows(x_ref, dst_ref, gpre_ref, scale_ref, shift_ref):
    d = x_ref.shape[1]
    gain = jnp.broadcast_to(gpre_ref[...] * (1.0 + scale_ref[0]), (ROW_GROUP, d))
    shift = jnp.broadcast_to(shift_ref[0], (ROW_GROUP, d))

    def body(rows):
        xr = x_ref[rows, :]
        ms = jnp.mean(xr * xr, axis=-1, keepdims=True)
        h = (xr * lax.rsqrt(ms + EPS)) * gain + shift
        dst_ref[rows, :] = h.astype(dst_ref.dtype)
    _row_groups(x_ref.shape[0], body)


def _gated_residual_rows(f_ref, x_ref, o_ref, gate_ref, gpost_ref):
    d = x_ref.shape[1]
    gain = jnp.broadcast_to(gate_ref[0] * gpost_ref[...], (ROW_GROUP, d))

    def body(rows):
        f = f_ref[rows, :]
        ms = jnp.mean(f * f, axis=-1, keepdims=True)
        o_ref[rows, :] = x_ref[rows, :] + (f * lax.rsqrt(ms + EPS)) * gain
    _row_groups(x_ref.shape[0], body)


def _col_blocks(w, tn):
    n, d, cols = w.shape
    tn = min(tn, cols)
    assert cols % tn == 0
    return w.astype(BF16).reshape(n, d, cols // tn, tn).transpose(0, 2, 1, 3)


def _row_tile(trunk):
    rows = trunk.batch * trunk.seq
    tm = min(ROW_TILE, rows)
    assert rows % tm == 0 and (trunk.rows_per_cond % tm == 0)
    return tm


def _mod_spec(trunk, tm, m, ahead=0):
    last_tile = trunk.batch * trunk.seq // tm - 1

    def index(i, *_):
        tile = jnp.minimum(i + ahead, last_tile)
        return ((trunk.mod_base + (tile * tm) // trunk.rows_per_cond) * N_MOD + m, 0, 0)
    return index


def _mod_kernel(c_ref, w_ref, b_ref, o_ref):
    c = c_ref[...]
    s = (c * jax.nn.sigmoid(c)).astype(BF16)
    o_ref[0] = jnp.dot(s, w_ref[0].astype(BF16), preferred_element_type=F32) + b_ref[0]


def _mod_call(cond, w_mod, b_mod):
    depth, d, n = w_mod.shape
    tn = min(MOD_COL_TILE, n)
    assert n % tn == 0
    blocks = [((MOD_ROWS, d), F32), ((1, d, tn), F32), ((1, 1, tn), F32), ((1, MOD_ROWS, tn), F32)]
    return pl.pallas_call(
        _mod_kernel,
        grid=(depth, n // tn),
        in_specs=[
            pl.BlockSpec((MOD_ROWS, d), lambda l, j: (0, 0)),
            pl.BlockSpec((1, d, tn), lambda l, j: (l, 0, j)),
            pl.BlockSpec((1, 1, tn), lambda l, j: (l, 0, j)),
        ],
        out_specs=pl.BlockSpec((1, MOD_ROWS, tn), lambda l, j: (l, 0, j)),
        out_shape=jax.ShapeDtypeStruct((depth, MOD_ROWS, n), F32),
        compiler_params=_params(("parallel", "parallel"), _vmem_limit(blocks, temps=[((d, tn), BF16)])),
        name="mod",
    )(cond, w_mod, b_mod.reshape(depth, 1, n))


def _ffn_kernel(x_ref, shift_ref, scale_ref, gate_ref, gpre_ref, gpost_ref, wg_ref, wu_ref, wd_ref,
                o_ref, h_sc, acc_sc, *, n_chunks):
    c = pl.program_id(1)

    @pl.when(c == 0)
    def _():
        _norm_mod_rows(x_ref, h_sc, gpre_ref, scale_ref, shift_ref)
        acc_sc[...] = jnp.zeros_like(acc_sc)

    h = h_sc[...]
    g = jnp.dot(h, wg_ref[...], preferred_element_type=F32)
    u = jnp.dot(h, wu_ref[...], preferred_element_type=F32)
    a = (g * jax.nn.sigmoid(g)) * u
    acc_sc[...] += jnp.dot(a.astype(BF16), wd_ref[...], preferred_element_type=F32)

    @pl.when(c == n_chunks - 1)
    def _():
        _gated_residual_rows(acc_sc, x_ref, o_ref, gate_ref, gpost_ref)


def _ffn_call(x, mod_l, g_pre, g_post, w_gu, w_down, layer, trunk):
    rows, d = x.shape
    hidden = w_down.shape[1]
    tm = _row_tile(trunk)
    th = min(FFN_HIDDEN_TILE, hidden)
    assert hidden % th == 0
    nc = hidden // th
    blocks = [((tm, d), F32)] * 2 + [((d, th), BF16), ((d, th), BF16), ((th, d), BF16)]
    scratch = [((tm, d), BF16), ((tm, d), F32)]
    temps = [((tm, th), F32)] * 4 + [((tm, d), F32)]
    return pl.pallas_call(
        functools.partial(_ffn_kernel, n_chunks=nc),
        grid=(rows // tm, nc),
        in_specs=[
            pl.BlockSpec((tm, d), lambda i, c: (i, 0)),
            pl.BlockSpec((1, 1, d), _mod_spec(trunk, tm, 3)),
            pl.BlockSpec((1, 1, d), _mod_spec(trunk, tm, 4)),
            pl.BlockSpec((1, 1, d), _mod_spec(trunk, tm, 5)),
            pl.BlockSpec((None, 1, d), lambda i, c: (layer, 0, 0)),
            pl.BlockSpec((None, 1, d), lambda i, c: (layer, 0, 0)),
            pl.BlockSpec((None, d, th), lambda i, c: (layer, 0, c)),
            pl.BlockSpec((None, d, th), lambda i, c: (layer, 0, nc + c)),
            pl.BlockSpec((None, th, d), lambda i, c: (layer, c, 0)),
        ],
        out_specs=pl.BlockSpec((tm, d), lambda i, c: (i, 0)),
        out_shape=jax.ShapeDtypeStruct((rows, d), F32),
        scratch_shapes=[pltpu.VMEM(s, t) for s, t in scratch],
        compiler_params=_params(("parallel", "arbitrary"), _vmem_limit(blocks, scratch, temps)),
        name="ffn",
    )(x, mod_l, mod_l, mod_l, g_pre, g_post, w_gu, w_gu, w_down)


def _proj_kernel(y_ref, w_ref, x_ref, gate_ref, gpost_ref, o_ref):
    m = jnp.dot(y_ref[...], w_ref[...], preferred_element_type=F32)
    o_ref[...] = x_ref[...] + gate_ref[0] * _rms(m, gpost_ref[...])


def _proj_call(y, w, w_layer, x, mod_l, g_post, layer, trunk):
    rows, d = x.shape
    k = y.shape[1]
    tm = _row_tile(trunk)
    blocks = [((tm, k), BF16), ((k, d), BF16), ((tm, d), F32), ((tm, d), F32)]
    return pl.pallas_call(
        _proj_kernel,
        grid=(rows // tm,),
        in_specs=[
            pl.BlockSpec((tm, k), lambda i: (i, 0)),
            pl.BlockSpec((None, k, d), lambda i: (w_layer, 0, 0)),
            pl.BlockSpec((tm, d), lambda i: (i, 0)),
            pl.BlockSpec((1, 1, d), _mod_spec(trunk, tm, 2)),
            pl.BlockSpec((None, 1, d), lambda i: (layer, 0, 0)),
        ],
        out_specs=pl.BlockSpec((tm, d), lambda i: (i, 0)),
        out_shape=jax.ShapeDtypeStruct((rows, d), F32),
        compiler_params=_params(("parallel",), _vmem_limit(blocks, temps=[((tm, d), F32)] * 2)),
        name="proj",
    )(y, w, x, mod_l, g_post)


def _mixa_kernel(x_ref, shift_ref, scale_ref, gpre_ref, win_ref, gv_ref, ws_ref, bs_ref,
                 y_ref, h_sc, a_sc, *, n_blocks, n_u_blocks, chunk, groups):
    j = pl.program_id(1)

    @pl.when(j == 0)
    def _():
        _norm_mod_rows(x_ref, h_sc, gpre_ref, scale_ref, shift_ref)

    a_sc[j] = jnp.dot(h_sc[...], win_ref[j], preferred_element_type=F32)

    @pl.when(j == n_blocks - 1)
    def _():
        tm, tn = a_sc.shape[1], a_sc.shape[2]
        width = n_u_blocks * tn
        gdim = width // groups
        ss = jnp.zeros((tm, 1), F32)
        for b in range(n_u_blocks, n_blocks):
            vb = a_sc[b]
            ss = ss + jnp.sum(vb * vb, axis=-1, keepdims=True)
        inv = lax.rsqrt(ss / width + EPS)
        for g in range(groups):
            col = g * gdim
            b, off = col // tn, col % tn
            gv = gv_ref[:, col:col + gdim]
            for c in range(tm // chunk):
                r = c * chunk
                v = (a_sc[n_u_blocks + b, r:r + chunk, off:off + gdim] * inv[r:r + chunk]) * gv
                s = jnp.dot(ws_ref[g], v.astype(BF16), preferred_element_type=F32) + bs_ref[g]
                u = a_sc[b, r:r + chunk, off:off + gdim]
                y_ref[r:r + chunk, col:col + gdim] = (u * s).astype(y_ref.dtype)


def _mixa_call(x, mod_l, g_pre, layer, w_in, g_v, w_s, b_s, a_layer, trunk):
    rows, d = x.shape
    nb, tn = w_in.shape[1], w_in.shape[3]
    width = nb * tn // 2
    groups, chunk = w_s.shape[1], w_s.shape[2]
    tm = _row_tile(trunk)
    gdim = width // groups
    assert width % tn == 0 and tn % gdim == 0 and tm % chunk == 0 and trunk.seq % chunk == 0
    blocks = [((tm, d), F32), ((tm, width), BF16), ((groups, chunk, chunk), BF16), ((groups, chunk, V7X_LANES), F32)]
    scratch = [((tm, d), BF16), ((nb, tm, tn), F32)]
    resident = [((nb, d, tn), BF16)]
    return pl.pallas_call(
        functools.partial(_mixa_kernel, n_blocks=nb, n_u_blocks=nb // 2, chunk=chunk, groups=groups),
        grid=(rows // tm, nb),
        in_specs=[
            pl.BlockSpec((tm, d), lambda i, j: (i, 0)),
            pl.BlockSpec((1, 1, d), _mod_spec(trunk, tm, 0)),
            pl.BlockSpec((1, 1, d), _mod_spec(trunk, tm, 1)),
            pl.BlockSpec((None, 1, d), lambda i, j: (layer, 0, 0)),
            pl.BlockSpec((None, nb, d, tn), lambda i, j: (a_layer, 0, 0, 0), pipeline_mode=pl.Buffered(1)),
            pl.BlockSpec((None, 1, width), lambda i, j: (a_layer, 0, 0)),
            pl.BlockSpec((None, groups, chunk, chunk), lambda i, j: (a_layer, 0, 0, 0)),
            pl.BlockSpec((None, groups, chunk, 1), lambda i, j: (a_layer, 0, 0, 0)),
        ],
        out_specs=pl.BlockSpec((tm, width), lambda i, j: (i, 0)),
        out_shape=jax.ShapeDtypeStruct((rows, width), BF16),
        scratch_shapes=[pltpu.VMEM(s, t) for s, t in scratch],
        compiler_params=_params(("parallel", "arbitrary"),
                                _vmem_limit(blocks, scratch + resident, temps=[((tm, d), F32), ((tm, tn), F32)])),
        name="mixa",
    )(x, mod_l, mod_l, g_pre, w_in, g_v, w_s, b_s)


def _rope_tables(seq, head_dim):
    axis_dim = head_dim // 2
    t = jnp.arange(seq)
    n_rows = seq // GRID_W
    row = jnp.minimum(t // GRID_W, n_rows - 1).astype(F32)
    col = (t % GRID_W).astype(F32)
    inv = jnp.power(ROPE_THETA, -jnp.arange(0, axis_dim, 2, dtype=F32) / axis_dim)
    ang = jnp.concatenate([row[:, None] * inv, col[:, None] * inv], axis=-1)
    cos, sin = jnp.cos(ang), jnp.sin(ang)
    zero = jnp.zeros_like(sin)
    cos2 = jnp.stack([cos, cos], axis=-1).reshape(seq, head_dim)
    sin_a = jnp.stack([-sin, zero], axis=-1).reshape(seq, head_dim)
    sin_b = jnp.stack([zero, sin], axis=-1).reshape(seq, head_dim)
    return cos2, sin_a, sin_b


def _qkv_kernel(*refs, n_q_blocks, head_dim, latent, q_scale):
    if latent:
        (x_ref, shift_ref, scale_ref, gpre_ref, w_ref, qg_ref, kg_ref, cos_ref, sa_ref, sb_ref,
         q_ref, k_ref, v_ref, h_sc) = refs
    else:
        (x_ref, shift_ref, scale_ref, gpre_ref, w_ref, qg_ref, kg_ref, q_ref, k_ref, v_ref, h_sc) = refs
    tn = k_ref.shape[1]
    n_heads = tn // head_dim
    _norm_mod_rows(x_ref, h_sc, gpre_ref, scale_ref, shift_ref)
    h = h_sc[...]

    def head(a, hh, gain):
        blk = _rms(a[:, hh * head_dim:(hh + 1) * head_dim], gain)
        if latent:
            nxt = pltpu.roll(blk, head_dim - 1, 1)
            prv = pltpu.roll(blk, 1, 1)
            blk = blk * cos_ref[...] + nxt * sa_ref[...] + prv * sb_ref[...]
        return blk

    for block in range(n_q_blocks + 2):
        a = jnp.dot(h, w_ref[:, block * tn:(block + 1) * tn], preferred_element_type=F32)
        if block < n_q_blocks:
            for hh in range(n_heads):
                col = block * tn + hh * head_dim
                q_ref[:, col:col + head_dim] = (head(a, hh, qg_ref[...]) * q_scale).astype(q_ref.dtype)
        elif block == n_q_blocks:
            for hh in range(n_heads):
                k_ref[:, hh * head_dim:(hh + 1) * head_dim] = head(a, hh, kg_ref[...]).astype(k_ref.dtype)
        else:
            v_ref[...] = a.astype(v_ref.dtype)


def _qkv_call(x, mod_l, g_pre, layer, w_qkv, q_gain, k_gain, b_layer, n_kv, trunk, kv_dtype):
    rows, d = x.shape
    hd = q_gain.shape[-1]
    tn = n_kv * hd
    cols = w_qkv.shape[2]
    nq = cols - 2 * tn
    assert nq % tn == 0
    nqb = nq // tn
    tm = _row_tile(trunk)
    in_specs = [
        pl.BlockSpec((tm, d), lambda i: (i, 0)),
        pl.BlockSpec((1, 1, d), _mod_spec(trunk, tm, 0)),
        pl.BlockSpec((1, 1, d), _mod_spec(trunk, tm, 1)),
        pl.BlockSpec((None, 1, d), lambda i: (layer, 0, 0)),
        pl.BlockSpec((None, d, cols), lambda i: (b_layer, 0, 0), pipeline_mode=pl.Buffered(1)),
        pl.BlockSpec((None, 1, hd), lambda i: (b_layer, 0, 0)),
        pl.BlockSpec((None, 1, hd), lambda i: (b_layer, 0, 0)),
    ]
    args = [x, mod_l, mod_l, g_pre, w_qkv, q_gain, k_gain]
    if trunk.latent:
        assert trunk.seq % tm == 0 and trunk.seq % GRID_W == 0
        tiles_per_seq = trunk.seq // tm
        in_specs += [pl.BlockSpec((tm, hd), lambda i: (i % tiles_per_seq, 0))] * 3
        args += list(_rope_tables(trunk.seq, hd))
    blocks = [((tm, d), F32), ((tm, nq), BF16), ((tm, tn), kv_dtype), ((tm, tn), kv_dtype),
              ((tm, hd), F32), ((tm, hd), F32), ((tm, hd), F32)]
    scratch = [((tm, d), BF16)]
    resident = [((d, cols), BF16)]
    return pl.pallas_call(
        functools.partial(_qkv_kernel, n_q_blocks=nqb, head_dim=hd, latent=trunk.latent,
                          q_scale=hd ** -0.5 * LOG2_E),
        grid=(rows // tm,),
        in_specs=in_specs,
        out_specs=[
            pl.BlockSpec((tm, nq), lambda i: (i, 0)),
            pl.BlockSpec((tm, tn), lambda i: (i, 0)),
            pl.BlockSpec((tm, tn), lambda i: (i, 0)),
        ],
        out_shape=[
            jax.ShapeDtypeStruct((rows, nq), BF16),
            jax.ShapeDtypeStruct((rows, tn), kv_dtype),
            jax.ShapeDtypeStruct((rows, tn), kv_dtype),
        ],
        scratch_shapes=[pltpu.VMEM(s, t) for s, t in scratch],
        compiler_params=_params(("parallel",),
                                _vmem_limit(blocks, scratch + resident, temps=[((tm, tn), F32)] * (nqb + 2))),
        name="qkv",
    )(*args)


def _attn_kernel(q_ref, k_ref, v_ref, o_ref, *, q_per_kv, head_dim):
    hd = head_dim
    tq, tk = q_ref.shape[1], k_ref.shape[1]
    n_groups = ATTN_HEAD_GROUPS if q_per_kv % ATTN_HEAD_GROUPS == 0 else 1
    per = q_per_kv // n_groups
    k = k_ref[0].astype(BF16)
    v1 = jnp.concatenate([v_ref[0].astype(BF16), jnp.ones((tk, hd), BF16)], axis=1)
    qs = [jnp.concatenate([q_ref[0, :, g * hd:(g + 1) * hd] for g in range(i * per, (i + 1) * per)], axis=0)
          for i in range(n_groups)]
    ss = [lax.dot_general(q, k, (((1,), (1,)), ((), ())), preferred_element_type=F32) for q in qs]
    m_curs = [jnp.max(s, axis=-1, keepdims=True) for s in ss]

    def write(i, acc):
        out = acc[:, :hd] / acc[:, hd:]
        for j in range(per):
            g = i * per + j
            o_ref[0, :, g * hd:(g + 1) * hd] = out[j * tq:(j + 1) * tq].astype(o_ref.dtype)

    ps = [jnp.exp2(s - m).astype(BF16) for s, m in zip(ss, m_curs)]
    accs = [jnp.dot(p, v1, preferred_element_type=F32) for p in ps]
    for i, acc in enumerate(accs):
        write(i, acc)


def _attn_call(q, k, v, n_kv):
    b, t, nq = q.shape
    s = k.shape[1]
    hd = k.shape[2] // n_kv
    gw = nq // n_kv
    tq = min(ATTN_Q_TILE, t)
    assert t % tq == 0 and s <= ATTN_MAX_KEYS and s % V7X_LANES == 0 and hd % V7X_LANES == 0
    rows = (gw // hd) * tq
    blocks = [((1, tq, gw), BF16), ((1, s, hd), k.dtype), ((1, s, hd), v.dtype), ((1, tq, gw), BF16)]
    temps = [((rows, s), F32)] * 2 + [((rows, s), BF16), ((s, 2 * hd), BF16), ((rows, 2 * hd), F32)]
    return pl.pallas_call(
        functools.partial(_attn_kernel, q_per_kv=gw // hd, head_dim=hd),
        grid=(b, n_kv, t // tq),
        in_specs=[
            pl.BlockSpec((1, tq, gw), lambda bi, h, qi: (bi, qi, h)),
            pl.BlockSpec((1, s, hd), lambda bi, h, qi: (bi, 0, h)),
            pl.BlockSpec((1, s, hd), lambda bi, h, qi: (bi, 0, h)),
        ],
        out_specs=pl.BlockSpec((1, tq, gw), lambda bi, h, qi: (bi, qi, h)),
        out_shape=jax.ShapeDtypeStruct((b, t, nq), BF16),
        compiler_params=_params(("parallel", "parallel", "parallel"), _vmem_limit(blocks, temps=temps)),
        name="attn",
    )(q, k, v)


def _pool_kernel(x_ref, xp_ref, xn_ref, shift_ref, scale_ref, gate_ref, gpre_ref, gpost_ref, w_ref, cs_ref,
                 o_ref, h_sc, ta_sc, tb_sc, m_sc, *, tiles_per_seq):
    tm, d = x_ref.shape
    pad = POOL_HALO
    gdim = d // len(POOL_WINDOWS)
    gpre, scale, shift = gpre_ref[...], scale_ref[0], shift_ref[0]
    t_in_seq = lax.rem(pl.program_id(0), tiles_per_seq)
    keep_prev = (t_in_seq > 0).astype(F32)
    keep_next = (t_in_seq < tiles_per_seq - 1).astype(F32)
    zeros = jnp.zeros((pad, d), F32)
    h_sc[0:pad, :] = zeros
    h_sc[pad:2 * pad, :] = _norm_mod(xp_ref[...], gpre, scale, shift) * keep_prev
    h_sc[2 * pad:2 * pad + tm, :] = _norm_mod(x_ref[...], gpre, scale, shift)
    h_sc[2 * pad + tm:3 * pad + tm, :] = _norm_mod(xn_ref[...], gpre, scale, shift) * keep_next
    h_sc[3 * pad + tm:, :] = zeros
    for t_sc in (ta_sc, tb_sc):
        t_sc[0:pad, :] = zeros[:, :gdim]
        t_sc[3 * pad + tm:, :] = zeros[:, :gdim]

    ext = tm + 2 * pad

    def shifted_sum(src, cols, lo, hi):
        return src[pl.ds(pad + lo, ext), cols] + src[pl.ds(pad + hi, ext), cols]

    pos = t_in_seq * tm + lax.broadcasted_iota(jnp.int32, (tm, 1), 0)
    seq = tiles_per_seq * tm
    all_cols = pl.ds(0, gdim)
    for j, w in enumerate(POOL_WINDOWS):
        half = w // 2
        cols = pl.ds(j * gdim, gdim)
        src, src_cols, span = h_sc, cols, 1
        for dst in (ta_sc, tb_sc, ta_sc):
            if span * 2 >= w:
                break
            lo, hi = (-1, 0) if span == 1 else (-(span // 2), span // 2)
            dst[pl.ds(pad, ext), :] = shifted_sum(src, src_cols, lo, hi)
            src, src_cols, span = dst, all_cols, span * 2
        lo, hi = (-1, 0) if span == 1 else (-(span // 2), span // 2)
        win = src[pl.ds(2 * pad + lo, tm), src_cols] + src[pl.ds(2 * pad + hi, tm), src_cols]
        cnt = (jnp.minimum(pos + half, seq) - jnp.maximum(pos - half, 0)).astype(F32)
        p = win * (1.0 / cnt) - h_sc[pl.ds(2 * pad, tm), cols]
        y = jnp.dot(p.astype(BF16), w_ref[j], preferred_element_type=F32)
        m_sc[:, cols] = y * cs_ref[:, cols]
    o_ref[...] = x_ref[...] + gate_ref[0] * _rms(m_sc[...], gpost_ref[...])


def _pool_call(x, mod_l, g_pre, g_post, layer, w_pool, c_scale, c_layer, trunk):
    rows, d = x.shape
    tm = min(_row_tile(trunk), trunk.seq)
    assert trunk.seq % tm == 0 and tm % POOL_HALO == 0 and max(POOL_WINDOWS) // 2 <= POOL_HALO
    groups, gdim = w_pool.shape[1], w_pool.shape[2]
    assert groups == len(POOL_WINDOWS) and POOL_WINDOWS == (2, 4, 8, 16)
    halo_per_tile = tm // POOL_HALO
    last_halo = rows // POOL_HALO - 1
    blocks = [((tm, d), F32), ((tm, d), F32), ((groups, gdim, gdim), BF16)]
    ext_rows = tm + 4 * POOL_HALO
    scratch = [((ext_rows, d), F32), ((ext_rows, gdim), F32), ((ext_rows, gdim), F32), ((tm, d), F32)]
    return pl.pallas_call(
        functools.partial(_pool_kernel, tiles_per_seq=trunk.seq // tm),
        grid=(rows // tm,),
        in_specs=[
            pl.BlockSpec((tm, d), lambda i: (i, 0)),
            pl.BlockSpec((POOL_HALO, d), lambda i: (jnp.maximum(i * halo_per_tile - 1, 0), 0)),
            pl.BlockSpec((POOL_HALO, d), lambda i: (jnp.minimum((i + 1) * halo_per_tile, last_halo), 0)),
            pl.BlockSpec((1, 1, d), _mod_spec(trunk, tm, 0)),
            pl.BlockSpec((1, 1, d), _mod_spec(trunk, tm, 1)),
            pl.BlockSpec((1, 1, d), _mod_spec(trunk, tm, 2)),
            pl.BlockSpec((None, 1, d), lambda i: (layer, 0, 0)),
            pl.BlockSpec((None, 1, d), lambda i: (layer, 0, 0)),
            pl.BlockSpec((None, groups, gdim, gdim), lambda i: (c_layer, 0, 0, 0)),
            pl.BlockSpec((None, 1, d), lambda i: (c_layer, 0, 0)),
        ],
        out_specs=pl.BlockSpec((tm, d), lambda i: (i, 0)),
        out_shape=jax.ShapeDtypeStruct((rows, d), F32),
        scratch_shapes=[pltpu.VMEM(s, t) for s, t in scratch],
        compiler_params=_params(("parallel",), _vmem_limit(blocks, scratch, temps=[((tm, d), F32)] * 2)),
        name="pool",
    )(x, x, x, mod_l, mod_l, mod_l, g_pre, g_post, w_pool, c_scale)


def _trunk(x3, trunk, mod, cache_kv, p):
    b, t, d = x3.shape
    x = x3.reshape(b * t, d)
    depth = p["norm_mix_pre"].shape[0]
    n_kv = p["n_kv"]
    ia = ib = ic = 0
    new_k, new_v = [], []
    for l in range(depth):
        mod_l = mod[l]
        g_pre, g_post = p["norm_mix_pre"], p["norm_mix_post"]
        kind = l % 3
        if kind == 0:
            y = _mixa_call(x, mod_l, g_pre, l, p["a_w_in"], p["a_norm_v"], p["a_w_s"], p["a_b_s"], ia, trunk)
            x = _proj_call(y, p["a_w_out"], ia, x, mod_l, g_post, l, trunk)
            ia += 1
        elif kind == 1:
            kv_dtype = BF16 if trunk.latent else F32
            q, k, v = _qkv_call(x, mod_l, g_pre, l, p["b_w_qkv"], p["b_q_norm"], p["b_k_norm"], ib, n_kv, trunk,
                                kv_dtype)
            k3, v3 = k.reshape(b, t, -1), v.reshape(b, t, -1)
            if trunk.latent:
                ck, cv = cache_kv
                past = ck.shape[2]
                k3 = jnp.concatenate([k3, ck[:, ib].reshape(b, past, -1).astype(BF16)], axis=1)
                v3 = jnp.concatenate([v3, cv[:, ib].reshape(b, past, -1).astype(BF16)], axis=1)
            else:
                new_k.append(k3.reshape(b, t, n_kv, -1))
                new_v.append(v3.reshape(b, t, n_kv, -1))
            o = _attn_call(q.reshape(b, t, -1), k3, v3, n_kv)
            x = _proj_call(o.reshape(b * t, -1), p["b_w_o"], ib, x, mod_l, g_post, l, trunk)
            ib += 1
        else:
            x = _pool_call(x, mod_l, g_pre, g_post, l, p["c_w_pool"], p["c_scale"], ic, trunk)
            ic += 1
        x = _ffn_call(x, mod_l, p["norm_ffn_pre"], p["norm_ffn_post"], p["f_w_gu"], p["f_w_down"], l, trunk)
    return x.reshape(b, t, d), new_k, new_v


def kernel(x_prompt, x_sample, cache_k, cache_v, c, c_ctx, w_mod, b_mod, norm_mix_pre, norm_mix_post, norm_ffn_pre, norm_ffn_post, a_w_in, a_norm_v, a_w_s, a_b_s, a_w_out, b_w_qkv, b_q_norm, b_k_norm, b_w_o, c_w_pool, c_scale, f_w_gu, f_w_down):
    batch, seq, d = x_prompt.shape
    dec_batch, dec_seq, _ = x_sample.shape
    depth = w_mod.shape[0]
    assert 1 + dec_batch <= MOD_ROWS

    cond = jnp.concatenate([c_ctx[None, :], c, jnp.zeros((MOD_ROWS - 1 - dec_batch, d), F32)], axis=0)
    mod = _mod_call(cond, w_mod, b_mod).reshape(depth, MOD_ROWS * N_MOD, 1, d)

    rows3 = lambda a: a.reshape(a.shape[0], 1, a.shape[-1])
    p = {
        "n_kv": cache_k.shape[3],
        "norm_mix_pre": rows3(norm_mix_pre), "norm_mix_post": rows3(norm_mix_post),
        "norm_ffn_pre": rows3(norm_ffn_pre), "norm_ffn_post": rows3(norm_ffn_post),
        "a_w_in": _col_blocks(a_w_in, COL_TILE), "a_norm_v": rows3(a_norm_v), "a_w_s": a_w_s.astype(BF16),
        "a_b_s": a_b_s[..., None], "a_w_out": a_w_out.astype(BF16),
        "b_w_qkv": b_w_qkv.astype(BF16), "b_q_norm": rows3(b_q_norm), "b_k_norm": rows3(b_k_norm),
        "b_w_o": b_w_o.astype(BF16),
        "c_w_pool": c_w_pool.astype(BF16), "c_scale": rows3(c_scale),
        "f_w_gu": f_w_gu.astype(BF16), "f_w_down": f_w_down.astype(BF16),
    }
    ctx = Trunk(batch=batch, seq=seq, mod_base=0, rows_per_cond=batch * seq, latent=False)
    lat = Trunk(batch=dec_batch, seq=dec_seq, mod_base=1, rows_per_cond=dec_seq, latent=True)
    y_prompt, new_k, new_v = _trunk(x_prompt, ctx, mod, None, p)
    y_sample, _, _ = _trunk(x_sample, lat, mod, (cache_k, cache_v), p)
    return (y_prompt, y_sample, jnp.stack(new_k, axis=1), jnp.stack(new_v, axis=1))
```

```python
import collections
import functools

import jax
import jax.numpy as jnp
from jax import lax
from jax.experimental import pallas as pl
from jax.experimental.pallas import tpu as pltpu

F32 = jnp.float32
BF16 = jnp.bfloat16

EPS = 1e-6
N_MOD = 6
GRID_W = 64
ROPE_THETA = 10000.0
POOL_WINDOWS = (2, 4, 8, 16)
POOL_HALO = 8
MOD_ROWS = 16
LOG2_E = 1.4426950408889634
V7X_LANES = 128

V7X_VMEM_BUDGET = 58 * 1024 * 1024
ROW_TILE = 512
ROW_GROUP = 16
ROW_GROUP_UNROLL = 8
FFN_HIDDEN_TILE = 512
COL_TILE = 512
ATTN_Q_TILE = 256
ATTN_MAX_KEYS = 4608
ATTN_HEAD_GROUPS = 2
MOD_COL_TILE = 1024

Trunk = collections.namedtuple("Trunk", "batch seq mod_base rows_per_cond latent")


def _nbytes(shape, dtype):
    n = 1
    for s in shape:
        n *= s
    return n * jnp.dtype(dtype).itemsize


def _vmem_limit(pipelined, scratch=(), temps=()):
    total = 2 * sum(_nbytes(s, d) for s, d in pipelined)
    total += sum(_nbytes(s, d) for s, d in scratch)
    total += sum(_nbytes(s, d) for s, d in temps)
    return int(min(max(total + (4 << 20), 16 << 20), V7X_VMEM_BUDGET))


def _params(semantics, limit):
    return pltpu.CompilerParams(dimension_semantics=semantics, vmem_limit_bytes=limit)


def _rms(xf, g):
    ms = jnp.mean(xf * xf, axis=-1, keepdims=True)
    return (xf * lax.rsqrt(ms + EPS)) * g


def _norm_mod(xf, g, scale, shift):
    return _rms(xf, g) * (1.0 + scale) + shift


def _row_groups(n_rows, body):
    assert n_rows % ROW_GROUP == 0

    def step(r, carry):
        body(pl.ds(pl.multiple_of(r * ROW_GROUP, ROW_GROUP), ROW_GROUP))
        return carry
    lax.fori_loop(0, n_rows // ROW_GROUP, step, 0, unroll=ROW_GROUP_UNROLL)


def _norm_mod_rows(x_ref, dst_ref, gpre_ref, scale_ref, shift_ref):
    d = x_ref.shape[1]
    gain = jnp.broadcast_to(gpre_ref[...] * (1.0 + scale_ref[0]), (ROW_GROUP, d))
    shift = jnp.broadcast_to(shift_ref[0], (ROW_GROUP, d))

    def body(rows):
        xr = x_ref[rows, :]
        ms = jnp.mean(xr * xr, axis=-1, keepdims=True)
        h = (xr * lax.rsqrt(ms + EPS)) * gain + shift
        dst_ref[rows, :] = h.astype(dst_ref.dtype)
    _row_groups(x_ref.shape[0], body)


def _gated_residual_rows(f_ref, x_ref, o_ref, gate_ref, gpost_ref):
    d = x_ref.shape[1]
    gain = jnp.broadcast_to(gate_ref[0] * gpost_ref[...], (ROW_GROUP, d))

    def body(rows):
        f = f_ref[rows, :]
        ms = jnp.mean(f * f, axis=-1, keepdims=True)
        o_ref[rows, :] = x_ref[rows, :] + (f * lax.rsqrt(ms + EPS)) * gain
    _row_groups(x_ref.shape[0], body)


def _row_tile(trunk):
    rows = trunk.batch * trunk.seq
    tm = min(ROW_TILE, rows)
    assert rows % tm == 0 and (trunk.rows_per_cond % tm == 0)
    return tm


def _mod_spec(trunk, tm, m, ahead=0):
    last_tile = trunk.batch * trunk.seq // tm - 1

    def index(i, *_):
        tile = jnp.minimum(i + ahead, last_tile)
        return ((trunk.mod_base + (tile * tm) // trunk.rows_per_cond) * N_MOD + m, 0, 0)
    return index


def _mod_kernel(c_ref, w_ref, b_ref, o_ref):
    c = c_ref[...]
    s = (c * jax.nn.sigmoid(c)).astype(BF16)
    o_ref[0] = jnp.dot(s, w_ref[0].astype(BF16), preferred_element_type=F32) + b_ref[0]


def _mod_call(cond, w_mod, b_mod):
    depth, d, n = w_mod.shape
    tn = min(MOD_COL_TILE, n)
    assert n % tn == 0
    blocks = [((MOD_ROWS, d), F32), ((1, d, tn), F32), ((1, 1, tn), F32), ((1, MOD_ROWS, tn), F32)]
    return pl.pallas_call(
        _mod_kernel,
        grid=(depth, n // tn),
        in_specs=[
            pl.BlockSpec((MOD_ROWS, d), lambda l, j: (0, 0)),
            pl.BlockSpec((1, d, tn), lambda l, j: (l, 0, j)),
            pl.BlockSpec((1, 1, tn), lambda l, j: (l, 0, j)),
        ],
        out_specs=pl.BlockSpec((1, MOD_ROWS, tn), lambda l, j: (l, 0, j)),
        out_shape=jax.ShapeDtypeStruct((depth, MOD_ROWS, n), F32),
        compiler_params=_params(("parallel", "parallel"), _vmem_limit(blocks, temps=[((d, tn), BF16)])),
        name="mod",
    )(cond, w_mod, b_mod.reshape(depth, 1, n))


def _ffn_kernel(x_ref, shift_ref, scale_ref, gate_ref, gpre_ref, gpost_ref, wg_ref, wu_ref, wd_ref,
                o_ref, h_sc, acc_sc, *, n_chunks):
    c = pl.program_id(1)

    @pl.when(c == 0)
    def _():
        _norm_mod_rows(x_ref, h_sc, gpre_ref, scale_ref, shift_ref)
        acc_sc[...] = jnp.zeros_like(acc_sc)

    h = h_sc[...]
    g = jnp.dot(h, wg_ref[...], preferred_element_type=F32)
    u = jnp.dot(h, wu_ref[...], preferred_element_type=F32)
    a = (g * jax.nn.sigmoid(g)) * u
    acc_sc[...] += jnp.dot(a.astype(BF16), wd_ref[...], preferred_element_type=F32)

    @pl.when(c == n_chunks - 1)
    def _():
        _gated_residual_rows(acc_sc, x_ref, o_ref, gate_ref, gpost_ref)


def _ffn_call(x, mod_l, g_pre, g_post, w_gu, w_down, layer, trunk):
    rows, d = x.shape
    hidden = w_down.shape[1]
    tm = _row_tile(trunk)
    th = min(FFN_HIDDEN_TILE, hidden)
    assert hidden % th == 0
    nc = hidden // th
    blocks = [((tm, d), F32)] * 2 + [((d, th), BF16), ((d, th), BF16), ((th, d), BF16)]
    scratch = [((tm, d), BF16), ((tm, d), F32)]
    temps = [((tm, th), F32)] * 4 + [((tm, d), F32)]
    return pl.pallas_call(
        functools.partial(_ffn_kernel, n_chunks=nc),
        grid=(rows // tm, nc),
        in_specs=[
            pl.BlockSpec((tm, d), lambda i, c: (i, 0)),
            pl.BlockSpec((1, 1, d), _mod_spec(trunk, tm, 3)),
            pl.BlockSpec((1, 1, d), _mod_spec(trunk, tm, 4)),
            pl.BlockSpec((1, 1, d), _mod_spec(trunk, tm, 5)),
            pl.BlockSpec((None, 1, d), lambda i, c: (layer, 0, 0)),
            pl.BlockSpec((None, 1, d), lambda i, c: (layer, 0, 0)),
            pl.BlockSpec((None, d, th), lambda i, c: (layer, 0, c)),
            pl.BlockSpec((None, d, th), lambda i, c: (layer, 0, nc + c)),
            pl.BlockSpec((None, th, d), lambda i, c: (layer, c, 0)),
        ],
        out_specs=pl.BlockSpec((tm, d), lambda i, c: (i, 0)),
        out_shape=jax.ShapeDtypeStruct((rows, d), F32),
        scratch_shapes=[pltpu.VMEM(s, t) for s, t in scratch],
        compiler_params=_params(("parallel", "arbitrary"), _vmem_limit(blocks, scratch, temps)),
        name="ffn",
    )(x, mod_l, mod_l, mod_l, g_pre, g_post, w_gu, w_gu, w_down)


def _proj_kernel(y_ref, w_ref, x_ref, gate_ref, gpost_ref, o_ref):
    m = jnp.dot(y_ref[...], w_ref[...], preferred_element_type=F32)
    o_ref[...] = x_ref[...] + gate_ref[0] * _rms(m, gpost_ref[...])


def _proj_call(y, w, w_layer, x, mod_l, g_post, layer, trunk):
    rows, d = x.shape
    k = y.shape[1]
    tm = _row_tile(trunk)
    blocks = [((tm, k), BF16), ((k, d), BF16), ((tm, d), F32), ((tm, d), F32)]
    return pl.pallas_call(
        _proj_kernel,
        grid=(rows // tm,),
        in_specs=[
            pl.BlockSpec((tm, k), lambda i: (i, 0)),
            pl.BlockSpec((None, k, d), lambda i: (w_layer, 0, 0)),
            pl.BlockSpec((tm, d), lambda i: (i, 0)),
            pl.BlockSpec((1, 1, d), _mod_spec(trunk, tm, 2)),
            pl.BlockSpec((None, 1, d), lambda i: (layer, 0, 0)),
        ],
        out_specs=pl.BlockSpec((tm, d), lambda i: (i, 0)),
        out_shape=jax.ShapeDtypeStruct((rows, d), F32),
        compiler_params=_params(("parallel",), _vmem_limit(blocks, temps=[((tm, d), F32)] * 2)),
        name="proj",
    )(y, w, x, mod_l, g_post)


def _mixa_kernel(x_ref, shift_ref, scale_ref, gpre_ref, win_ref, gv_ref, ws_ref, bs_ref, y_ref, h_sc,
                 *, col_tile, chunk, groups):
    tm = x_ref.shape[0]
    width = win_ref.shape[1] // 2
    tn = col_tile
    n_u = width // tn
    gdim = width // groups
    _norm_mod_rows(x_ref, h_sc, gpre_ref, scale_ref, shift_ref)
    h = h_sc[...]
    vs = [jnp.dot(h, win_ref[:, width + b * tn:width + (b + 1) * tn], preferred_element_type=F32)
          for b in range(n_u)]
    ss = jnp.zeros((tm, 1), F32)
    for vb in vs:
        ss = ss + jnp.sum(vb * vb, axis=-1, keepdims=True)
    inv = lax.rsqrt(ss / width + EPS)
    for b in range(n_u):
        u = jnp.dot(h, win_ref[:, b * tn:(b + 1) * tn], preferred_element_type=F32)
        for off in range(0, tn, gdim):
            col = b * tn + off
            g = col // gdim
            gv = gv_ref[:, col:col + gdim]
            for c in range(tm // chunk):
                r = c * chunk
                v = (vs[b][r:r + chunk, off:off + gdim] * inv[r:r + chunk]) * gv
                s = jnp.dot(ws_ref[g], v.astype(BF16), preferred_element_type=F32) + bs_ref[g]
                y_ref[r:r + chunk, col:col + gdim] = (u[r:r + chunk, off:off + gdim] * s).astype(y_ref.dtype)


def _mixa_call(x, mod_l, g_pre, layer, w_in, g_v, w_s, b_s, a_layer, trunk):
    rows, d = x.shape
    width = w_in.shape[2] // 2
    groups, chunk = w_s.shape[1], w_s.shape[2]
    tm = _row_tile(trunk)
    tn = min(COL_TILE, width)
    gdim = width // groups
    assert width % tn == 0 and tn % gdim == 0 and tm % chunk == 0 and trunk.seq % chunk == 0
    blocks = [((tm, d), F32), ((tm, width), BF16), ((groups, chunk, chunk), BF16), ((groups, chunk, V7X_LANES), F32)]
    scratch = [((tm, d), BF16)]
    resident = [((d, 2 * width), BF16)]
    return pl.pallas_call(
        functools.partial(_mixa_kernel, col_tile=tn, chunk=chunk, groups=groups),
        grid=(rows // tm,),
        in_specs=[
            pl.BlockSpec((tm, d), lambda i: (i, 0)),
            pl.BlockSpec((1, 1, d), _mod_spec(trunk, tm, 0)),
            pl.BlockSpec((1, 1, d), _mod_spec(trunk, tm, 1)),
            pl.BlockSpec((None, 1, d), lambda i: (layer, 0, 0)),
            pl.BlockSpec((None, d, 2 * width), lambda i: (a_layer, 0, 0), pipeline_mode=pl.Buffered(1)),
            pl.BlockSpec((None, 1, width), lambda i: (a_layer, 0, 0)),
            pl.BlockSpec((None, groups, chunk, chunk), lambda i: (a_layer, 0, 0, 0)),
            pl.BlockSpec((None, groups, chunk, 1), lambda i: (a_layer, 0, 0, 0)),
        ],
        out_specs=pl.BlockSpec((tm, width), lambda i: (i, 0)),
        out_shape=jax.ShapeDtypeStruct((rows, width), BF16),
        scratch_shapes=[pltpu.VMEM(s, t) for s, t in scratch],
        compiler_params=_params(("parallel",),
                                _vmem_limit(blocks, scratch + resident, temps=[((tm, 2 * width), F32)])),
        name="mixa",
    )(x, mod_l, mod_l, g_pre, w_in, g_v, w_s, b_s)


def _rope_tables(seq, head_dim):
    axis_dim = head_dim // 2
    t = jnp.arange(seq)
    n_rows = seq // GRID_W
    row = jnp.minimum(t // GRID_W, n_rows - 1).astype(F32)
    col = (t % GRID_W).astype(F32)
    inv = jnp.power(ROPE_THETA, -jnp.arange(0, axis_dim, 2, dtype=F32) / axis_dim)
    ang = jnp.concatenate([row[:, None] * inv, col[:, None] * inv], axis=-1)
    cos, sin = jnp.cos(ang), jnp.sin(ang)
    zero = jnp.zeros_like(sin)
    cos2 = jnp.stack([cos, cos], axis=-1).reshape(seq, head_dim)
    sin_a = jnp.stack([-sin, zero], axis=-1).reshape(seq, head_dim)
    sin_b = jnp.stack([zero, sin], axis=-1).reshape(seq, head_dim)
    return cos2, sin_a, sin_b


def _qkv_kernel(*refs, n_q_blocks, head_dim, latent, q_scale):
    if latent:
        (x_ref, shift_ref, scale_ref, gpre_ref, w_ref, qg_ref, kg_ref, cos_ref, sa_ref, sb_ref,
         q_ref, k_ref, v_ref, h_sc) = refs
    else:
        (x_ref, shift_ref, scale_ref, gpre_ref, w_ref, qg_ref, kg_ref, q_ref, k_ref, v_ref, h_sc) = refs
    tn = k_ref.shape[1]
    n_heads = tn // head_dim
    _norm_mod_rows(x_ref, h_sc, gpre_ref, scale_ref, shift_ref)
    h = h_sc[...]

    def head(a, hh, gain):
        blk = _rms(a[:, hh * head_dim:(hh + 1) * head_dim], gain)
        if latent:
            nxt = pltpu.roll(blk, head_dim - 1, 1)
            prv = pltpu.roll(blk, 1, 1)
            blk = blk * cos_ref[...] + nxt * sa_ref[...] + prv * sb_ref[...]
        return blk

    for block in range(n_q_blocks + 2):
        a = jnp.dot(h, w_ref[:, block * tn:(block + 1) * tn], preferred_element_type=F32)
        if block < n_q_blocks:
            for hh in range(n_heads):
                col = block * tn + hh * head_dim
                q_ref[:, col:col + head_dim] = (head(a, hh, qg_ref[...]) * q_scale).astype(q_ref.dtype)
        elif block == n_q_blocks:
            for hh in range(n_heads):
                k_ref[:, hh * head_dim:(hh + 1) * head_dim] = head(a, hh, kg_ref[...]).astype(k_ref.dtype)
        else:
            v_ref[...] = a.astype(v_ref.dtype)


def _qkv_call(x, mod_l, g_pre, layer, w_qkv, q_gain, k_gain, b_layer, n_kv, trunk, kv_dtype):
    rows, d = x.shape
    hd = q_gain.shape[-1]
    tn = n_kv * hd
    cols = w_qkv.shape[2]
    nq = cols - 2 * tn
    assert nq % tn == 0
    nqb = nq // tn
    tm = _row_tile(trunk)
    in_specs = [
        pl.BlockSpec((tm, d), lambda i: (i, 0)),
        pl.BlockSpec((1, 1, d), _mod_spec(trunk, tm, 0)),
        pl.BlockSpec((1, 1, d), _mod_spec(trunk, tm, 1)),
        pl.BlockSpec((None, 1, d), lambda i: (layer, 0, 0)),
        pl.BlockSpec((None, d, cols), lambda i: (b_layer, 0, 0), pipeline_mode=pl.Buffered(1)),
        pl.BlockSpec((None, 1, hd), lambda i: (b_layer, 0, 0)),
        pl.BlockSpec((None, 1, hd), lambda i: (b_layer, 0, 0)),
    ]
    args = [x, mod_l, mod_l, g_pre, w_qkv, q_gain, k_gain]
    if trunk.latent:
        assert trunk.seq % tm == 0 and trunk.seq % GRID_W == 0
        tiles_per_seq = trunk.seq // tm
        in_specs += [pl.BlockSpec((tm, hd), lambda i: (i % tiles_per_seq, 0))] * 3
        args += list(_rope_tables(trunk.seq, hd))
    blocks = [((tm, d), F32), ((tm, nq), BF16), ((tm, tn), kv_dtype), ((tm, tn), kv_dtype),
              ((tm, hd), F32), ((tm, hd), F32), ((tm, hd), F32)]
    scratch = [((tm, d), BF16)]
    resident = [((d, cols), BF16)]
    return pl.pallas_call(
        functools.partial(_qkv_kernel, n_q_blocks=nqb, head_dim=hd, latent=trunk.latent,
                          q_scale=hd ** -0.5 * LOG2_E),
        grid=(rows // tm,),
        in_specs=in_specs,
        out_specs=[
            pl.BlockSpec((tm, nq), lambda i: (i, 0)),
            pl.BlockSpec((tm, tn), lambda i: (i, 0)),
            pl.BlockSpec((tm, tn), lambda i: (i, 0)),
        ],
        out_shape=[
            jax.ShapeDtypeStruct((rows, nq), BF16),
            jax.ShapeDtypeStruct((rows, tn), kv_dtype),
            jax.ShapeDtypeStruct((rows, tn), kv_dtype),
        ],
        scratch_shapes=[pltpu.VMEM(s, t) for s, t in scratch],
        compiler_params=_params(("parallel",),
                                _vmem_limit(blocks, scratch + resident, temps=[((tm, tn), F32)] * (nqb + 2))),
        name="qkv",
    )(*args)


def _attn_kernel(q_ref, k_ref, v_ref, o_ref, *, q_per_kv, head_dim):
    hd = head_dim
    tq, tk = q_ref.shape[1], k_ref.shape[1]
    n_groups = ATTN_HEAD_GROUPS if q_per_kv % ATTN_HEAD_GROUPS == 0 else 1
    per = q_per_kv // n_groups
    k = k_ref[0].astype(BF16)
    v1 = jnp.concatenate([v_ref[0].astype(BF16), jnp.ones((tk, hd), BF16)], axis=1)
    qs = [jnp.concatenate([q_ref[0, :, g * hd:(g + 1) * hd] for g in range(i * per, (i + 1) * per)], axis=0)
          for i in range(n_groups)]
    ss = [lax.dot_general(q, k, (((1,), (1,)), ((), ())), preferred_element_type=F32) for q in qs]
    m_curs = [jnp.max(s, axis=-1, keepdims=True) for s in ss]

    def write(i, acc):
        out = acc[:, :hd] / acc[:, hd:]
        for j in range(per):
            g = i * per + j
            o_ref[0, :, g * hd:(g + 1) * hd] = out[j * tq:(j + 1) * tq].astype(o_ref.dtype)

    ps = [jnp.exp2(s - m).astype(BF16) for s, m in zip(ss, m_curs)]
    accs = [jnp.dot(p, v1, preferred_element_type=F32) for p in ps]
    for i, acc in enumerate(accs):
        write(i, acc)


def _attn_call(q, k, v, n_kv):
    b, t, nq = q.shape
    s = k.shape[1]
    hd = k.shape[2] // n_kv
    gw = nq // n_kv
    tq = min(ATTN_Q_TILE, t)
    assert t % tq == 0 and s <= ATTN_MAX_KEYS and s % V7X_LANES == 0 and hd % V7X_LANES == 0
    rows = (gw // hd) * tq
    blocks = [((1, tq, gw), BF16), ((1, s, hd), k.dtype), ((1, s, hd), v.dtype), ((1, tq, gw), BF16)]
    temps = [((rows, s), F32)] * 2 + [((rows, s), BF16), ((s, 2 * hd), BF16), ((rows, 2 * hd), F32)]
    return pl.pallas_call(
        functools.partial(_attn_kernel, q_per_kv=gw // hd, head_dim=hd),
        grid=(b, n_kv, t // tq),
        in_specs=[
            pl.BlockSpec((1, tq, gw), lambda bi, h, qi: (bi, qi, h)),
            pl.BlockSpec((1, s, hd), lambda bi, h, qi: (bi, 0, h)),
            pl.BlockSpec((1, s, hd), lambda bi, h, qi: (bi, 0, h)),
        ],
        out_specs=pl.BlockSpec((1, tq, gw), lambda bi, h, qi: (bi, qi, h)),
        out_shape=jax.ShapeDtypeStruct((b, t, nq), BF16),
        compiler_params=_params(("parallel", "parallel", "parallel"), _vmem_limit(blocks, temps=temps)),
        name="attn",
    )(q, k, v)


def _pool_kernel(x_ref, xp_ref, xn_ref, shift_ref, scale_ref, gate_ref, gpre_ref, gpost_ref, w_ref, cs_ref,
                 o_ref, h_sc, ta_sc, tb_sc, m_sc, *, tiles_per_seq):
    tm, d = x_ref.shape
    pad = POOL_HALO
    gdim = d // len(POOL_WINDOWS)
    gpre, scale, shift = gpre_ref[...], scale_ref[0], shift_ref[0]
    t_in_seq = lax.rem(pl.program_id(0), tiles_per_seq)
    keep_prev = (t_in_seq > 0).astype(F32)
    keep_next = (t_in_seq < tiles_per_seq - 1).astype(F32)
    zeros = jnp.zeros((pad, d), F32)
    h_sc[0:pad, :] = zeros
    h_sc[pad:2 * pad, :] = _norm_mod(xp_ref[...], gpre, scale, shift) * keep_prev
    h_sc[2 * pad:2 * pad + tm, :] = _norm_mod(x_ref[...], gpre, scale, shift)
    h_sc[2 * pad + tm:3 * pad + tm, :] = _norm_mod(xn_ref[...], gpre, scale, shift) * keep_next
    h_sc[3 * pad + tm:, :] = zeros
    for t_sc in (ta_sc, tb_sc):
        t_sc[0:pad, :] = zeros[:, :gdim]
        t_sc[3 * pad + tm:, :] = zeros[:, :gdim]

    ext = tm + 2 * pad

    def shifted_sum(src, cols, lo, hi):
        return src[pl.ds(pad + lo, ext), cols] + src[pl.ds(pad + hi, ext), cols]

    pos = t_in_seq * tm + lax.broadcasted_iota(jnp.int32, (tm, 1), 0)
    seq = tiles_per_seq * tm
    all_cols = pl.ds(0, gdim)
    for j, w in enumerate(POOL_WINDOWS):
        half = w // 2
        cols = pl.ds(j * gdim, gdim)
        src, src_cols, span = h_sc, cols, 1
        for dst in (ta_sc, tb_sc, ta_sc):
            if span * 2 >= w:
                break
            lo, hi = (-1, 0) if span == 1 else (-(span // 2), span // 2)
            dst[pl.ds(pad, ext), :] = shifted_sum(src, src_cols, lo, hi)
            src, src_cols, span = dst, all_cols, span * 2
        lo, hi = (-1, 0) if span == 1 else (-(span // 2), span // 2)
        win = src[pl.ds(2 * pad + lo, tm), src_cols] + src[pl.ds(2 * pad + hi, tm), src_cols]
        cnt = (jnp.minimum(pos + half, seq) - jnp.maximum(pos - half, 0)).astype(F32)
        p = win * (1.0 / cnt) - h_sc[pl.ds(2 * pad, tm), cols]
        y = jnp.dot(p.astype(BF16), w_ref[j], preferred_element_type=F32)
        m_sc[:, cols] = y * cs_ref[:, cols]
    o_ref[...] = x_ref[...] + gate_ref[0] * _rms(m_sc[...], gpost_ref[...])


def _pool_call(x, mod_l, g_pre, g_post, layer, w_pool, c_scale, c_layer, trunk):
    rows, d = x.shape
    tm = min(_row_tile(trunk), trunk.seq)
    assert trunk.seq % tm == 0 and tm % POOL_HALO == 0 and max(POOL_WINDOWS) // 2 <= POOL_HALO
    groups, gdim = w_pool.shape[1], w_pool.shape[2]
    assert groups == len(POOL_WINDOWS) and POOL_WINDOWS == (2, 4, 8, 16)
    halo_per_tile = tm // POOL_HALO
    last_halo = rows // POOL_HALO - 1
    blocks = [((tm, d), F32), ((tm, d), F32), ((groups, gdim, gdim), BF16)]
    ext_rows = tm + 4 * POOL_HALO
    scratch = [((ext_rows, d), F32), ((ext_rows, gdim), F32), ((ext_rows, gdim), F32), ((tm, d), F32)]
    return pl.pallas_call(
        functools.partial(_pool_kernel, tiles_per_seq=trunk.seq // tm),
        grid=(rows // tm,),
        in_specs=[
            pl.BlockSpec((tm, d), lambda i: (i, 0)),
            pl.BlockSpec((POOL_HALO, d), lambda i: (jnp.maximum(i * halo_per_tile - 1, 0), 0)),
            pl.BlockSpec((POOL_HALO, d), lambda i: (jnp.minimum((i + 1) * halo_per_tile, last_halo), 0)),
            pl.BlockSpec((1, 1, d), _mod_spec(trunk, tm, 0)),
            pl.BlockSpec((1, 1, d), _mod_spec(trunk, tm, 1)),
            pl.BlockSpec((1, 1, d), _mod_spec(trunk, tm, 2)),
            pl.BlockSpec((None, 1, d), lambda i: (layer, 0, 0)),
            pl.BlockSpec((None, 1, d), lambda i: (layer, 0, 0)),
            pl.BlockSpec((None, groups, gdim, gdim), lambda i: (c_layer, 0, 0, 0)),
            pl.BlockSpec((None, 1, d), lambda i: (c_layer, 0, 0)),
        ],
        out_specs=pl.BlockSpec((tm, d), lambda i: (i, 0)),
        out_shape=jax.ShapeDtypeStruct((rows, d), F32),
        scratch_shapes=[pltpu.VMEM(s, t) for s, t in scratch],
        compiler_params=_params(("parallel",), _vmem_limit(blocks, scratch, temps=[((tm, d), F32)] * 2)),
        name="pool",
    )(x, x, x, mod_l, mod_l, mod_l, g_pre, g_post, w_pool, c_scale)


def _trunk(x3, trunk, mod, cache_kv, p):
    b, t, d = x3.shape
    x = x3.reshape(b * t, d)
    depth = p["norm_mix_pre"].shape[0]
    n_kv = p["n_kv"]
    ia = ib = ic = 0
    new_k, new_v = [], []
    for l in range(depth):
        mod_l = mod[l]
        g_pre, g_post = p["norm_mix_pre"], p["norm_mix_post"]
        kind = l % 3
        if kind == 0:
            y = _mixa_call(x, mod_l, g_pre, l, p["a_w_in"], p["a_norm_v"], p["a_w_s"], p["a_b_s"], ia, trunk)
            x = _proj_call(y, p["a_w_out"], ia, x, mod_l, g_post, l, trunk)
            ia += 1
        elif kind == 1:
            kv_dtype = BF16 if trunk.latent else F32
            q, k, v = _qkv_call(x, mod_l, g_pre, l, p["b_w_qkv"], p["b_q_norm"], p["b_k_norm"], ib, n_kv, trunk,
                                kv_dtype)
            k3, v3 = k.reshape(b, t, -1), v.reshape(b, t, -1)
            if trunk.latent:
                ck, cv = cache_kv
                past = ck.shape[2]
                k3 = jnp.concatenate([k3, ck[:, ib].reshape(b, past, -1).astype(BF16)], axis=1)
                v3 = jnp.concatenate([v3, cv[:, ib].reshape(b, past, -1).astype(BF16)], axis=1)
            else:
                new_k.append(k3.reshape(b, t, n_kv, -1))
                new_v.append(v3.reshape(b, t, n_kv, -1))
            o = _attn_call(q.reshape(b, t, -1), k3, v3, n_kv)
            x = _proj_call(o.reshape(b * t, -1), p["b_w_o"], ib, x, mod_l, g_post, l, trunk)
            ib += 1
        else:
            x = _pool_call(x, mod_l, g_pre, g_post, l, p["c_w_pool"], p["c_scale"], ic, trunk)
            ic += 1
        x = _ffn_call(x, mod_l, p["norm_ffn_pre"], p["norm_ffn_post"], p["f_w_gu"], p["f_w_down"], l, trunk)
    return x.reshape(b, t, d), new_k, new_v


def kernel(x_prompt, x_sample, cache_k, cache_v, c, c_ctx, w_mod, b_mod, norm_mix_pre, norm_mix_post, norm_ffn_pre, norm_ffn_post, a_w_in, a_norm_v, a_w_s, a_b_s, a_w_out, b_w_qkv, b_q_norm, b_k_norm, b_w_o, c_w_pool, c_scale, f_w_gu, f_w_down):
    batch, seq, d = x_prompt.shape
    dec_batch, dec_seq, _ = x_sample.shape
    depth = w_mod.shape[0]
    assert 1 + dec_batch <= MOD_ROWS

    cond = jnp.concatenate([c_ctx[None, :], c, jnp.zeros((MOD_ROWS - 1 - dec_batch, d), F32)], axis=0)
    mod = _mod_call(cond, w_mod, b_mod).reshape(depth, MOD_ROWS * N_MOD, 1, d)

    rows3 = lambda a: a.reshape(a.shape[0], 1, a.shape[-1])
    p = {
        "n_kv": cache_k.shape[3],
        "norm_mix_pre": rows3(norm_mix_pre), "norm_mix_post": rows3(norm_mix_post),
        "norm_ffn_pre": rows3(norm_ffn_pre), "norm_ffn_post": rows3(norm_ffn_post),
        "a_w_in": a_w_in.astype(BF16), "a_norm_v": rows3(a_norm_v), "a_w_s": a_w_s.astype(BF16),
        "a_b_s": a_b_s[..., None], "a_w_out": a_w_out.astype(BF16),
        "b_w_qkv": b_w_qkv.astype(BF16), "b_q_norm": rows3(b_q_norm), "b_k_norm": rows3(b_k_norm),
        "b_w_o": b_w_o.astype(BF16),
        "c_w_pool": c_w_pool.astype(BF16), "c_scale": rows3(c_scale),
        "f_w_gu": f_w_gu.astype(BF16), "f_w_down": f_w_down.astype(BF16),
    }
    ctx = Trunk(batch=batch, seq=seq, mod_base=0, rows_per_cond=batch * seq, latent=False)
    lat = Trunk(batch=dec_batch, seq=dec_seq, mod_base=1, rows_per_cond=dec_seq, latent=True)
    y_prompt, new_k, new_v = _trunk(x_prompt, ctx, mod, None, p)
    y_sample, _, _ = _trunk(x_sample, lat, mod, (cache_k, cache_v), p)
    return (y_prompt, y_sample, jnp.stack(new_k, axis=1), jnp.stack(new_v, axis=1))
```

```python
import collections
import functools

import jax
import jax.numpy as jnp
from jax import lax
from jax.experimental import pallas as pl
from jax.experimental.pallas import tpu as pltpu

F32 = jnp.float32
BF16 = jnp.bfloat16

EPS = 1e-6
N_MOD = 6
GRID_W = 64
ROPE_THETA = 10000.0
POOL_WINDOWS = (2, 4, 8, 16)
POOL_HALO = 8
MOD_ROWS = 16
LOG2_E = 1.4426950408889634
V7X_LANES = 128

V7X_VMEM_BUDGET = 58 * 1024 * 1024
ROW_TILE = 512
ROW_GROUP = 16
ROW_GROUP_UNROLL = 8
FFN_HIDDEN_TILE = 512
FFN_ROW_TILE = 1024
COL_TILE = 512
ATTN_Q_TILE = 256
ATTN_MAX_KEYS = 4608
ATTN_HEAD_GROUPS = 2
MOD_COL_TILE = 1024

Trunk = collections.namedtuple("Trunk", "batch seq mod_base rows_per_cond latent")


def _nbytes(shape, dtype):
    n = 1
    for s in shape:
        n *= s
    return n * jnp.dtype(dtype).itemsize


def _vmem_limit(pipelined, scratch=(), temps=()):
    total = 2 * sum(_nbytes(s, d) for s, d in pipelined)
    total += sum(_nbytes(s, d) for s, d in scratch)
    total += sum(_nbytes(s, d) for s, d in temps)
    return int(min(max(total + (4 << 20), 16 << 20), V7X_VMEM_BUDGET))


def _params(semantics, limit):
    return pltpu.CompilerParams(dimension_semantics=semantics, vmem_limit_bytes=limit)


def _rms(xf, g):
    ms = jnp.mean(xf * xf, axis=-1, keepdims=True)
    return (xf * lax.rsqrt(ms + EPS)) * g


def _norm_mod(xf, g, scale, shift):
    return _rms(xf, g) * (1.0 + scale) + shift


def _row_groups(n_rows, body, static_rows=False):
    assert n_rows % ROW_GROUP == 0
    if static_rows:
        for r in range(0, n_rows, ROW_GROUP):
            body(pl.ds(r, ROW_GROUP))
        return

    def step(r, carry):
        body(pl.ds(pl.multiple_of(r * ROW_GROUP, ROW_GROUP), ROW_GROUP))
        return carry
    lax.fori_loop(0, n_rows // ROW_GROUP, step, 0, unroll=ROW_GROUP_UNROLL)


def _norm_mod_rows(x_ref, dst_ref, gpre_ref, scale_ref, shift_ref):
    d = x_ref.shape[1]
    gain = jnp.broadcast_to(gpre_ref[...] * (1.0 + scale_ref[0]), (ROW_GROUP, d))
    shift = jnp.broadcast_to(shift_ref[0], (ROW_GROUP, d))

    def body(rows):
        xr = x_ref[rows, :]
        ms = jnp.mean(xr * xr, axis=-1, keepdims=True)
        h = (xr * lax.rsqrt(ms + EPS)) * gain + shift
        dst_ref[rows, :] = h.astype(dst_ref.dtype)
    _row_groups(x_ref.shape[0], body)


def _gated_residual_rows(f_ref, x_ref, o_ref, gate_ref, gpost_ref):
    d = x_ref.shape[1]
    gain = jnp.broadcast_to(gate_ref[0] * gpost_ref[...], (ROW_GROUP, d))

    def body(rows):
        f = f_ref[rows, :]
        ms = jnp.mean(f * f, axis=-1, keepdims=True)
        o_ref[rows, :] = x_ref[rows, :] + (f * lax.rsqrt(ms + EPS)) * gain
    _row_groups(x_ref.shape[0], body, static_rows=f_ref is o_ref)


def _row_tile(trunk):
    rows = trunk.batch * trunk.seq
    tm = min(ROW_TILE, rows)
    assert rows % tm == 0 and (trunk.rows_per_cond % tm == 0)
    return tm


def _mod_spec(trunk, tm, m, ahead=0):
    last_tile = trunk.batch * trunk.seq // tm - 1

    def index(i, *_):
        tile = jnp.minimum(i + ahead, last_tile)
        return ((trunk.mod_base + (tile * tm) // trunk.rows_per_cond) * N_MOD + m, 0, 0)
    return index


def _mod_kernel(c_ref, w_ref, b_ref, o_ref):
    c = c_ref[...]
    s = (c * jax.nn.sigmoid(c)).astype(BF16)
    o_ref[0] = jnp.dot(s, w_ref[0].astype(BF16), preferred_element_type=F32) + b_ref[0]


def _mod_call(cond, w_mod, b_mod):
    depth, d, n = w_mod.shape
    tn = min(MOD_COL_TILE, n)
    assert n % tn == 0
    blocks = [((MOD_ROWS, d), F32), ((1, d, tn), F32), ((1, 1, tn), F32), ((1, MOD_ROWS, tn), F32)]
    return pl.pallas_call(
        _mod_kernel,
        grid=(depth, n // tn),
        in_specs=[
            pl.BlockSpec((MOD_ROWS, d), lambda l, j: (0, 0)),
            pl.BlockSpec((1, d, tn), lambda l, j: (l, 0, j)),
            pl.BlockSpec((1, 1, tn), lambda l, j: (l, 0, j)),
        ],
        out_specs=pl.BlockSpec((1, MOD_ROWS, tn), lambda l, j: (l, 0, j)),
        out_shape=jax.ShapeDtypeStruct((depth, MOD_ROWS, n), F32),
        compiler_params=_params(("parallel", "parallel"), _vmem_limit(blocks, temps=[((d, tn), BF16)])),
        name="mod",
    )(cond, w_mod, b_mod.reshape(depth, 1, n))


def _ffn_kernel(x_ref, shift_ref, scale_ref, gate_ref, gpre_ref, gpost_ref, wg_ref, wu_ref, wd_ref,
                o_ref, h_sc, *, n_chunks, sub_rows):
    c = pl.program_id(1)
    tm = x_ref.shape[0]

    @pl.when(c == 0)
    def _():
        _norm_mod_rows(x_ref, h_sc, gpre_ref, scale_ref, shift_ref)
        o_ref[...] = jnp.zeros_like(o_ref)

    for r in range(0, tm, sub_rows):
        h = h_sc[r:r + sub_rows, :]
        g = jnp.dot(h, wg_ref[...], preferred_element_type=F32)
        u = jnp.dot(h, wu_ref[...], preferred_element_type=F32)
        a = (g * jax.nn.sigmoid(g)) * u
        o_ref[r:r + sub_rows, :] += jnp.dot(a.astype(BF16), wd_ref[...], preferred_element_type=F32)

    @pl.when(c == n_chunks - 1)
    def _():
        _gated_residual_rows(o_ref, x_ref, o_ref, gate_ref, gpost_ref)


def _ffn_call(x, mod_l, g_pre, g_post, w_gu, w_down, layer, trunk):
    rows, d = x.shape
    hidden = w_down.shape[1]
    tm = min(FFN_ROW_TILE, rows)
    sub = min(ROW_TILE, tm)
    assert rows % tm == 0 and trunk.rows_per_cond % tm == 0 and tm % sub == 0
    th = min(FFN_HIDDEN_TILE, hidden)
    assert hidden % th == 0
    nc = hidden // th
    blocks = [((tm, d), F32)] * 2 + [((d, th), BF16), ((d, th), BF16), ((th, d), BF16)]
    scratch = [((tm, d), BF16)]
    temps = [((sub, th), F32)] * 4 + [((sub, d), F32)]
    return pl.pallas_call(
        functools.partial(_ffn_kernel, n_chunks=nc, sub_rows=sub),
        grid=(rows // tm, nc),
        in_specs=[
            pl.BlockSpec((tm, d), lambda i, c: (i, 0)),
            pl.BlockSpec((1, 1, d), _mod_spec(trunk, tm, 3)),
            pl.BlockSpec((1, 1, d), _mod_spec(trunk, tm, 4)),
            pl.BlockSpec((1, 1, d), _mod_spec(trunk, tm, 5)),
            pl.BlockSpec((None, 1, d), lambda i, c: (layer, 0, 0)),
            pl.BlockSpec((None, 1, d), lambda i, c: (layer, 0, 0)),
            pl.BlockSpec((None, d, th), lambda i, c: (layer, 0, c)),
            pl.BlockSpec((None, d, th), lambda i, c: (layer, 0, nc + c)),
            pl.BlockSpec((None, th, d), lambda i, c: (layer, c, 0)),
        ],
        out_specs=pl.BlockSpec((tm, d), lambda i, c: (i, 0)),
        out_shape=jax.ShapeDtypeStruct((rows, d), F32),
        scratch_shapes=[pltpu.VMEM(s, t) for s, t in scratch],
        compiler_params=_params(("parallel", "arbitrary"), _vmem_limit(blocks, scratch, temps)),
        name="ffn",
    )(x, mod_l, mod_l, mod_l, g_pre, g_post, w_gu, w_gu, w_down)


def _proj_kernel(y_ref, w_ref, x_ref, gate_ref, gpost_ref, o_ref):
    m = jnp.dot(y_ref[...], w_ref[...], preferred_element_type=F32)
    o_ref[...] = x_ref[...] + gate_ref[0] * _rms(m, gpost_ref[...])


def _proj_call(y, w, w_layer, x, mod_l, g_post, layer, trunk):
    rows, d = x.shape
    k = y.shape[1]
    tm = _row_tile(trunk)
    blocks = [((tm, k), BF16), ((k, d), BF16), ((tm, d), F32), ((tm, d), F32)]
    return pl.pallas_call(
        _proj_kernel,
        grid=(rows // tm,),
        in_specs=[
            pl.BlockSpec((tm, k), lambda i: (i, 0)),
            pl.BlockSpec((None, k, d), lambda i: (w_layer, 0, 0)),
            pl.BlockSpec((tm, d), lambda i: (i, 0)),
            pl.BlockSpec((1, 1, d), _mod_spec(trunk, tm, 2)),
            pl.BlockSpec((None, 1, d), lambda i: (layer, 0, 0)),
        ],
        out_specs=pl.BlockSpec((tm, d), lambda i: (i, 0)),
        out_shape=jax.ShapeDtypeStruct((rows, d), F32),
        compiler_params=_params(("parallel",), _vmem_limit(blocks, temps=[((tm, d), F32)] * 2)),
        name="proj",
    )(y, w, x, mod_l, g_post)


def _mixa_kernel(x_ref, shift_ref, scale_ref, gpre_ref, win_ref, gv_ref, ws_ref, bs_ref, y_ref, h_sc,
                 *, col_tile, chunk, groups):
    tm = x_ref.shape[0]
    width = win_ref.shape[1] // 2
    tn = col_tile
    n_u = width // tn
    gdim = width // groups
    _norm_mod_rows(x_ref, h_sc, gpre_ref, scale_ref, shift_ref)
    h = h_sc[...]
    vs = [jnp.dot(h, win_ref[:, width + b * tn:width + (b + 1) * tn], preferred_element_type=F32)
          for b in range(n_u)]
    ss = jnp.zeros((tm, 1), F32)
    for vb in vs:
        ss = ss + jnp.sum(vb * vb, axis=-1, keepdims=True)
    inv = lax.rsqrt(ss / width + EPS)
    for b in range(n_u):
        u = jnp.dot(h, win_ref[:, b * tn:(b + 1) * tn], preferred_element_type=F32)
        for off in range(0, tn, gdim):
            col = b * tn + off
            g = col // gdim
            gv = gv_ref[:, col:col + gdim]
            for c in range(tm // chunk):
                r = c * chunk
                v = (vs[b][r:r + chunk, off:off + gdim] * inv[r:r + chunk]) * gv
                s = jnp.dot(ws_ref[g], v.astype(BF16), preferred_element_type=F32) + bs_ref[g]
                y_ref[r:r + chunk, col:col + gdim] = (u[r:r + chunk, off:off + gdim] * s).astype(y_ref.dtype)


def _mixa_call(x, mod_l, g_pre, layer, w_in, g_v, w_s, b_s, a_layer, trunk):
    rows, d = x.shape
    width = w_in.shape[2] // 2
    groups, chunk = w_s.shape[1], w_s.shape[2]
    tm = _row_tile(trunk)
    tn = min(COL_TILE, width)
    gdim = width // groups
    assert width % tn == 0 and tn % gdim == 0 and tm % chunk == 0 and trunk.seq % chunk == 0
    blocks = [((tm, d), F32), ((tm, width), BF16), ((groups, chunk, chunk), BF16), ((groups, chunk, V7X_LANES), F32)]
    scratch = [((tm, d), BF16)]
    resident = [((d, 2 * width), BF16)]
    return pl.pallas_call(
        functools.partial(_mixa_kernel, col_tile=tn, chunk=chunk, groups=groups),
        grid=(rows // tm,),
        in_specs=[
            pl.BlockSpec((tm, d), lambda i: (i, 0)),
            pl.BlockSpec((1, 1, d), _mod_spec(trunk, tm, 0)),
            pl.BlockSpec((1, 1, d), _mod_spec(trunk, tm, 1)),
            pl.BlockSpec((None, 1, d), lambda i: (layer, 0, 0)),
            pl.BlockSpec((None, d, 2 * width), lambda i: (a_layer, 0, 0), pipeline_mode=pl.Buffered(1)),
            pl.BlockSpec((None, 1, width), lambda i: (a_layer, 0, 0)),
            pl.BlockSpec((None, groups, chunk, chunk), lambda i: (a_layer, 0, 0, 0)),
            pl.BlockSpec((None, groups, chunk, 1), lambda i: (a_layer, 0, 0, 0)),
        ],
        out_specs=pl.BlockSpec((tm, width), lambda i: (i, 0)),
        out_shape=jax.ShapeDtypeStruct((rows, width), BF16),
        scratch_shapes=[pltpu.VMEM(s, t) for s, t in scratch],
        compiler_params=_params(("parallel",),
                                _vmem_limit(blocks, scratch + resident, temps=[((tm, 2 * width), F32)])),
        name="mixa",
    )(x, mod_l, mod_l, g_pre, w_in, g_v, w_s, b_s)


def _rope_tables(seq, head_dim):
    axis_dim = head_dim // 2
    t = jnp.arange(seq)
    n_rows = seq // GRID_W
    row = jnp.minimum(t // GRID_W, n_rows - 1).astype(F32)
    col = (t % GRID_W).astype(F32)
    inv = jnp.power(ROPE_THETA, -jnp.arange(0, axis_dim, 2, dtype=F32) / axis_dim)
    ang = jnp.concatenate([row[:, None] * inv, col[:, None] * inv], axis=-1)
    cos, sin = jnp.cos(ang), jnp.sin(ang)
    zero = jnp.zeros_like(sin)
    cos2 = jnp.stack([cos, cos], axis=-1).reshape(seq, head_dim)
    sin_a = jnp.stack([-sin, zero], axis=-1).reshape(seq, head_dim)
    sin_b = jnp.stack([zero, sin], axis=-1).reshape(seq, head_dim)
    return cos2, sin_a, sin_b


def _qkv_kernel(*refs, n_q_blocks, head_dim, latent, q_scale):
    if latent:
        (x_ref, shift_ref, scale_ref, gpre_ref, w_ref, qg_ref, kg_ref, cos_ref, sa_ref, sb_ref,
         q_ref, k_ref, v_ref, h_sc) = refs
    else:
        (x_ref, shift_ref, scale_ref, gpre_ref, w_ref, qg_ref, kg_ref, q_ref, k_ref, v_ref, h_sc) = refs
    tn = k_ref.shape[1]
    n_heads = tn // head_dim
    _norm_mod_rows(x_ref, h_sc, gpre_ref, scale_ref, shift_ref)
    h = h_sc[...]

    def head(a, hh, gain):
        blk = _rms(a[:, hh * head_dim:(hh + 1) * head_dim], gain)
        if latent:
            nxt = pltpu.roll(blk, head_dim - 1, 1)
            prv = pltpu.roll(blk, 1, 1)
            blk = blk * cos_ref[...] + nxt * sa_ref[...] + prv * sb_ref[...]
        return blk

    for block in range(n_q_blocks + 2):
        a = jnp.dot(h, w_ref[:, block * tn:(block + 1) * tn], preferred_element_type=F32)
        if block < n_q_blocks:
            for hh in range(n_heads):
                col = block * tn + hh * head_dim
                q_ref[:, col:col + head_dim] = (head(a, hh, qg_ref[...]) * q_scale).astype(q_ref.dtype)
        elif block == n_q_blocks:
            for hh in range(n_heads):
                k_ref[:, hh * head_dim:(hh + 1) * head_dim] = head(a, hh, kg_ref[...]).astype(k_ref.dtype)
        else:
            v_ref[...] = a.astype(v_ref.dtype)


def _qkv_call(x, mod_l, g_pre, layer, w_qkv, q_gain, k_gain, b_layer, n_kv, trunk, kv_dtype):
    rows, d = x.shape
    hd = q_gain.shape[-1]
    tn = n_kv * hd
    cols = w_qkv.shape[2]
    nq = cols - 2 * tn
    assert nq % tn == 0
    nqb = nq // tn
    tm = _row_tile(trunk)
    in_specs = [
        pl.BlockSpec((tm, d), lambda i: (i, 0)),
        pl.BlockSpec((1, 1, d), _mod_spec(trunk, tm, 0)),
        pl.BlockSpec((1, 1, d), _mod_spec(trunk, tm, 1)),
        pl.BlockSpec((None, 1, d), lambda i: (layer, 0, 0)),
        pl.BlockSpec((None, d, cols), lambda i: (b_layer, 0, 0), pipeline_mode=pl.Buffered(1)),
        pl.BlockSpec((None, 1, hd), lambda i: (b_layer, 0, 0)),
        pl.BlockSpec((None, 1, hd), lambda i: (b_layer, 0, 0)),
    ]
    args = [x, mod_l, mod_l, g_pre, w_qkv, q_gain, k_gain]
    if trunk.latent:
        assert trunk.seq % tm == 0 and trunk.seq % GRID_W == 0
        tiles_per_seq = trunk.seq // tm
        in_specs += [pl.BlockSpec((tm, hd), lambda i: (i % tiles_per_seq, 0))] * 3
        args += list(_rope_tables(trunk.seq, hd))
    blocks = [((tm, d), F32), ((tm, nq), BF16), ((tm, tn), kv_dtype), ((tm, tn), kv_dtype),
              ((tm, hd), F32), ((tm, hd), F32), ((tm, hd), F32)]
    scratch = [((tm, d), BF16)]
    resident = [((d, cols), BF16)]
    return pl.pallas_call(
        functools.partial(_qkv_kernel, n_q_blocks=nqb, head_dim=hd, latent=trunk.latent,
                          q_scale=hd ** -0.5 * LOG2_E),
        grid=(rows // tm,),
        in_specs=in_specs,
        out_specs=[
            pl.BlockSpec((tm, nq), lambda i: (i, 0)),
            pl.BlockSpec((tm, tn), lambda i: (i, 0)),
            pl.BlockSpec((tm, tn), lambda i: (i, 0)),
        ],
        out_shape=[
            jax.ShapeDtypeStruct((rows, nq), BF16),
            jax.ShapeDtypeStruct((rows, tn), kv_dtype),
            jax.ShapeDtypeStruct((rows, tn), kv_dtype),
        ],
        scratch_shapes=[pltpu.VMEM(s, t) for s, t in scratch],
        compiler_params=_params(("parallel",),
                                _vmem_limit(blocks, scratch + resident, temps=[((tm, tn), F32)] * (nqb + 2))),
        name="qkv",
    )(*args)


def _attn_kernel(q_ref, k_ref, v_ref, o_ref, *, q_per_kv, head_dim):
    hd = head_dim
    tq, tk = q_ref.shape[1], k_ref.shape[1]
    n_groups = ATTN_HEAD_GROUPS if q_per_kv % ATTN_HEAD_GROUPS == 0 else 1
    per = q_per_kv // n_groups
    k = k_ref[0].astype(BF16)
    v1 = jnp.concatenate([v_ref[0].astype(BF16), jnp.ones((tk, hd), BF16)], axis=1)
    qs = [jnp.concatenate([q_ref[0, :, g * hd:(g + 1) * hd] for g in range(i * per, (i + 1) * per)], axis=0)
          for i in range(n_groups)]
    ss = [lax.dot_general(q, k, (((1,), (1,)), ((), ())), preferred_element_type=F32) for q in qs]
    m_curs = [jnp.max(s, axis=-1, keepdims=True) for s in ss]

    def write(i, acc):
        out = acc[:, :hd] / acc[:, hd:]
        for j in range(per):
            g = i * per + j
            o_ref[0, :, g * hd:(g + 1) * hd] = out[j * tq:(j + 1) * tq].astype(o_ref.dtype)

    ps = [jnp.exp2(s - m).astype(BF16) for s, m in zip(ss, m_curs)]
    accs = [jnp.dot(p, v1, preferred_element_type=F32) for p in ps]
    for i, acc in enumerate(accs):
        write(i, acc)


def _attn_call(q, k, v, n_kv):
    b, t, nq = q.shape
    s = k.shape[1]
    hd = k.shape[2] // n_kv
    gw = nq // n_kv
    tq = min(ATTN_Q_TILE, t)
    assert t % tq == 0 and s <= ATTN_MAX_KEYS and s % V7X_LANES == 0 and hd % V7X_LANES == 0
    rows = (gw // hd) * tq
    blocks = [((1, tq, gw), BF16), ((1, s, hd), k.dtype), ((1, s, hd), v.dtype), ((1, tq, gw), BF16)]
    temps = [((rows, s), F32)] * 2 + [((rows, s), BF16), ((s, 2 * hd), BF16), ((rows, 2 * hd), F32)]
    return pl.pallas_call(
        functools.partial(_attn_kernel, q_per_kv=gw // hd, head_dim=hd),
        grid=(b, n_kv, t // tq),
        in_specs=[
            pl.BlockSpec((1, tq, gw), lambda bi, h, qi: (bi, qi, h)),
            pl.BlockSpec((1, s, hd), lambda bi, h, qi: (bi, 0, h)),
            pl.BlockSpec((1, s, hd), lambda bi, h, qi: (bi, 0, h)),
        ],
        out_specs=pl.BlockSpec((1, tq, gw), lambda bi, h, qi: (bi, qi, h)),
        out_shape=jax.ShapeDtypeStruct((b, t, nq), BF16),
        compiler_params=_params(("parallel", "parallel", "parallel"), _vmem_limit(blocks, temps=temps)),
        name="attn",
    )(q, k, v)


def _pool_kernel(x_ref, xp_ref, xn_ref, shift_ref, scale_ref, gate_ref, gpre_ref, gpost_ref, w_ref, cs_ref,
                 o_ref, h_sc, ta_sc, tb_sc, m_sc, *, tiles_per_seq):
    tm, d = x_ref.shape
    pad = POOL_HALO
    gdim = d // len(POOL_WINDOWS)
    gpre, scale, shift = gpre_ref[...], scale_ref[0], shift_ref[0]
    t_in_seq = lax.rem(pl.program_id(0), tiles_per_seq)
    keep_prev = (t_in_seq > 0).astype(F32)
    keep_next = (t_in_seq < tiles_per_seq - 1).astype(F32)
    zeros = jnp.zeros((pad, d), F32)
    h_sc[0:pad, :] = zeros
    h_sc[pad:2 * pad, :] = _norm_mod(xp_ref[...], gpre, scale, shift) * keep_prev
    h_sc[2 * pad:2 * pad + tm, :] = _norm_mod(x_ref[...], gpre, scale, shift)
    h_sc[2 * pad + tm:3 * pad + tm, :] = _norm_mod(xn_ref[...], gpre, scale, shift) * keep_next
    h_sc[3 * pad + tm:, :] = zeros
    for t_sc in (ta_sc, tb_sc):
        t_sc[0:pad, :] = zeros[:, :gdim]
        t_sc[3 * pad + tm:, :] = zeros[:, :gdim]

    ext = tm + 2 * pad

    def shifted_sum(src, cols, lo, hi):
        return src[pl.ds(pad + lo, ext), cols] + src[pl.ds(pad + hi, ext), cols]

    pos = t_in_seq * tm + lax.broadcasted_iota(jnp.int32, (tm, 1), 0)
    seq = tiles_per_seq * tm
    all_cols = pl.ds(0, gdim)
    for j, w in enumerate(POOL_WINDOWS):
        half = w // 2
        cols = pl.ds(j * gdim, gdim)
        src, src_cols, span = h_sc, cols, 1
        for dst in (ta_sc, tb_sc, ta_sc):
            if span * 2 >= w:
                break
            lo, hi = (-1, 0) if span == 1 else (-(span // 2), span // 2)
            dst[pl.ds(pad, ext), :] = shifted_sum(src, src_cols, lo, hi)
            src, src_cols, span = dst, all_cols, span * 2
        lo, hi = (-1, 0) if span == 1 else (-(span // 2), span // 2)
        win = src[pl.ds(2 * pad + lo, tm), src_cols] + src[pl.ds(2 * pad + hi, tm), src_cols]
        cnt = (jnp.minimum(pos + half, seq) - jnp.maximum(pos - half, 0)).astype(F32)
        p = win * (1.0 / cnt) - h_sc[pl.ds(2 * pad, tm), cols]
        y = jnp.dot(p.astype(BF16), w_ref[j], preferred_element_type=F32)
        m_sc[:, cols] = y * cs_ref[:, cols]
    o_ref[...] = x_ref[...] + gate_ref[0] * _rms(m_sc[...], gpost_ref[...])


def _pool_call(x, mod_l, g_pre, g_post, layer, w_pool, c_scale, c_layer, trunk):
    rows, d = x.shape
    tm = min(_row_tile(trunk), trunk.seq)
    assert trunk.seq % tm == 0 and tm % POOL_HALO == 0 and max(POOL_WINDOWS) // 2 <= POOL_HALO
    groups, gdim = w_pool.shape[1], w_pool.shape[2]
    assert groups == len(POOL_WINDOWS) and POOL_WINDOWS == (2, 4, 8, 16)
    halo_per_tile = tm // POOL_HALO
    last_halo = rows // POOL_HALO - 1
    blocks = [((tm, d), F32), ((tm, d), F32), ((groups, gdim, gdim), BF16)]
    ext_rows = tm + 4 * POOL_HALO
    scratch = [((ext_rows, d), F32), ((ext_rows, gdim), F32), ((ext_rows, gdim), F32), ((tm, d), F32)]
    return pl.pallas_call(
        functools.partial(_pool_kernel, tiles_per_seq=trunk.seq // tm),
        grid=(rows // tm,),
        in_specs=[
            pl.BlockSpec((tm, d), lambda i: (i, 0)),
            pl.BlockSpec((POOL_HALO, d), lambda i: (jnp.maximum(i * halo_per_tile - 1, 0), 0)),
            pl.BlockSpec((POOL_HALO, d), lambda i: (jnp.minimum((i + 1) * halo_per_tile, last_halo), 0)),
            pl.BlockSpec((1, 1, d), _mod_spec(trunk, tm, 0)),
            pl.BlockSpec((1, 1, d), _mod_spec(trunk, tm, 1)),
            pl.BlockSpec((1, 1, d), _mod_spec(trunk, tm, 2)),
            pl.BlockSpec((None, 1, d), lambda i: (layer, 0, 0)),
            pl.BlockSpec((None, 1, d), lambda i: (layer, 0, 0)),
            pl.BlockSpec((None, groups, gdim, gdim), lambda i: (c_layer, 0, 0, 0)),
            pl.BlockSpec((None, 1, d), lambda i: (c_layer, 0, 0)),
        ],
        out_specs=pl.BlockSpec((tm, d), lambda i: (i, 0)),
        out_shape=jax.ShapeDtypeStruct((rows, d), F32),
        scratch_shapes=[pltpu.VMEM(s, t) for s, t in scratch],
        compiler_params=_params(("parallel",), _vmem_limit(blocks, scratch, temps=[((tm, d), F32)] * 2)),
        name="pool",
    )(x, x, x, mod_l, mod_l, mod_l, g_pre, g_post, w_pool, c_scale)


def _trunk(x3, trunk, mod, cache_kv, p):
    b, t, d = x3.shape
    x = x3.reshape(b * t, d)
    depth = p["norm_mix_pre"].shape[0]
    n_kv = p["n_kv"]
    ia = ib = ic = 0
    new_k, new_v = [], []
    for l in range(depth):
        mod_l = mod[l]
        g_pre, g_post = p["norm_mix_pre"], p["norm_mix_post"]
        kind = l % 3
        if kind == 0:
            y = _mixa_call(x, mod_l, g_pre, l, p["a_w_in"], p["a_norm_v"], p["a_w_s"], p["a_b_s"], ia, trunk)
            x = _proj_call(y, p["a_w_out"], ia, x, mod_l, g_post, l, trunk)
            ia += 1
        elif kind == 1:
            kv_dtype = BF16 if trunk.latent else F32
            q, k, v = _qkv_call(x, mod_l, g_pre, l, p["b_w_qkv"], p["b_q_norm"], p["b_k_norm"], ib, n_kv, trunk,
                                kv_dtype)
            k3, v3 = k.reshape(b, t, -1), v.reshape(b, t, -1)
            if trunk.latent:
                ck, cv = cache_kv
                past = ck.shape[2]
                k3 = jnp.concatenate([k3, ck[:, ib].reshape(b, past, -1).astype(BF16)], axis=1)
                v3 = jnp.concatenate([v3, cv[:, ib].reshape(b, past, -1).astype(BF16)], axis=1)
            else:
                new_k.append(k3.reshape(b, t, n_kv, -1))
                new_v.append(v3.reshape(b, t, n_kv, -1))
            o = _attn_call(q.reshape(b, t, -1), k3, v3, n_kv)
            x = _proj_call(o.reshape(b * t, -1), p["b_w_o"], ib, x, mod_l, g_post, l, trunk)
            ib += 1
        else:
            x = _pool_call(x, mod_l, g_pre, g_post, l, p["c_w_pool"], p["c_scale"], ic, trunk)
            ic += 1
        x = _ffn_call(x, mod_l, p["norm_ffn_pre"], p["norm_ffn_post"], p["f_w_gu"], p["f_w_down"], l, trunk)
    return x.reshape(b, t, d), new_k, new_v


def kernel(x_prompt, x_sample, cache_k, cache_v, c, c_ctx, w_mod, b_mod, norm_mix_pre, norm_mix_post, norm_ffn_pre, norm_ffn_post, a_w_in, a_norm_v, a_w_s, a_b_s, a_w_out, b_w_qkv, b_q_norm, b_k_norm, b_w_o, c_w_pool, c_scale, f_w_gu, f_w_down):
    batch, seq, d = x_prompt.shape
    dec_batch, dec_seq, _ = x_sample.shape
    depth = w_mod.shape[0]
    assert 1 + dec_batch <= MOD_ROWS

    cond = jnp.concatenate([c_ctx[None, :], c, jnp.zeros((MOD_ROWS - 1 - dec_batch, d), F32)], axis=0)
    mod = _mod_call(cond, w_mod, b_mod).reshape(depth, MOD_ROWS * N_MOD, 1, d)

    rows3 = lambda a: a.reshape(a.shape[0], 1, a.shape[-1])
    p = {
        "n_kv": cache_k.shape[3],
        "norm_mix_pre": rows3(norm_mix_pre), "norm_mix_post": rows3(norm_mix_post),
        "norm_ffn_pre": rows3(norm_ffn_pre), "norm_ffn_post": rows3(norm_ffn_post),
        "a_w_in": a_w_in.astype(BF16), "a_norm_v": rows3(a_norm_v), "a_w_s": a_w_s.astype(BF16),
        "a_b_s": a_b_s[..., None], "a_w_out": a_w_out.astype(BF16),
        "b_w_qkv": b_w_qkv.astype(BF16), "b_q_norm": rows3(b_q_norm), "b_k_norm": rows3(b_k_norm),
        "b_w_o": b_w_o.astype(BF16),
        "c_w_pool": c_w_pool.astype(BF16), "c_scale": rows3(c_scale),
        "f_w_gu": f_w_gu.astype(BF16), "f_w_down": f_w_down.astype(BF16),
    }
    ctx = Trunk(batch=batch, seq=seq, mod_base=0, rows_per_cond=batch * seq, latent=False)
    lat = Trunk(batch=dec_batch, seq=dec_seq, mod_base=1, rows_per_cond=dec_seq, latent=True)
    y_prompt, new_k, new_v = _trunk(x_prompt, ctx, mod, None, p)
    y_sample, _, _ = _trunk(x_sample, lat, mod, (cache_k, cache_v), p)
    return (y_prompt, y_sample, jnp.stack(new_k, axis=1), jnp.stack(new_v, axis=1))
```

```python
import collections
import functools

import jax
import jax.numpy as jnp
from jax import lax
from jax.experimental import pallas as pl
from jax.experimental.pallas import tpu as pltpu

F32 = jnp.float32
BF16 = jnp.bfloat16

EPS = 1e-6
N_MOD = 6
GRID_W = 64
ROPE_THETA = 10000.0
POOL_WINDOWS = (2, 4, 8, 16)
POOL_HALO = 8
MOD_ROWS = 16
LOG2_E = 1.4426950408889634
V7X_LANES = 128

V7X_VMEM_BUDGET = 58 * 1024 * 1024
ROW_TILE = 512
ROW_GROUP = 16
ROW_GROUP_UNROLL = 8
FFN_HIDDEN_TILE = 512
FFN_ROW_TILE = 1024
COL_TILE = 512
ATTN_Q_TILE = 256
ATTN_MAX_KEYS = 4608
ATTN_HEAD_GROUPS = 2
MOD_COL_TILE = 1024

Trunk = collections.namedtuple("Trunk", "batch seq mod_base rows_per_cond latent")


def _nbytes(shape, dtype):
    n = 1
    for s in shape:
        n *= s
    return n * jnp.dtype(dtype).itemsize


def _vmem_limit(pipelined, scratch=(), temps=()):
    total = 2 * sum(_nbytes(s, d) for s, d in pipelined)
    total += sum(_nbytes(s, d) for s, d in scratch)
    total += sum(_nbytes(s, d) for s, d in temps)
    return int(min(max(total + (4 << 20), 16 << 20), V7X_VMEM_BUDGET))


def _params(semantics, limit):
    return pltpu.CompilerParams(dimension_semantics=semantics, vmem_limit_bytes=limit)


def _rms(xf, g):
    ms = jnp.mean(xf * xf, axis=-1, keepdims=True)
    return (xf * lax.rsqrt(ms + EPS)) * g


def _norm_mod(xf, g, scale, shift):
    return _rms(xf, g) * (1.0 + scale) + shift


def _row_groups(row0, n_rows, body, static_rows=False):
    assert n_rows % ROW_GROUP == 0 and row0 % ROW_GROUP == 0
    if static_rows:
        for r in range(row0, row0 + n_rows, ROW_GROUP):
            body(pl.ds(r, ROW_GROUP))
        return

    def step(r, carry):
        body(pl.ds(pl.multiple_of(row0 + r * ROW_GROUP, ROW_GROUP), ROW_GROUP))
        return carry
    lax.fori_loop(0, n_rows // ROW_GROUP, step, 0, unroll=ROW_GROUP_UNROLL)


def _norm_mod_rows(x_ref, dst_ref, gpre_ref, scale_ref, shift_ref, row0=0, n_rows=None, static_rows=False):
    d = x_ref.shape[1]
    n_rows = x_ref.shape[0] if n_rows is None else n_rows
    gain = jnp.broadcast_to(gpre_ref[...] * (1.0 + scale_ref[0]), (ROW_GROUP, d))
    shift = jnp.broadcast_to(shift_ref[0], (ROW_GROUP, d))

    def body(rows):
        xr = x_ref[rows, :]
        ms = jnp.mean(xr * xr, axis=-1, keepdims=True)
        h = (xr * lax.rsqrt(ms + EPS)) * gain + shift
        dst_ref[rows, :] = h.astype(dst_ref.dtype)
    _row_groups(row0, n_rows, body, static_rows)


def _gated_residual_rows(f_ref, x_ref, o_ref, gate_ref, gpost_ref, row0=0, n_rows=None):
    d = x_ref.shape[1]
    n_rows = x_ref.shape[0] if n_rows is None else n_rows
    gain = jnp.broadcast_to(gate_ref[0] * gpost_ref[...], (ROW_GROUP, d))

    def body(rows):
        f = f_ref[rows, :]
        ms = jnp.mean(f * f, axis=-1, keepdims=True)
        o_ref[rows, :] = x_ref[rows, :] + (f * lax.rsqrt(ms + EPS)) * gain
    _row_groups(row0, n_rows, body, static_rows=f_ref is o_ref)


def _row_tile(trunk):
    rows = trunk.batch * trunk.seq
    tm = min(ROW_TILE, rows)
    assert rows % tm == 0 and (trunk.rows_per_cond % tm == 0)
    return tm


def _mod_spec(trunk, tm, m, ahead=0):
    last_tile = trunk.batch * trunk.seq // tm - 1

    def index(i, *_):
        tile = jnp.minimum(i + ahead, last_tile)
        return ((trunk.mod_base + (tile * tm) // trunk.rows_per_cond) * N_MOD + m, 0, 0)
    return index


def _mod_kernel(c_ref, w_ref, b_ref, o_ref):
    c = c_ref[...]
    s = (c * jax.nn.sigmoid(c)).astype(BF16)
    o_ref[0] = jnp.dot(s, w_ref[0].astype(BF16), preferred_element_type=F32) + b_ref[0]


def _mod_call(cond, w_mod, b_mod):
    depth, d, n = w_mod.shape
    tn = min(MOD_COL_TILE, n)
    assert n % tn == 0
    blocks = [((MOD_ROWS, d), F32), ((1, d, tn), F32), ((1, 1, tn), F32), ((1, MOD_ROWS, tn), F32)]
    return pl.pallas_call(
        _mod_kernel,
        grid=(depth, n // tn),
        in_specs=[
            pl.BlockSpec((MOD_ROWS, d), lambda l, j: (0, 0)),
            pl.BlockSpec((1, d, tn), lambda l, j: (l, 0, j)),
            pl.BlockSpec((1, 1, tn), lambda l, j: (l, 0, j)),
        ],
        out_specs=pl.BlockSpec((1, MOD_ROWS, tn), lambda l, j: (l, 0, j)),
        out_shape=jax.ShapeDtypeStruct((depth, MOD_ROWS, n), F32),
        compiler_params=_params(("parallel", "parallel"), _vmem_limit(blocks, temps=[((d, tn), BF16)])),
        name="mod",
    )(cond, w_mod, b_mod.reshape(depth, 1, n))


def _ffn_kernel(x_ref, shift_ref, scale_ref, gate_ref, gpre_ref, gpost_ref, wg_ref, wu_ref, wd_ref,
                o_ref, h_sc, *, n_chunks, sub_rows):
    c = pl.program_id(1)
    tm = x_ref.shape[0]
    parts = list(range(0, tm, sub_rows))

    def part(r):
        h = h_sc[r:r + sub_rows, :]
        g = jnp.dot(h, wg_ref[...], preferred_element_type=F32)
        u = jnp.dot(h, wu_ref[...], preferred_element_type=F32)
        a = (g * jax.nn.sigmoid(g)) * u
        return jnp.dot(a.astype(BF16), wd_ref[...], preferred_element_type=F32)

    @pl.when(c == 0)
    def _():
        _norm_mod_rows(x_ref, h_sc, gpre_ref, scale_ref, shift_ref, 0, sub_rows)
        for r in parts:
            if r + sub_rows < tm:
                _norm_mod_rows(x_ref, h_sc, gpre_ref, scale_ref, shift_ref, r + sub_rows, sub_rows, static_rows=True)
            o_ref[r:r + sub_rows, :] = part(r)

    @pl.when(jnp.logical_and(c > 0, c < n_chunks - 1))
    def _():
        for r in parts:
            o_ref[r:r + sub_rows, :] += part(r)

    @pl.when(c == n_chunks - 1)
    def _():
        for r in parts:
            o_ref[r:r + sub_rows, :] += part(r)
            _gated_residual_rows(o_ref, x_ref, o_ref, gate_ref, gpost_ref, r, sub_rows)


def _ffn_call(x, mod_l, g_pre, g_post, w_gu, w_down, layer, trunk):
    rows, d = x.shape
    hidden = w_down.shape[1]
    tm = min(FFN_ROW_TILE, rows)
    sub = min(ROW_TILE, tm)
    assert rows % tm == 0 and trunk.rows_per_cond % tm == 0 and tm % sub == 0
    th = min(FFN_HIDDEN_TILE, hidden)
    nc = hidden // th
    assert hidden % th == 0 and nc >= 2
    blocks = [((tm, d), F32)] * 2 + [((d, th), BF16), ((d, th), BF16), ((th, d), BF16)]
    scratch = [((tm, d), BF16)]
    temps = [((sub, th), F32)] * 4 + [((sub, d), F32)]
    return pl.pallas_call(
        functools.partial(_ffn_kernel, n_chunks=nc, sub_rows=sub),
        grid=(rows // tm, nc),
        in_specs=[
            pl.BlockSpec((tm, d), lambda i, c: (i, 0)),
            pl.BlockSpec((1, 1, d), _mod_spec(trunk, tm, 3)),
            pl.BlockSpec((1, 1, d), _mod_spec(trunk, tm, 4)),
            pl.BlockSpec((1, 1, d), _mod_spec(trunk, tm, 5)),
            pl.BlockSpec((None, 1, d), lambda i, c: (layer, 0, 0)),
            pl.BlockSpec((None, 1, d), lambda i, c: (layer, 0, 0)),
            pl.BlockSpec((None, d, th), lambda i, c: (layer, 0, c)),
            pl.BlockSpec((None, d, th), lambda i, c: (layer, 0, nc + c)),
            pl.BlockSpec((None, th, d), lambda i, c: (layer, c, 0)),
        ],
        out_specs=pl.BlockSpec((tm, d), lambda i, c: (i, 0)),
        out_shape=jax.ShapeDtypeStruct((rows, d), F32),
        scratch_shapes=[pltpu.VMEM(s, t) for s, t in scratch],
        compiler_params=_params(("parallel", "arbitrary"), _vmem_limit(blocks, scratch, temps)),
        name="ffn",
    )(x, mod_l, mod_l, mod_l, g_pre, g_post, w_gu, w_gu, w_down)


def _proj_kernel(y_ref, w_ref, x_ref, gate_ref, gpost_ref, o_ref):
    m = jnp.dot(y_ref[...], w_ref[...], preferred_element_type=F32)
    o_ref[...] = x_ref[...] + gate_ref[0] * _rms(m, gpost_ref[...])


def _proj_call(y, w, w_layer, x, mod_l, g_post, layer, trunk):
    rows, d = x.shape
    k = y.shape[1]
    tm = _row_tile(trunk)
    blocks = [((tm, k), BF16), ((k, d), BF16), ((tm, d), F32), ((tm, d), F32)]
    return pl.pallas_call(
        _proj_kernel,
        grid=(rows // tm,),
        in_specs=[
            pl.BlockSpec((tm, k), lambda i: (i, 0)),
            pl.BlockSpec((None, k, d), lambda i: (w_layer, 0, 0)),
            pl.BlockSpec((tm, d), lambda i: (i, 0)),
            pl.BlockSpec((1, 1, d), _mod_spec(trunk, tm, 2)),
            pl.BlockSpec((None, 1, d), lambda i: (layer, 0, 0)),
        ],
        out_specs=pl.BlockSpec((tm, d), lambda i: (i, 0)),
        out_shape=jax.ShapeDtypeStruct((rows, d), F32),
        compiler_params=_params(("parallel",), _vmem_limit(blocks, temps=[((tm, d), F32)] * 2)),
        name="proj",
    )(y, w, x, mod_l, g_post)


def _mixa_kernel(x_ref, shift_ref, scale_ref, gpre_ref, win_ref, gv_ref, ws_ref, bs_ref, y_ref, h_sc,
                 *, col_tile, chunk, groups):
    tm = x_ref.shape[0]
    width = win_ref.shape[1] // 2
    tn = col_tile
    n_u = width // tn
    gdim = width // groups
    _norm_mod_rows(x_ref, h_sc, gpre_ref, scale_ref, shift_ref)
    h = h_sc[...]
    vs = [jnp.dot(h, win_ref[:, width + b * tn:width + (b + 1) * tn], preferred_element_type=F32)
          for b in range(n_u)]
    ss = jnp.zeros((tm, 1), F32)
    for vb in vs:
        ss = ss + jnp.sum(vb * vb, axis=-1, keepdims=True)
    inv = lax.rsqrt(ss / width + EPS)
    for b in range(n_u):
        u = jnp.dot(h, win_ref[:, b * tn:(b + 1) * tn], preferred_element_type=F32)
        for off in range(0, tn, gdim):
            col = b * tn + off
            g = col // gdim
            gv = gv_ref[:, col:col + gdim]
            for c in range(tm // chunk):
                r = c * chunk
                v = (vs[b][r:r + chunk, off:off + gdim] * inv[r:r + chunk]) * gv
                s = jnp.dot(ws_ref[g], v.astype(BF16), preferred_element_type=F32) + bs_ref[g]
                y_ref[r:r + chunk, col:col + gdim] = (u[r:r + chunk, off:off + gdim] * s).astype(y_ref.dtype)


def _mixa_call(x, mod_l, g_pre, layer, w_in, g_v, w_s, b_s, a_layer, trunk):
    rows, d = x.shape
    width = w_in.shape[2] // 2
    groups, chunk = w_s.shape[1], w_s.shape[2]
    tm = _row_tile(trunk)
    tn = min(COL_TILE, width)
    gdim = width // groups
    assert width % tn == 0 and tn % gdim == 0 and tm % chunk == 0 and trunk.seq % chunk == 0
    blocks = [((tm, d), F32), ((tm, width), BF16), ((groups, chunk, chunk), BF16), ((groups, chunk, V7X_LANES), F32)]
    scratch = [((tm, d), BF16)]
    resident = [((d, 2 * width), BF16)]
    return pl.pallas_call(
        functools.partial(_mixa_kernel, col_tile=tn, chunk=chunk, groups=groups),
        grid=(rows // tm,),
        in_specs=[
            pl.BlockSpec((tm, d), lambda i: (i, 0)),
            pl.BlockSpec((1, 1, d), _mod_spec(trunk, tm, 0)),
            pl.BlockSpec((1, 1, d), _mod_spec(trunk, tm, 1)),
            pl.BlockSpec((None, 1, d), lambda i: (layer, 0, 0)),
            pl.BlockSpec((None, d, 2 * width), lambda i: (a_layer, 0, 0), pipeline_mode=pl.Buffered(1)),
            pl.BlockSpec((None, 1, width), lambda i: (a_layer, 0, 0)),
            pl.BlockSpec((None, groups, chunk, chunk), lambda i: (a_layer, 0, 0, 0)),
            pl.BlockSpec((None, groups, chunk, 1), lambda i: (a_layer, 0, 0, 0)),
        ],
        out_specs=pl.BlockSpec((tm, width), lambda i: (i, 0)),
        out_shape=jax.ShapeDtypeStruct((rows, width), BF16),
        scratch_shapes=[pltpu.VMEM(s, t) for s, t in scratch],
        compiler_params=_params(("parallel",),
                                _vmem_limit(blocks, scratch + resident, temps=[((tm, 2 * width), F32)])),
        name="mixa",
    )(x, mod_l, mod_l, g_pre, w_in, g_v, w_s, b_s)


def _rope_tables(seq, head_dim):
    axis_dim = head_dim // 2
    t = jnp.arange(seq)
    n_rows = seq // GRID_W
    row = jnp.minimum(t // GRID_W, n_rows - 1).astype(F32)
    col = (t % GRID_W).astype(F32)
    inv = jnp.power(ROPE_THETA, -jnp.arange(0, axis_dim, 2, dtype=F32) / axis_dim)
    ang = jnp.concatenate([row[:, None] * inv, col[:, None] * inv], axis=-1)
    cos, sin = jnp.cos(ang), jnp.sin(ang)
    zero = jnp.zeros_like(sin)
    cos2 = jnp.stack([cos, cos], axis=-1).reshape(seq, head_dim)
    sin_a = jnp.stack([-sin, zero], axis=-1).reshape(seq, head_dim)
    sin_b = jnp.stack([zero, sin], axis=-1).reshape(seq, head_dim)
    return cos2, sin_a, sin_b


def _qkv_kernel(*refs, n_q_blocks, head_dim, latent, q_scale):
    if latent:
        (x_ref, shift_ref, scale_ref, gpre_ref, w_ref, qg_ref, kg_ref, cos_ref, sa_ref, sb_ref,
         q_ref, k_ref, v_ref, h_sc) = refs
    else:
        (x_ref, shift_ref, scale_ref, gpre_ref, w_ref, qg_ref, kg_ref, q_ref, k_ref, v_ref, h_sc) = refs
    tn = k_ref.shape[1]
    n_heads = tn // head_dim
    _norm_mod_rows(x_ref, h_sc, gpre_ref, scale_ref, shift_ref)
    h = h_sc[...]

    def head(a, hh, gain):
        blk = _rms(a[:, hh * head_dim:(hh + 1) * head_dim], gain)
        if latent:
            nxt = pltpu.roll(blk, head_dim - 1, 1)
            prv = pltpu.roll(blk, 1, 1)
            blk = blk * cos_ref[...] + nxt * sa_ref[...] + prv * sb_ref[...]
        return blk

    for block in range(n_q_blocks + 2):
        a = jnp.dot(h, w_ref[:, block * tn:(block + 1) * tn], preferred_element_type=F32)
        if block < n_q_blocks:
            for hh in range(n_heads):
                col = block * tn + hh * head_dim
                q_ref[:, col:col + head_dim] = (head(a, hh, qg_ref[...]) * q_scale).astype(q_ref.dtype)
        elif block == n_q_blocks:
            for hh in range(n_heads):
                k_ref[:, hh * head_dim:(hh + 1) * head_dim] = head(a, hh, kg_ref[...]).astype(k_ref.dtype)
        else:
            v_ref[...] = a.astype(v_ref.dtype)


def _qkv_call(x, mod_l, g_pre, layer, w_qkv, q_gain, k_gain, b_layer, n_kv, trunk, kv_dtype):
    rows, d = x.shape
    hd = q_gain.shape[-1]
    tn = n_kv * hd
    cols = w_qkv.shape[2]
    nq = cols - 2 * tn
    assert nq % tn == 0
    nqb = nq // tn
    tm = _row_tile(trunk)
    in_specs = [
        pl.BlockSpec((tm, d), lambda i: (i, 0)),
        pl.BlockSpec((1, 1, d), _mod_spec(trunk, tm, 0)),
        pl.BlockSpec((1, 1, d), _mod_spec(trunk, tm, 1)),
        pl.BlockSpec((None, 1, d), lambda i: (layer, 0, 0)),
        pl.BlockSpec((None, d, cols), lambda i: (b_layer, 0, 0), pipeline_mode=pl.Buffered(1)),
        pl.BlockSpec((None, 1, hd), lambda i: (b_layer, 0, 0)),
        pl.BlockSpec((None, 1, hd), lambda i: (b_layer, 0, 0)),
    ]
    args = [x, mod_l, mod_l, g_pre, w_qkv, q_gain, k_gain]
    if trunk.latent:
        assert trunk.seq % tm == 0 and trunk.seq % GRID_W == 0
        tiles_per_seq = trunk.seq // tm
        in_specs += [pl.BlockSpec((tm, hd), lambda i: (i % tiles_per_seq, 0))] * 3
        args += list(_rope_tables(trunk.seq, hd))
    blocks = [((tm, d), F32), ((tm, nq), BF16), ((tm, tn), kv_dtype), ((tm, tn), kv_dtype),
              ((tm, hd), F32), ((tm, hd), F32), ((tm, hd), F32)]
    scratch = [((tm, d), BF16)]
    resident = [((d, cols), BF16)]
    return pl.pallas_call(
        functools.partial(_qkv_kernel, n_q_blocks=nqb, head_dim=hd, latent=trunk.latent,
                          q_scale=hd ** -0.5 * LOG2_E),
        grid=(rows // tm,),
        in_specs=in_specs,
        out_specs=[
            pl.BlockSpec((tm, nq), lambda i: (i, 0)),
            pl.BlockSpec((tm, tn), lambda i: (i, 0)),
            pl.BlockSpec((tm, tn), lambda i: (i, 0)),
        ],
        out_shape=[
            jax.ShapeDtypeStruct((rows, nq), BF16),
            jax.ShapeDtypeStruct((rows, tn), kv_dtype),
            jax.ShapeDtypeStruct((rows, tn), kv_dtype),
        ],
        scratch_shapes=[pltpu.VMEM(s, t) for s, t in scratch],
        compiler_params=_params(("parallel",),
                                _vmem_limit(blocks, scratch + resident, temps=[((tm, tn), F32)] * (nqb + 2))),
        name="qkv",
    )(*args)


def _attn_kernel(q_ref, k_ref, v_ref, o_ref, *, q_per_kv, head_dim):
    hd = head_dim
    tq, tk = q_ref.shape[1], k_ref.shape[1]
    n_groups = ATTN_HEAD_GROUPS if q_per_kv % ATTN_HEAD_GROUPS == 0 else 1
    per = q_per_kv // n_groups
    k = k_ref[0].astype(BF16)
    v1 = jnp.concatenate([v_ref[0].astype(BF16), jnp.ones((tk, hd), BF16)], axis=1)
    qs = [jnp.concatenate([q_ref[0, :, g * hd:(g + 1) * hd] for g in range(i * per, (i + 1) * per)], axis=0)
          for i in range(n_groups)]
    ss = [lax.dot_general(q, k, (((1,), (1,)), ((), ())), preferred_element_type=F32) for q in qs]
    m_curs = [jnp.max(s, axis=-1, keepdims=True) for s in ss]

    def write(i, acc):
        out = acc[:, :hd] / acc[:, hd:]
        for j in range(per):
            g = i * per + j
            o_ref[0, :, g * hd:(g + 1) * hd] = out[j * tq:(j + 1) * tq].astype(o_ref.dtype)

    ps = [jnp.exp2(s - m).astype(BF16) for s, m in zip(ss, m_curs)]
    accs = [jnp.dot(p, v1, preferred_element_type=F32) for p in ps]
    for i, acc in enumerate(accs):
        write(i, acc)


def _attn_call(q, k, v, n_kv):
    b, t, nq = q.shape
    s = k.shape[1]
    hd = k.shape[2] // n_kv
    gw = nq // n_kv
    tq = min(ATTN_Q_TILE, t)
    assert t % tq == 0 and s <= ATTN_MAX_KEYS and s % V7X_LANES == 0 and hd % V7X_LANES == 0
    rows = (gw // hd) * tq
    blocks = [((1, tq, gw), BF16), ((1, s, hd), k.dtype), ((1, s, hd), v.dtype), ((1, tq, gw), BF16)]
    temps = [((rows, s), F32)] * 2 + [((rows, s), BF16), ((s, 2 * hd), BF16), ((rows, 2 * hd), F32)]
    return pl.pallas_call(
        functools.partial(_attn_kernel, q_per_kv=gw // hd, head_dim=hd),
        grid=(b, n_kv, t // tq),
        in_specs=[
            pl.BlockSpec((1, tq, gw), lambda bi, h, qi: (bi, qi, h)),
            pl.BlockSpec((1, s, hd), lambda bi, h, qi: (bi, 0, h)),
            pl.BlockSpec((1, s, hd), lambda bi, h, qi: (bi, 0, h)),
        ],
        out_specs=pl.BlockSpec((1, tq, gw), lambda bi, h, qi: (bi, qi, h)),
        out_shape=jax.ShapeDtypeStruct((b, t, nq), BF16),
        compiler_params=_params(("parallel", "parallel", "parallel"), _vmem_limit(blocks, temps=temps)),
        name="attn",
    )(q, k, v)


def _pool_kernel(x_ref, xp_ref, xn_ref, shift_ref, scale_ref, gate_ref, gpre_ref, gpost_ref, w_ref, cs_ref,
                 o_ref, h_sc, ta_sc, tb_sc, m_sc, *, tiles_per_seq):
    tm, d = x_ref.shape
    pad = POOL_HALO
    gdim = d // len(POOL_WINDOWS)
    gpre, scale, shift = gpre_ref[...], scale_ref[0], shift_ref[0]
    t_in_seq = lax.rem(pl.program_id(0), tiles_per_seq)
    keep_prev = (t_in_seq > 0).astype(F32)
    keep_next = (t_in_seq < tiles_per_seq - 1).astype(F32)
    zeros = jnp.zeros((pad, d), F32)
    h_sc[0:pad, :] = zeros
    h_sc[pad:2 * pad, :] = _norm_mod(xp_ref[...], gpre, scale, shift) * keep_prev
    h_sc[2 * pad:2 * pad + tm, :] = _norm_mod(x_ref[...], gpre, scale, shift)
    h_sc[2 * pad + tm:3 * pad + tm, :] = _norm_mod(xn_ref[...], gpre, scale, shift) * keep_next
    h_sc[3 * pad + tm:, :] = zeros
    for t_sc in (ta_sc, tb_sc):
        t_sc[0:pad, :] = zeros[:, :gdim]
        t_sc[3 * pad + tm:, :] = zeros[:, :gdim]

    ext = tm + 2 * pad

    def shifted_sum(src, cols, lo, hi):
        return src[pl.ds(pad + lo, ext), cols] + src[pl.ds(pad + hi, ext), cols]

    pos = t_in_seq * tm + lax.broadcasted_iota(jnp.int32, (tm, 1), 0)
    seq = tiles_per_seq * tm
    all_cols = pl.ds(0, gdim)
    for j, w in enumerate(POOL_WINDOWS):
        half = w // 2
        cols = pl.ds(j * gdim, gdim)
        src, src_cols, span = h_sc, cols, 1
        for dst in (ta_sc, tb_sc, ta_sc):
            if span * 2 >= w:
                break
            lo, hi = (-1, 0) if span == 1 else (-(span // 2), span // 2)
            dst[pl.ds(pad, ext), :] = shifted_sum(src, src_cols, lo, hi)
            src, src_cols, span = dst, all_cols, span * 2
        lo, hi = (-1, 0) if span == 1 else (-(span // 2), span // 2)
        win = src[pl.ds(2 * pad + lo, tm), src_cols] + src[pl.ds(2 * pad + hi, tm), src_cols]
        cnt = (jnp.minimum(pos + half, seq) - jnp.maximum(pos - half, 0)).astype(F32)
        p = win * (1.0 / cnt) - h_sc[pl.ds(2 * pad, tm), cols]
        y = jnp.dot(p.astype(BF16), w_ref[j], preferred_element_type=F32)
        m_sc[:, cols] = y * cs_ref[:, cols]
    o_ref[...] = x_ref[...] + gate_ref[0] * _rms(m_sc[...], gpost_ref[...])


def _pool_call(x, mod_l, g_pre, g_post, layer, w_pool, c_scale, c_layer, trunk):
    rows, d = x.shape
    tm = min(_row_tile(trunk), trunk.seq)
    assert trunk.seq % tm == 0 and tm % POOL_HALO == 0 and max(POOL_WINDOWS) // 2 <= POOL_HALO
    groups, gdim = w_pool.shape[1], w_pool.shape[2]
    assert groups == len(POOL_WINDOWS) and POOL_WINDOWS == (2, 4, 8, 16)
    halo_per_tile = tm // POOL_HALO
    last_halo = rows // POOL_HALO - 1
    blocks = [((tm, d), F32), ((tm, d), F32), ((groups, gdim, gdim), BF16)]
    ext_rows = tm + 4 * POOL_HALO
    scratch = [((ext_rows, d), F32), ((ext_rows, gdim), F32), ((ext_rows, gdim), F32), ((tm, d), F32)]
    return pl.pallas_call(
        functools.partial(_pool_kernel, tiles_per_seq=trunk.seq // tm),
        grid=(rows // tm,),
        in_specs=[
            pl.BlockSpec((tm, d), lambda i: (i, 0)),
            pl.BlockSpec((POOL_HALO, d), lambda i: (jnp.maximum(i * halo_per_tile - 1, 0), 0)),
            pl.BlockSpec((POOL_HALO, d), lambda i: (jnp.minimum((i + 1) * halo_per_tile, last_halo), 0)),
            pl.BlockSpec((1, 1, d), _mod_spec(trunk, tm, 0)),
            pl.BlockSpec((1, 1, d), _mod_spec(trunk, tm, 1)),
            pl.BlockSpec((1, 1, d), _mod_spec(trunk, tm, 2)),
            pl.BlockSpec((None, 1, d), lambda i: (layer, 0, 0)),
            pl.BlockSpec((None, 1, d), lambda i: (layer, 0, 0)),
            pl.BlockSpec((None, groups, gdim, gdim), lambda i: (c_layer, 0, 0, 0)),
            pl.BlockSpec((None, 1, d), lambda i: (c_layer, 0, 0)),
        ],
        out_specs=pl.BlockSpec((tm, d), lambda i: (i, 0)),
        out_shape=jax.ShapeDtypeStruct((rows, d), F32),
        scratch_shapes=[pltpu.VMEM(s, t) for s, t in scratch],
        compiler_params=_params(("parallel",), _vmem_limit(blocks, scratch, temps=[((tm, d), F32)] * 2)),
        name="pool",
    )(x, x, x, mod_l, mod_l, mod_l, g_pre, g_post, w_pool, c_scale)


def _trunk(x3, trunk, mod, cache_kv, p):
    b, t, d = x3.shape
    x = x3.reshape(b * t, d)
    depth = p["norm_mix_pre"].shape[0]
    n_kv = p["n_kv"]
    ia = ib = ic = 0
    new_k, new_v = [], []
    for l in range(depth):
        mod_l = mod[l]
        g_pre, g_post = p["norm_mix_pre"], p["norm_mix_post"]
        kind = l % 3
        if kind == 0:
            y = _mixa_call(x, mod_l, g_pre, l, p["a_w_in"], p["a_norm_v"], p["a_w_s"], p["a_b_s"], ia, trunk)
            x = _proj_call(y, p["a_w_out"], ia, x, mod_l, g_post, l, trunk)
            ia += 1
        elif kind == 1:
            kv_dtype = BF16 if trunk.latent else F32
            q, k, v = _qkv_call(x, mod_l, g_pre, l, p["b_w_qkv"], p["b_q_norm"], p["b_k_norm"], ib, n_kv, trunk,
                                kv_dtype)
            k3, v3 = k.reshape(b, t, -1), v.reshape(b, t, -1)
            if trunk.latent:
                ck, cv = cache_kv
                past = ck.shape[2]
                k3 = jnp.concatenate([k3, ck[:, ib].reshape(b, past, -1).astype(BF16)], axis=1)
                v3 = jnp.concatenate([v3, cv[:, ib].reshape(b, past, -1).astype(BF16)], axis=1)
            else:
                new_k.append(k3.reshape(b, t, n_kv, -1))
                new_v.append(v3.reshape(b, t, n_kv, -1))
            o = _attn_call(q.reshape(b, t, -1), k3, v3, n_kv)
            x = _proj_call(o.reshape(b * t, -1), p["b_w_o"], ib, x, mod_l, g_post, l, trunk)
            ib += 1
        else:
            x = _pool_call(x, mod_l, g_pre, g_post, l, p["c_w_pool"], p["c_scale"], ic, trunk)
            ic += 1
        x = _ffn_call(x, mod_l, p["norm_ffn_pre"], p["norm_ffn_post"], p["f_w_gu"], p["f_w_down"], l, trunk)
    return x.reshape(b, t, d), new_k, new_v


def kernel(x_prompt, x_sample, cache_k, cache_v, c, c_ctx, w_mod, b_mod, norm_mix_pre, norm_mix_post, norm_ffn_pre, norm_ffn_post, a_w_in, a_norm_v, a_w_s, a_b_s, a_w_out, b_w_qkv, b_q_norm, b_k_norm, b_w_o, c_w_pool, c_scale, f_w_gu, f_w_down):
    batch, seq, d = x_prompt.shape
    dec_batch, dec_seq, _ = x_sample.shape
    depth = w_mod.shape[0]
    assert 1 + dec_batch <= MOD_ROWS

    cond = jnp.concatenate([c_ctx[None, :], c, jnp.zeros((MOD_ROWS - 1 - dec_batch, d), F32)], axis=0)
    mod = _mod_call(cond, w_mod, b_mod).reshape(depth, MOD_ROWS * N_MOD, 1, d)

    rows3 = lambda a: a.reshape(a.shape[0], 1, a.shape[-1])
    p = {
        "n_kv": cache_k.shape[3],
        "norm_mix_pre": rows3(norm_mix_pre), "norm_mix_post": rows3(norm_mix_post),
        "norm_ffn_pre": rows3(norm_ffn_pre), "norm_ffn_post": rows3(norm_ffn_post),
        "a_w_in": a_w_in.astype(BF16), "a_norm_v": rows3(a_norm_v), "a_w_s": a_w_s.astype(BF16),
        "a_b_s": a_b_s[..., None], "a_w_out": a_w_out.astype(BF16),
        "b_w_qkv": b_w_qkv.astype(BF16), "b_q_norm": rows3(b_q_norm), "b_k_norm": rows3(b_k_norm),
        "b_w_o": b_w_o.astype(BF16),
        "c_w_pool": c_w_pool.astype(BF16), "c_scale": rows3(c_scale),
        "f_w_gu": f_w_gu.astype(BF16), "f_w_down": f_w_down.astype(BF16),
    }
    ctx = Trunk(batch=batch, seq=seq, mod_base=0, rows_per_cond=batch * seq, latent=False)
    lat = Trunk(batch=dec_batch, seq=dec_seq, mod_base=1, rows_per_cond=dec_seq, latent=True)
    y_prompt, new_k, new_v = _trunk(x_prompt, ctx, mod, None, p)
    y_sample, _, _ = _trunk(x_sample, lat, mod, (cache_k, cache_v), p)
    return (y_prompt, y_sample, jnp.stack(new_k, axis=1), jnp.stack(new_v, axis=1))
```

```python
import collections
import functools

import jax
import jax.numpy as jnp
from jax import lax
from jax.experimental import pallas as pl
from jax.experimental.pallas import tpu as pltpu

F32 = jnp.float32
BF16 = jnp.bfloat16

EPS = 1e-6
N_MOD = 6
GRID_W = 64
ROPE_THETA = 10000.0
POOL_WINDOWS = (2, 4, 8, 16)
POOL_HALO = 8
MOD_ROWS = 16
LOG2_E = 1.4426950408889634
V7X_LANES = 128

V7X_VMEM_BUDGET = 58 * 1024 * 1024
ROW_TILE = 512
ROW_GROUP = 16
ROW_GROUP_UNROLL = 8
FFN_HIDDEN_TILE = 512
FFN_ROW_TILE = 1024
PROJ_ROW_TILE = 1024
QKV_ROW_TILE = 1024
COL_TILE = 512
ATTN_Q_TILE = 256
ATTN_MAX_KEYS = 4608
ATTN_HEAD_GROUPS = 2
MOD_COL_TILE = 1024

Trunk = collections.namedtuple("Trunk", "batch seq mod_base rows_per_cond latent")


def _nbytes(shape, dtype):
    n = 1
    for s in shape:
        n *= s
    return n * jnp.dtype(dtype).itemsize


def _vmem_limit(pipelined, scratch=(), temps=()):
    total = 2 * sum(_nbytes(s, d) for s, d in pipelined)
    total += sum(_nbytes(s, d) for s, d in scratch)
    total += sum(_nbytes(s, d) for s, d in temps)
    return int(min(max(total + (4 << 20), 16 << 20), V7X_VMEM_BUDGET))


def _params(semantics, limit):
    return pltpu.CompilerParams(dimension_semantics=semantics, vmem_limit_bytes=limit)


def _rms(xf, g):
    ms = jnp.mean(xf * xf, axis=-1, keepdims=True)
    return (xf * lax.rsqrt(ms + EPS)) * g


def _norm_mod(xf, g, scale, shift):
    return _rms(xf, g) * (1.0 + scale) + shift


def _row_groups(row0, n_rows, body, static_rows=False):
    assert n_rows % ROW_GROUP == 0 and row0 % ROW_GROUP == 0
    if static_rows:
        for r in range(row0, row0 + n_rows, ROW_GROUP):
            body(pl.ds(r, ROW_GROUP))
        return

    def step(r, carry):
        body(pl.ds(pl.multiple_of(row0 + r * ROW_GROUP, ROW_GROUP), ROW_GROUP))
        return carry
    lax.fori_loop(0, n_rows // ROW_GROUP, step, 0, unroll=ROW_GROUP_UNROLL)


def _norm_mod_rows(x_ref, dst_ref, gpre_ref, scale_ref, shift_ref, row0=0, n_rows=None, static_rows=False):
    d = x_ref.shape[1]
    n_rows = x_ref.shape[0] if n_rows is None else n_rows
    gain = jnp.broadcast_to(gpre_ref[...] * (1.0 + scale_ref[0]), (ROW_GROUP, d))
    shift = jnp.broadcast_to(shift_ref[0], (ROW_GROUP, d))

    def body(rows):
        xr = x_ref[rows, :]
        ms = jnp.mean(xr * xr, axis=-1, keepdims=True)
        h = (xr * lax.rsqrt(ms + EPS)) * gain + shift
        dst_ref[rows, :] = h.astype(dst_ref.dtype)
    _row_groups(row0, n_rows, body, static_rows)


def _gated_residual_rows(f_ref, x_ref, o_ref, gate_ref, gpost_ref, row0=0, n_rows=None):
    d = x_ref.shape[1]
    n_rows = x_ref.shape[0] if n_rows is None else n_rows
    gain = jnp.broadcast_to(gate_ref[0] * gpost_ref[...], (ROW_GROUP, d))

    def body(rows):
        f = f_ref[rows, :]
        ms = jnp.mean(f * f, axis=-1, keepdims=True)
        o_ref[rows, :] = x_ref[rows, :] + (f * lax.rsqrt(ms + EPS)) * gain
    _row_groups(row0, n_rows, body, static_rows=f_ref is o_ref)


def _row_tile(trunk):
    rows = trunk.batch * trunk.seq
    tm = min(ROW_TILE, rows)
    assert rows % tm == 0 and (trunk.rows_per_cond % tm == 0)
    return tm


def _mod_spec(trunk, tm, m, ahead=0):
    last_tile = trunk.batch * trunk.seq // tm - 1

    def index(i, *_):
        tile = jnp.minimum(i + ahead, last_tile)
        return ((trunk.mod_base + (tile * tm) // trunk.rows_per_cond) * N_MOD + m, 0, 0)
    return index


def _mod_kernel(c_ref, w_ref, b_ref, o_ref):
    c = c_ref[...]
    s = (c * jax.nn.sigmoid(c)).astype(BF16)
    o_ref[0] = jnp.dot(s, w_ref[0].astype(BF16), preferred_element_type=F32) + b_ref[0]


def _mod_call(cond, w_mod, b_mod):
    depth, d, n = w_mod.shape
    tn = min(MOD_COL_TILE, n)
    assert n % tn == 0
    blocks = [((MOD_ROWS, d), F32), ((1, d, tn), F32), ((1, 1, tn), F32), ((1, MOD_ROWS, tn), F32)]
    return pl.pallas_call(
        _mod_kernel,
        grid=(depth, n // tn),
        in_specs=[
            pl.BlockSpec((MOD_ROWS, d), lambda l, j: (0, 0)),
            pl.BlockSpec((1, d, tn), lambda l, j: (l, 0, j)),
            pl.BlockSpec((1, 1, tn), lambda l, j: (l, 0, j)),
        ],
        out_specs=pl.BlockSpec((1, MOD_ROWS, tn), lambda l, j: (l, 0, j)),
        out_shape=jax.ShapeDtypeStruct((depth, MOD_ROWS, n), F32),
        compiler_params=_params(("parallel", "parallel"), _vmem_limit(blocks, temps=[((d, tn), BF16)])),
        name="mod",
    )(cond, w_mod, b_mod.reshape(depth, 1, n))


def _ffn_kernel(x_ref, shift_ref, scale_ref, gate_ref, gpre_ref, gpost_ref, wg_ref, wu_ref, wd_ref,
                o_ref, h_sc, *, n_chunks, sub_rows):
    c = pl.program_id(1)
    tm = x_ref.shape[0]
    parts = list(range(0, tm, sub_rows))

    def part(r):
        h = h_sc[r:r + sub_rows, :]
        g = jnp.dot(h, wg_ref[...], preferred_element_type=F32)
        u = jnp.dot(h, wu_ref[...], preferred_element_type=F32)
        a = (g * jax.nn.sigmoid(g)) * u
        return jnp.dot(a.astype(BF16), wd_ref[...], preferred_element_type=F32)

    @pl.when(c == 0)
    def _():
        _norm_mod_rows(x_ref, h_sc, gpre_ref, scale_ref, shift_ref, 0, sub_rows)
        for r in parts:
            if r + sub_rows < tm:
                _norm_mod_rows(x_ref, h_sc, gpre_ref, scale_ref, shift_ref, r + sub_rows, sub_rows, static_rows=True)
            o_ref[r:r + sub_rows, :] = part(r)

    @pl.when(jnp.logical_and(c > 0, c < n_chunks - 1))
    def _():
        for r in parts:
            o_ref[r:r + sub_rows, :] += part(r)

    @pl.when(c == n_chunks - 1)
    def _():
        for r in parts:
            o_ref[r:r + sub_rows, :] += part(r)
            _gated_residual_rows(o_ref, x_ref, o_ref, gate_ref, gpost_ref, r, sub_rows)


def _ffn_call(x, mod_l, g_pre, g_post, w_gu, w_down, layer, trunk):
    rows, d = x.shape
    hidden = w_down.shape[1]
    tm = min(FFN_ROW_TILE, rows)
    sub = min(ROW_TILE, tm)
    assert rows % tm == 0 and trunk.rows_per_cond % tm == 0 and tm % sub == 0
    th = min(FFN_HIDDEN_TILE, hidden)
    nc = hidden // th
    assert hidden % th == 0 and nc >= 2
    blocks = [((tm, d), F32)] * 2 + [((d, th), BF16), ((d, th), BF16), ((th, d), BF16)]
    scratch = [((tm, d), BF16)]
    temps = [((sub, th), F32)] * 4 + [((sub, d), F32)]
    return pl.pallas_call(
        functools.partial(_ffn_kernel, n_chunks=nc, sub_rows=sub),
        grid=(rows // tm, nc),
        in_specs=[
            pl.BlockSpec((tm, d), lambda i, c: (i, 0)),
            pl.BlockSpec((1, 1, d), _mod_spec(trunk, tm, 3)),
            pl.BlockSpec((1, 1, d), _mod_spec(trunk, tm, 4)),
            pl.BlockSpec((1, 1, d), _mod_spec(trunk, tm, 5)),
            pl.BlockSpec((None, 1, d), lambda i, c: (layer, 0, 0)),
            pl.BlockSpec((None, 1, d), lambda i, c: (layer, 0, 0)),
            pl.BlockSpec((None, d, th), lambda i, c: (layer, 0, c)),
            pl.BlockSpec((None, d, th), lambda i, c: (layer, 0, nc + c)),
            pl.BlockSpec((None, th, d), lambda i, c: (layer, c, 0)),
        ],
        out_specs=pl.BlockSpec((tm, d), lambda i, c: (i, 0)),
        out_shape=jax.ShapeDtypeStruct((rows, d), F32),
        scratch_shapes=[pltpu.VMEM(s, t) for s, t in scratch],
        compiler_params=_params(("parallel", "arbitrary"), _vmem_limit(blocks, scratch, temps)),
        name="ffn",
    )(x, mod_l, mod_l, mod_l, g_pre, g_post, w_gu, w_gu, w_down)


def _proj_kernel(y_ref, w_ref, x_ref, gate_ref, gpost_ref, o_ref, *, sub_rows):
    for r in range(0, x_ref.shape[0], sub_rows):
        o_ref[r:r + sub_rows, :] = jnp.dot(y_ref[r:r + sub_rows, :], w_ref[...], preferred_element_type=F32)
        _gated_residual_rows(o_ref, x_ref, o_ref, gate_ref, gpost_ref, r, sub_rows)


def _proj_call(y, w, w_layer, x, mod_l, g_post, layer, trunk):
    rows, d = x.shape
    k = y.shape[1]
    tm = min(PROJ_ROW_TILE, rows)
    sub = min(ROW_TILE, tm)
    assert rows % tm == 0 and trunk.rows_per_cond % tm == 0 and tm % sub == 0
    blocks = [((tm, k), BF16), ((tm, d), F32), ((tm, d), F32)]
    return pl.pallas_call(
        functools.partial(_proj_kernel, sub_rows=sub),
        grid=(rows // tm,),
        in_specs=[
            pl.BlockSpec((tm, k), lambda i: (i, 0)),
            pl.BlockSpec((None, k, d), lambda i: (w_layer, 0, 0), pipeline_mode=pl.Buffered(1)),
            pl.BlockSpec((tm, d), lambda i: (i, 0)),
            pl.BlockSpec((1, 1, d), _mod_spec(trunk, tm, 2)),
            pl.BlockSpec((None, 1, d), lambda i: (layer, 0, 0)),
        ],
        out_specs=pl.BlockSpec((tm, d), lambda i: (i, 0)),
        out_shape=jax.ShapeDtypeStruct((rows, d), F32),
        compiler_params=_params(("parallel",), _vmem_limit(blocks, scratch=[((k, d), BF16)],
                                                           temps=[((sub, d), F32)])),
        name="proj",
    )(y, w, x, mod_l, g_post)


def _mixa_kernel(x_ref, shift_ref, scale_ref, gpre_ref, win_ref, gv_ref, ws_ref, bs_ref, y_ref, h_sc,
                 *, col_tile, chunk, groups):
    tm = x_ref.shape[0]
    width = win_ref.shape[1] // 2
    tn = col_tile
    n_u = width // tn
    gdim = width // groups
    _norm_mod_rows(x_ref, h_sc, gpre_ref, scale_ref, shift_ref)
    h = h_sc[...]
    vs = [jnp.dot(h, win_ref[:, width + b * tn:width + (b + 1) * tn], preferred_element_type=F32)
          for b in range(n_u)]
    ss = jnp.zeros((tm, 1), F32)
    for vb in vs:
        ss = ss + jnp.sum(vb * vb, axis=-1, keepdims=True)
    inv = lax.rsqrt(ss / width + EPS)
    for b in range(n_u):
        u = jnp.dot(h, win_ref[:, b * tn:(b + 1) * tn], preferred_element_type=F32)
        for off in range(0, tn, gdim):
            col = b * tn + off
            g = col // gdim
            gv = gv_ref[:, col:col + gdim]
            for c in range(tm // chunk):
                r = c * chunk
                v = (vs[b][r:r + chunk, off:off + gdim] * inv[r:r + chunk]) * gv
                s = jnp.dot(ws_ref[g], v.astype(BF16), preferred_element_type=F32) + bs_ref[g]
                y_ref[r:r + chunk, col:col + gdim] = (u[r:r + chunk, off:off + gdim] * s).astype(y_ref.dtype)


def _mixa_call(x, mod_l, g_pre, layer, w_in, g_v, w_s, b_s, a_layer, trunk):
    rows, d = x.shape
    width = w_in.shape[2] // 2
    groups, chunk = w_s.shape[1], w_s.shape[2]
    tm = _row_tile(trunk)
    tn = min(COL_TILE, width)
    gdim = width // groups
    assert width % tn == 0 and tn % gdim == 0 and tm % chunk == 0 and trunk.seq % chunk == 0
    blocks = [((tm, d), F32), ((tm, width), BF16), ((groups, chunk, chunk), BF16), ((groups, chunk, V7X_LANES), F32)]
    scratch = [((tm, d), BF16)]
    resident = [((d, 2 * width), BF16)]
    return pl.pallas_call(
        functools.partial(_mixa_kernel, col_tile=tn, chunk=chunk, groups=groups),
        grid=(rows // tm,),
        in_specs=[
            pl.BlockSpec((tm, d), lambda i: (i, 0)),
            pl.BlockSpec((1, 1, d), _mod_spec(trunk, tm, 0)),
            pl.BlockSpec((1, 1, d), _mod_spec(trunk, tm, 1)),
            pl.BlockSpec((None, 1, d), lambda i: (layer, 0, 0)),
            pl.BlockSpec((None, d, 2 * width), lambda i: (a_layer, 0, 0), pipeline_mode=pl.Buffered(1)),
            pl.BlockSpec((None, 1, width), lambda i: (a_layer, 0, 0)),
            pl.BlockSpec((None, groups, chunk, chunk), lambda i: (a_layer, 0, 0, 0)),
            pl.BlockSpec((None, groups, chunk, 1), lambda i: (a_layer, 0, 0, 0)),
        ],
        out_specs=pl.BlockSpec((tm, width), lambda i: (i, 0)),
        out_shape=jax.ShapeDtypeStruct((rows, width), BF16),
        scratch_shapes=[pltpu.VMEM(s, t) for s, t in scratch],
        compiler_params=_params(("parallel",),
                                _vmem_limit(blocks, scratch + resident, temps=[((tm, 2 * width), F32)])),
        name="mixa",
    )(x, mod_l, mod_l, g_pre, w_in, g_v, w_s, b_s)


def _rope_tables(seq, head_dim):
    axis_dim = head_dim // 2
    t = jnp.arange(seq)
    n_rows = seq // GRID_W
    row = jnp.minimum(t // GRID_W, n_rows - 1).astype(F32)
    col = (t % GRID_W).astype(F32)
    inv = jnp.power(ROPE_THETA, -jnp.arange(0, axis_dim, 2, dtype=F32) / axis_dim)
    ang = jnp.concatenate([row[:, None] * inv, col[:, None] * inv], axis=-1)
    cos, sin = jnp.cos(ang), jnp.sin(ang)
    zero = jnp.zeros_like(sin)
    cos2 = jnp.stack([cos, cos], axis=-1).reshape(seq, head_dim)
    sin_a = jnp.stack([-sin, zero], axis=-1).reshape(seq, head_dim)
    sin_b = jnp.stack([zero, sin], axis=-1).reshape(seq, head_dim)
    return cos2, sin_a, sin_b


def _qkv_kernel(*refs, n_q_blocks, head_dim, latent, q_scale, sub_rows):
    if latent:
        (x_ref, shift_ref, scale_ref, gpre_ref, w_ref, qg_ref, kg_ref, cos_ref, sa_ref, sb_ref,
         q_ref, k_ref, v_ref, h_sc) = refs
    else:
        (x_ref, shift_ref, scale_ref, gpre_ref, w_ref, qg_ref, kg_ref, q_ref, k_ref, v_ref, h_sc) = refs
    tn = k_ref.shape[1]
    n_heads = tn // head_dim
    tm = x_ref.shape[0]

    def head(a, hh, gain, rows):
        blk = _rms(a[:, hh * head_dim:(hh + 1) * head_dim], gain)
        if latent:
            nxt = pltpu.roll(blk, head_dim - 1, 1)
            prv = pltpu.roll(blk, 1, 1)
            blk = blk * cos_ref[rows, :] + nxt * sa_ref[rows, :] + prv * sb_ref[rows, :]
        return blk

    _norm_mod_rows(x_ref, h_sc, gpre_ref, scale_ref, shift_ref, 0, sub_rows)
    for r in range(0, tm, sub_rows):
        if r + sub_rows < tm:
            _norm_mod_rows(x_ref, h_sc, gpre_ref, scale_ref, shift_ref, r + sub_rows, sub_rows, static_rows=True)
        rows = pl.ds(r, sub_rows)
        h = h_sc[rows, :]
        for block in range(n_q_blocks + 2):
            a = jnp.dot(h, w_ref[:, block * tn:(block + 1) * tn], preferred_element_type=F32)
            if block < n_q_blocks:
                for hh in range(n_heads):
                    col = block * tn + hh * head_dim
                    q_ref[rows, col:col + head_dim] = (head(a, hh, qg_ref[...], rows) * q_scale).astype(q_ref.dtype)
            elif block == n_q_blocks:
                for hh in range(n_heads):
                    k_ref[rows, hh * head_dim:(hh + 1) * head_dim] = head(a, hh, kg_ref[...], rows).astype(k_ref.dtype)
            else:
                v_ref[rows, :] = a.astype(v_ref.dtype)


def _qkv_call(x, mod_l, g_pre, layer, w_qkv, q_gain, k_gain, b_layer, n_kv, trunk, kv_dtype):
    rows, d = x.shape
    hd = q_gain.shape[-1]
    tn = n_kv * hd
    cols = w_qkv.shape[2]
    nq = cols - 2 * tn
    assert nq % tn == 0
    nqb = nq // tn
    tm = min(QKV_ROW_TILE, rows)
    sub = min(ROW_TILE, tm)
    assert rows % tm == 0 and trunk.rows_per_cond % tm == 0 and tm % sub == 0
    in_specs = [
        pl.BlockSpec((tm, d), lambda i: (i, 0)),
        pl.BlockSpec((1, 1, d), _mod_spec(trunk, tm, 0)),
        pl.BlockSpec((1, 1, d), _mod_spec(trunk, tm, 1)),
        pl.BlockSpec((None, 1, d), lambda i: (layer, 0, 0)),
        pl.BlockSpec((None, d, cols), lambda i: (b_layer, 0, 0), pipeline_mode=pl.Buffered(1)),
        pl.BlockSpec((None, 1, hd), lambda i: (b_layer, 0, 0)),
        pl.BlockSpec((None, 1, hd), lambda i: (b_layer, 0, 0)),
    ]
    args = [x, mod_l, mod_l, g_pre, w_qkv, q_gain, k_gain]
    if trunk.latent:
        assert trunk.seq % tm == 0 and trunk.seq % GRID_W == 0
        tiles_per_seq = trunk.seq // tm
        in_specs += [pl.BlockSpec((tm, hd), lambda i: (i % tiles_per_seq, 0))] * 3
        args += list(_rope_tables(trunk.seq, hd))
    blocks = [((tm, d), F32), ((tm, nq), BF16), ((tm, tn), kv_dtype), ((tm, tn), kv_dtype),
              ((tm, hd), F32), ((tm, hd), F32), ((tm, hd), F32)]
    scratch = [((tm, d), BF16)]
    resident = [((d, cols), BF16)]
    return pl.pallas_call(
        functools.partial(_qkv_kernel, n_q_blocks=nqb, head_dim=hd, latent=trunk.latent,
                          q_scale=hd ** -0.5 * LOG2_E, sub_rows=sub),
        grid=(rows // tm,),
        in_specs=in_specs,
        out_specs=[
            pl.BlockSpec((tm, nq), lambda i: (i, 0)),
            pl.BlockSpec((tm, tn), lambda i: (i, 0)),
            pl.BlockSpec((tm, tn), lambda i: (i, 0)),
        ],
        out_shape=[
            jax.ShapeDtypeStruct((rows, nq), BF16),
            jax.ShapeDtypeStruct((rows, tn), kv_dtype),
            jax.ShapeDtypeStruct((rows, tn), kv_dtype),
        ],
        scratch_shapes=[pltpu.VMEM(s, t) for s, t in scratch],
        compiler_params=_params(("parallel",),
                                _vmem_limit(blocks, scratch + resident, temps=[((sub, tn), F32)] * (nqb + 2))),
        name="qkv",
    )(*args)


def _attn_kernel(q_ref, k_ref, v_ref, o_ref, *, q_per_kv, head_dim):
    hd = head_dim
    tq, tk = q_ref.shape[1], k_ref.shape[1]
    n_groups = ATTN_HEAD_GROUPS if q_per_kv % ATTN_HEAD_GROUPS == 0 else 1
    per = q_per_kv // n_groups
    k = k_ref[0].astype(BF16)
    v1 = jnp.concatenate([v_ref[0].astype(BF16), jnp.ones((tk, hd), BF16)], axis=1)
    qs = [jnp.concatenate([q_ref[0, :, g * hd:(g + 1) * hd] for g in range(i * per, (i + 1) * per)], axis=0)
          for i in range(n_groups)]
    ss = [lax.dot_general(q, k, (((1,), (1,)), ((), ())), preferred_element_type=F32) for q in qs]
    m_curs = [jnp.max(s, axis=-1, keepdims=True) for s in ss]

    def write(i, acc):
        out = acc[:, :hd] / acc[:, hd:]
        for j in range(per):
            g = i * per + j
            o_ref[0, :, g * hd:(g + 1) * hd] = out[j * tq:(j + 1) * tq].astype(o_ref.dtype)

    ps = [jnp.exp2(s - m).astype(BF16) for s, m in zip(ss, m_curs)]
    accs = [jnp.dot(p, v1, preferred_element_type=F32) for p in ps]
    for i, acc in enumerate(accs):
        write(i, acc)


def _attn_call(q, k, v, n_kv):
    b, t, nq = q.shape
    s = k.shape[1]
    hd = k.shape[2] // n_kv
    gw = nq // n_kv
    tq = min(ATTN_Q_TILE, t)
    assert t % tq == 0 and s <= ATTN_MAX_KEYS and s % V7X_LANES == 0 and hd % V7X_LANES == 0
    rows = (gw // hd) * tq
    blocks = [((1, tq, gw), BF16), ((1, s, hd), k.dtype), ((1, s, hd), v.dtype), ((1, tq, gw), BF16)]
    temps = [((rows, s), F32)] * 2 + [((rows, s), BF16), ((s, 2 * hd), BF16), ((rows, 2 * hd), F32)]
    return pl.pallas_call(
        functools.partial(_attn_kernel, q_per_kv=gw // hd, head_dim=hd),
        grid=(b, n_kv, t // tq),
        in_specs=[
            pl.BlockSpec((1, tq, gw), lambda bi, h, qi: (bi, qi, h)),
            pl.BlockSpec((1, s, hd), lambda bi, h, qi: (bi, 0, h)),
            pl.BlockSpec((1, s, hd), lambda bi, h, qi: (bi, 0, h)),
        ],
        out_specs=pl.BlockSpec((1, tq, gw), lambda bi, h, qi: (bi, qi, h)),
        out_shape=jax.ShapeDtypeStruct((b, t, nq), BF16),
        compiler_params=_params(("parallel", "parallel", "parallel"), _vmem_limit(blocks, temps=temps)),
        name="attn",
    )(q, k, v)


def _pool_kernel(x_ref, xp_ref, xn_ref, shift_ref, scale_ref, gate_ref, gpre_ref, gpost_ref, w_ref, cs_ref,
                 o_ref, h_sc, ta_sc, tb_sc, m_sc, *, tiles_per_seq):
    tm, d = x_ref.shape
    pad = POOL_HALO
    gdim = d // len(POOL_WINDOWS)
    gpre, scale, shift = gpre_ref[...], scale_ref[0], shift_ref[0]
    t_in_seq = lax.rem(pl.program_id(0), tiles_per_seq)
    keep_prev = (t_in_seq > 0).astype(F32)
    keep_next = (t_in_seq < tiles_per_seq - 1).astype(F32)
    zeros = jnp.zeros((pad, d), F32)
    h_sc[0:pad, :] = zeros
    h_sc[pad:2 * pad, :] = _norm_mod(xp_ref[...], gpre, scale, shift) * keep_prev
    h_sc[2 * pad:2 * pad + tm, :] = _norm_mod(x_ref[...], gpre, scale, shift)
    h_sc[2 * pad + tm:3 * pad + tm, :] = _norm_mod(xn_ref[...], gpre, scale, shift) * keep_next
    h_sc[3 * pad + tm:, :] = zeros
    for t_sc in (ta_sc, tb_sc):
        t_sc[0:pad, :] = zeros[:, :gdim]
        t_sc[3 * pad + tm:, :] = zeros[:, :gdim]

    ext = tm + 2 * pad

    def shifted_sum(src, cols, lo, hi):
        return src[pl.ds(pad + lo, ext), cols] + src[pl.ds(pad + hi, ext), cols]

    pos = t_in_seq * tm + lax.broadcasted_iota(jnp.int32, (tm, 1), 0)
    seq = tiles_per_seq * tm
    all_cols = pl.ds(0, gdim)
    for j, w in enumerate(POOL_WINDOWS):
        half = w // 2
        cols = pl.ds(j * gdim, gdim)
        src, src_cols, span = h_sc, cols, 1
        for dst in (ta_sc, tb_sc, ta_sc):
            if span * 2 >= w:
                break
            lo, hi = (-1, 0) if span == 1 else (-(span // 2), span // 2)
            dst[pl.ds(pad, ext), :] = shifted_sum(src, src_cols, lo, hi)
            src, src_cols, span = dst, all_cols, span * 2
        lo, hi = (-1, 0) if span == 1 else (-(span // 2), span // 2)
        win = src[pl.ds(2 * pad + lo, tm), src_cols] + src[pl.ds(2 * pad + hi, tm), src_cols]
        cnt = (jnp.minimum(pos + half, seq) - jnp.maximum(pos - half, 0)).astype(F32)
        p = win * (1.0 / cnt) - h_sc[pl.ds(2 * pad, tm), cols]
        y = jnp.dot(p.astype(BF16), w_ref[j], preferred_element_type=F32)
        m_sc[:, cols] = y * cs_ref[:, cols]
    o_ref[...] = x_ref[...] + gate_ref[0] * _rms(m_sc[...], gpost_ref[...])


def _pool_call(x, mod_l, g_pre, g_post, layer, w_pool, c_scale, c_layer, trunk):
    rows, d = x.shape
    tm = min(_row_tile(trunk), trunk.seq)
    assert trunk.seq % tm == 0 and tm % POOL_HALO == 0 and max(POOL_WINDOWS) // 2 <= POOL_HALO
    groups, gdim = w_pool.shape[1], w_pool.shape[2]
    assert groups == len(POOL_WINDOWS) and POOL_WINDOWS == (2, 4, 8, 16)
    halo_per_tile = tm // POOL_HALO
    last_halo = rows // POOL_HALO - 1
    blocks = [((tm, d), F32), ((tm, d), F32), ((groups, gdim, gdim), BF16)]
    ext_rows = tm + 4 * POOL_HALO
    scratch = [((ext_rows, d), F32), ((ext_rows, gdim), F32), ((ext_rows, gdim), F32), ((tm, d), F32)]
    return pl.pallas_call(
        functools.partial(_pool_kernel, tiles_per_seq=trunk.seq // tm),
        grid=(rows // tm,),
        in_specs=[
            pl.BlockSpec((tm, d), lambda i: (i, 0)),
            pl.BlockSpec((POOL_HALO, d), lambda i: (jnp.maximum(i * halo_per_tile - 1, 0), 0)),
            pl.BlockSpec((POOL_HALO, d), lambda i: (jnp.minimum((i + 1) * halo_per_tile, last_halo), 0)),
            pl.BlockSpec((1, 1, d), _mod_spec(trunk, tm, 0)),
            pl.BlockSpec((1, 1, d), _mod_spec(trunk, tm, 1)),
            pl.BlockSpec((1, 1, d), _mod_spec(trunk, tm, 2)),
            pl.BlockSpec((None, 1, d), lambda i: (layer, 0, 0)),
            pl.BlockSpec((None, 1, d), lambda i: (layer, 0, 0)),
            pl.BlockSpec((None, groups, gdim, gdim), lambda i: (c_layer, 0, 0, 0)),
            pl.BlockSpec((None, 1, d), lambda i: (c_layer, 0, 0)),
        ],
        out_specs=pl.BlockSpec((tm, d), lambda i: (i, 0)),
        out_shape=jax.ShapeDtypeStruct((rows, d), F32),
        scratch_shapes=[pltpu.VMEM(s, t) for s, t in scratch],
        compiler_params=_params(("parallel",), _vmem_limit(blocks, scratch, temps=[((tm, d), F32)] * 2)),
        name="pool",
    )(x, x, x, mod_l, mod_l, mod_l, g_pre, g_post, w_pool, c_scale)


def _trunk(x3, trunk, mod, cache_kv, p):
    b, t, d = x3.shape
    x = x3.reshape(b * t, d)
    depth = p["norm_mix_pre"].shape[0]
    n_kv = p["n_kv"]
    ia = ib = ic = 0
    new_k, new_v = [], []
    for l in range(depth):
        mod_l = mod[l]
        g_pre, g_post = p["norm_mix_pre"], p["norm_mix_post"]
        kind = l % 3
        if kind == 0:
            y = _mixa_call(x, mod_l, g_pre, l, p["a_w_in"], p["a_norm_v"], p["a_w_s"], p["a_b_s"], ia, trunk)
            x = _proj_call(y, p["a_w_out"], ia, x, mod_l, g_post, l, trunk)
            ia += 1
        elif kind == 1:
            kv_dtype = BF16 if trunk.latent else F32
            q, k, v = _qkv_call(x, mod_l, g_pre, l, p["b_w_qkv"], p["b_q_norm"], p["b_k_norm"], ib, n_kv, trunk,
                                kv_dtype)
            k3, v3 = k.reshape(b, t, -1), v.reshape(b, t, -1)
            if trunk.latent:
                ck, cv = cache_kv
                past = ck.shape[2]
                k3 = jnp.concatenate([k3, ck[:, ib].reshape(b, past, -1).astype(BF16)], axis=1)
                v3 = jnp.concatenate([v3, cv[:, ib].reshape(b, past, -1).astype(BF16)], axis=1)
            else:
                new_k.append(k3.reshape(b, t, n_kv, -1))
                new_v.append(v3.reshape(b, t, n_kv, -1))
            o = _attn_call(q.reshape(b, t, -1), k3, v3, n_kv)
            x = _proj_call(o.reshape(b * t, -1), p["b_w_o"], ib, x, mod_l, g_post, l, trunk)
            ib += 1
        else:
            x = _pool_call(x, mod_l, g_pre, g_post, l, p["c_w_pool"], p["c_scale"], ic, trunk)
            ic += 1
        x = _ffn_call(x, mod_l, p["norm_ffn_pre"], p["norm_ffn_post"], p["f_w_gu"], p["f_w_down"], l, trunk)
    return x.reshape(b, t, d), new_k, new_v


def kernel(x_prompt, x_sample, cache_k, cache_v, c, c_ctx, w_mod, b_mod, norm_mix_pre, norm_mix_post, norm_ffn_pre, norm_ffn_post, a_w_in, a_norm_v, a_w_s, a_b_s, a_w_out, b_w_qkv, b_q_norm, b_k_norm, b_w_o, c_w_pool, c_scale, f_w_gu, f_w_down):
    batch, seq, d = x_prompt.shape
    dec_batch, dec_seq, _ = x_sample.shape
    depth = w_mod.shape[0]
    assert 1 + dec_batch <= MOD_ROWS

    cond = jnp.concatenate([c_ctx[None, :], c, jnp.zeros((MOD_ROWS - 1 - dec_batch, d), F32)], axis=0)
    mod = _mod_call(cond, w_mod, b_mod).reshape(depth, MOD_ROWS * N_MOD, 1, d)

    rows3 = lambda a: a.reshape(a.shape[0], 1, a.shape[-1])
    p = {
        "n_kv": cache_k.shape[3],
        "norm_mix_pre": rows3(norm_mix_pre), "norm_mix_post": rows3(norm_mix_post),
        "norm_ffn_pre": rows3(norm_ffn_pre), "norm_ffn_post": rows3(norm_ffn_post),
        "a_w_in": a_w_in.astype(BF16), "a_norm_v": rows3(a_norm_v), "a_w_s": a_w_s.astype(BF16),
        "a_b_s": a_b_s[..., None], "a_w_out": a_w_out.astype(BF16),
        "b_w_qkv": b_w_qkv.astype(BF16), "b_q_norm": rows3(b_q_norm), "b_k_norm": rows3(b_k_norm),
        "b_w_o": b_w_o.astype(BF16),
        "c_w_pool": c_w_pool.astype(BF16), "c_scale": rows3(c_scale),
        "f_w_gu": f_w_gu.astype(BF16), "f_w_down": f_w_down.astype(BF16),
    }
    ctx = Trunk(batch=batch, seq=seq, mod_base=0, rows_per_cond=batch * seq, latent=False)
    lat = Trunk(batch=dec_batch, seq=dec_seq, mod_base=1, rows_per_cond=dec_seq, latent=True)
    y_prompt, new_k, new_v = _trunk(x_prompt, ctx, mod, None, p)
    y_sample, _, _ = _trunk(x_sample, lat, mod, (cache_k, cache_v), p)
    return (y_prompt, y_sample, jnp.stack(new_k, axis=1), jnp.stack(new_v, axis=1))
```

```python
import collections
import functools

import jax
import jax.numpy as jnp
from jax import lax
from jax.experimental import pallas as pl
from jax.experimental.pallas import tpu as pltpu

F32 = jnp.float32
BF16 = jnp.bfloat16

EPS = 1e-6
N_MOD = 6
GRID_W = 64
ROPE_THETA = 10000.0
POOL_WINDOWS = (2, 4, 8, 16)
POOL_HALO = 8
MOD_ROWS = 16
LOG2_E = 1.4426950408889634
V7X_LANES = 128

V7X_VMEM_BUDGET = 60 * 1024 * 1024
ROW_TILE = 512
ROW_GROUP = 16
ROW_GROUP_UNROLL = 8
FFN_HIDDEN_TILE = 512
WIDE_ROW_TILE = 1024
COL_TILE = 512
ATTN_Q_TILE = 256
ATTN_MAX_KEYS = 4608
ATTN_HEAD_GROUPS = 2
MOD_COL_TILE = 1024

Trunk = collections.namedtuple("Trunk", "batch seq mod_base rows_per_cond latent")


def _nbytes(shape, dtype):
    n = 1
    for s in shape:
        n *= s
    return n * jnp.dtype(dtype).itemsize


def _vmem_limit(pipelined, scratch=(), temps=()):
    total = 2 * sum(_nbytes(s, d) for s, d in pipelined)
    total += sum(_nbytes(s, d) for s, d in scratch)
    total += sum(_nbytes(s, d) for s, d in temps)
    return int(min(max(total + (4 << 20), 16 << 20), V7X_VMEM_BUDGET))


def _params(semantics, limit):
    return pltpu.CompilerParams(dimension_semantics=semantics, vmem_limit_bytes=limit)


def _rms(xf, g):
    ms = jnp.mean(xf * xf, axis=-1, keepdims=True)
    return (xf * lax.rsqrt(ms + EPS)) * g


def _norm_mod(xf, g, scale, shift):
    return _rms(xf, g) * (1.0 + scale) + shift


def _row_groups(row0, n_rows, body, static_rows=False):
    assert n_rows % ROW_GROUP == 0 and row0 % ROW_GROUP == 0
    if static_rows:
        for r in range(row0, row0 + n_rows, ROW_GROUP):
            body(pl.ds(r, ROW_GROUP))
        return

    def step(r, carry):
        body(pl.ds(pl.multiple_of(row0 + r * ROW_GROUP, ROW_GROUP), ROW_GROUP))
        return carry
    lax.fori_loop(0, n_rows // ROW_GROUP, step, 0, unroll=ROW_GROUP_UNROLL)


def _norm_mod_rows(x_ref, dst_ref, gpre_ref, scale_ref, shift_ref, row0=0, n_rows=None, static_rows=False):
    d = x_ref.shape[1]
    n_rows = x_ref.shape[0] if n_rows is None else n_rows
    gain = jnp.broadcast_to(gpre_ref[...] * (1.0 + scale_ref[0]), (ROW_GROUP, d))
    shift = jnp.broadcast_to(shift_ref[0], (ROW_GROUP, d))

    def body(rows):
        xr = x_ref[rows, :]
        ms = jnp.mean(xr * xr, axis=-1, keepdims=True)
        h = (xr * lax.rsqrt(ms + EPS)) * gain + shift
        dst_ref[rows, :] = h.astype(dst_ref.dtype)
    _row_groups(row0, n_rows, body, static_rows)


def _gated_residual_rows(f_ref, x_ref, o_ref, gate_ref, gpost_ref, row0=0, n_rows=None):
    d = x_ref.shape[1]
    n_rows = x_ref.shape[0] if n_rows is None else n_rows
    gain = jnp.broadcast_to(gate_ref[0] * gpost_ref[...], (ROW_GROUP, d))

    def body(rows):
        f = f_ref[rows, :]
        ms = jnp.mean(f * f, axis=-1, keepdims=True)
        o_ref[rows, :] = x_ref[rows, :] + (f * lax.rsqrt(ms + EPS)) * gain
    _row_groups(row0, n_rows, body, static_rows=f_ref is o_ref)


def _row_tile(trunk):
    rows = trunk.batch * trunk.seq
    tm = min(ROW_TILE, rows)
    assert rows % tm == 0 and (trunk.rows_per_cond % tm == 0)
    return tm


def _wide_row_tile(trunk):
    rows = trunk.batch * trunk.seq
    tm = min(WIDE_ROW_TILE, rows)
    sub = min(ROW_TILE, tm)
    assert rows % tm == 0 and trunk.rows_per_cond % tm == 0 and tm % sub == 0
    return tm, sub


def _mod_spec(trunk, tm, m):
    def index(i, *_):
        return ((trunk.mod_base + (i * tm) // trunk.rows_per_cond) * N_MOD + m, 0, 0)
    return index


def _mod_kernel(c_ref, w_ref, b_ref, o_ref):
    c = c_ref[...]
    s = (c * jax.nn.sigmoid(c)).astype(BF16)
    o_ref[0] = jnp.dot(s, w_ref[0].astype(BF16), preferred_element_type=F32) + b_ref[0]


def _mod_call(cond, w_mod, b_mod):
    depth, d, n = w_mod.shape
    tn = min(MOD_COL_TILE, n)
    assert n % tn == 0
    blocks = [((MOD_ROWS, d), F32), ((1, d, tn), F32), ((1, 1, tn), F32), ((1, MOD_ROWS, tn), F32)]
    return pl.pallas_call(
        _mod_kernel,
        grid=(depth, n // tn),
        in_specs=[
            pl.BlockSpec((MOD_ROWS, d), lambda l, j: (0, 0)),
            pl.BlockSpec((1, d, tn), lambda l, j: (l, 0, j)),
            pl.BlockSpec((1, 1, tn), lambda l, j: (l, 0, j)),
        ],
        out_specs=pl.BlockSpec((1, MOD_ROWS, tn), lambda l, j: (l, 0, j)),
        out_shape=jax.ShapeDtypeStruct((depth, MOD_ROWS, n), F32),
        compiler_params=_params(("parallel", "parallel"), _vmem_limit(blocks, temps=[((d, tn), BF16)])),
        name="mod",
    )(cond, w_mod, b_mod.reshape(depth, 1, n))


def _ffn_kernel(x_ref, shift_ref, scale_ref, gate_ref, gpre_ref, gpost_ref, wg_ref, wu_ref, wd_ref,
                o_ref, h_sc, *, n_chunks, sub_rows):
    c = pl.program_id(1)
    tm = x_ref.shape[0]
    parts = list(range(0, tm, sub_rows))

    def part(r):
        h = h_sc[r:r + sub_rows, :]
        g = jnp.dot(h, wg_ref[...], preferred_element_type=F32)
        u = jnp.dot(h, wu_ref[...], preferred_element_type=F32)
        a = (g * jax.nn.sigmoid(g)) * u
        return jnp.dot(a.astype(BF16), wd_ref[...], preferred_element_type=F32)

    @pl.when(c == 0)
    def _():
        _norm_mod_rows(x_ref, h_sc, gpre_ref, scale_ref, shift_ref, 0, sub_rows)
        for r in parts:
            if r + sub_rows < tm:
                _norm_mod_rows(x_ref, h_sc, gpre_ref, scale_ref, shift_ref, r + sub_rows, sub_rows, static_rows=True)
            o_ref[r:r + sub_rows, :] = part(r)

    @pl.when(jnp.logical_and(c > 0, c < n_chunks - 1))
    def _():
        for r in parts:
            o_ref[r:r + sub_rows, :] += part(r)

    @pl.when(c == n_chunks - 1)
    def _():
        for r in parts:
            o_ref[r:r + sub_rows, :] += part(r)
            _gated_residual_rows(o_ref, x_ref, o_ref, gate_ref, gpost_ref, r, sub_rows)


def _ffn_call(x, mod_l, g_pre, g_post, w_gu, w_down, layer, trunk):
    rows, d = x.shape
    hidden = w_down.shape[1]
    tm, sub = _wide_row_tile(trunk)
    th = min(FFN_HIDDEN_TILE, hidden)
    nc = hidden // th
    assert hidden % th == 0 and nc >= 2
    blocks = [((tm, d), F32)] * 2 + [((d, th), BF16), ((d, th), BF16), ((th, d), BF16)]
    scratch = [((tm, d), BF16)]
    temps = [((sub, th), F32)] * 4 + [((sub, d), F32)]
    return pl.pallas_call(
        functools.partial(_ffn_kernel, n_chunks=nc, sub_rows=sub),
        grid=(rows // tm, nc),
        in_specs=[
            pl.BlockSpec((tm, d), lambda i, c: (i, 0)),
            pl.BlockSpec((1, 1, d), _mod_spec(trunk, tm, 3)),
            pl.BlockSpec((1, 1, d), _mod_spec(trunk, tm, 4)),
            pl.BlockSpec((1, 1, d), _mod_spec(trunk, tm, 5)),
            pl.BlockSpec((None, 1, d), lambda i, c: (layer, 0, 0)),
            pl.BlockSpec((None, 1, d), lambda i, c: (layer, 0, 0)),
            pl.BlockSpec((None, d, th), lambda i, c: (layer, 0, c)),
            pl.BlockSpec((None, d, th), lambda i, c: (layer, 0, nc + c)),
            pl.BlockSpec((None, th, d), lambda i, c: (layer, c, 0)),
        ],
        out_specs=pl.BlockSpec((tm, d), lambda i, c: (i, 0)),
        out_shape=jax.ShapeDtypeStruct((rows, d), F32),
        scratch_shapes=[pltpu.VMEM(s, t) for s, t in scratch],
        compiler_params=_params(("parallel", "arbitrary"), _vmem_limit(blocks, scratch, temps)),
        name="ffn",
    )(x, mod_l, mod_l, mod_l, g_pre, g_post, w_gu, w_gu, w_down)


def _proj_kernel(y_ref, w_ref, x_ref, gate_ref, gpost_ref, o_ref, *, sub_rows):
    for r in range(0, x_ref.shape[0], sub_rows):
        o_ref[r:r + sub_rows, :] = jnp.dot(y_ref[r:r + sub_rows, :], w_ref[...], preferred_element_type=F32)
        _gated_residual_rows(o_ref, x_ref, o_ref, gate_ref, gpost_ref, r, sub_rows)


def _proj_call(y, w, w_layer, x, mod_l, g_post, layer, trunk):
    rows, d = x.shape
    k = y.shape[1]
    tm, sub = _wide_row_tile(trunk)
    blocks = [((tm, k), BF16), ((tm, d), F32), ((tm, d), F32)]
    return pl.pallas_call(
        functools.partial(_proj_kernel, sub_rows=sub),
        grid=(rows // tm,),
        in_specs=[
            pl.BlockSpec((tm, k), lambda i: (i, 0)),
            pl.BlockSpec((None, k, d), lambda i: (w_layer, 0, 0), pipeline_mode=pl.Buffered(1)),
            pl.BlockSpec((tm, d), lambda i: (i, 0)),
            pl.BlockSpec((1, 1, d), _mod_spec(trunk, tm, 2)),
            pl.BlockSpec((None, 1, d), lambda i: (layer, 0, 0)),
        ],
        out_specs=pl.BlockSpec((tm, d), lambda i: (i, 0)),
        out_shape=jax.ShapeDtypeStruct((rows, d), F32),
        compiler_params=_params(("parallel",), _vmem_limit(blocks, scratch=[((k, d), BF16)],
                                                           temps=[((sub, d), F32)])),
        name="proj",
    )(y, w, x, mod_l, g_post)


def _mixa_kernel(x_ref, shift_ref, scale_ref, gpre_ref, win_ref, gv_ref, ws_ref, bs_ref, y_ref, h_sc,
                 *, col_tile, chunk, groups, sub_rows):
    tm = x_ref.shape[0]
    width = win_ref.shape[1] // 2
    tn = col_tile
    n_u = width // tn
    gdim = width // groups
    _norm_mod_rows(x_ref, h_sc, gpre_ref, scale_ref, shift_ref, 0, sub_rows)
    for r0 in range(0, tm, sub_rows):
        if r0 + sub_rows < tm:
            _norm_mod_rows(x_ref, h_sc, gpre_ref, scale_ref, shift_ref, r0 + sub_rows, sub_rows, static_rows=True)
        h = h_sc[r0:r0 + sub_rows, :]
        vs = [jnp.dot(h, win_ref[:, width + b * tn:width + (b + 1) * tn], preferred_element_type=F32)
              for b in range(n_u)]
        ss = jnp.zeros((sub_rows, 1), F32)
        for vb in vs:
            ss = ss + jnp.sum(vb * vb, axis=-1, keepdims=True)
        inv = lax.rsqrt(ss / width + EPS)
        for b in range(n_u):
            u = jnp.dot(h, win_ref[:, b * tn:(b + 1) * tn], preferred_element_type=F32)
            for off in range(0, tn, gdim):
                col = b * tn + off
                g = col // gdim
                gv = gv_ref[:, col:col + gdim]
                for c in range(sub_rows // chunk):
                    r = c * chunk
                    v = (vs[b][r:r + chunk, off:off + gdim] * inv[r:r + chunk]) * gv
                    s = jnp.dot(ws_ref[g], v.astype(BF16), preferred_element_type=F32) + bs_ref[g]
                    y_ref[r0 + r:r0 + r + chunk, col:col + gdim] = (
                        u[r:r + chunk, off:off + gdim] * s).astype(y_ref.dtype)


def _mixa_call(x, mod_l, g_pre, layer, w_in, g_v, w_s, b_s, a_layer, trunk):
    rows, d = x.shape
    width = w_in.shape[2] // 2
    groups, chunk = w_s.shape[1], w_s.shape[2]
    tm, sub = _wide_row_tile(trunk)
    tn = min(COL_TILE, width)
    gdim = width // groups
    assert width % tn == 0 and tn % gdim == 0 and sub % chunk == 0 and trunk.seq % chunk == 0
    blocks = [((tm, d), F32), ((tm, width), BF16), ((groups, chunk, chunk), BF16), ((groups, chunk, V7X_LANES), F32)]
    scratch = [((tm, d), BF16)]
    resident = [((d, 2 * width), BF16)]
    return pl.pallas_call(
        functools.partial(_mixa_kernel, col_tile=tn, chunk=chunk, groups=groups, sub_rows=sub),
        grid=(rows // tm,),
        in_specs=[
            pl.BlockSpec((tm, d), lambda i: (i, 0)),
            pl.BlockSpec((1, 1, d), _mod_spec(trunk, tm, 0)),
            pl.BlockSpec((1, 1, d), _mod_spec(trunk, tm, 1)),
            pl.BlockSpec((None, 1, d), lambda i: (layer, 0, 0)),
            pl.BlockSpec((None, d, 2 * width), lambda i: (a_layer, 0, 0), pipeline_mode=pl.Buffered(1)),
            pl.BlockSpec((None, 1, width), lambda i: (a_layer, 0, 0)),
            pl.BlockSpec((None, groups, chunk, chunk), lambda i: (a_layer, 0, 0, 0)),
            pl.BlockSpec((None, groups, chunk, 1), lambda i: (a_layer, 0, 0, 0)),
        ],
        out_specs=pl.BlockSpec((tm, width), lambda i: (i, 0)),
        out_shape=jax.ShapeDtypeStruct((rows, width), BF16),
        scratch_shapes=[pltpu.VMEM(s, t) for s, t in scratch],
        compiler_params=_params(("parallel",),
                                _vmem_limit(blocks, scratch + resident, temps=[((sub, 2 * width), F32), ((sub, width), F32)])),
        name="mixa",
    )(x, mod_l, mod_l, g_pre, w_in, g_v, w_s, b_s)


def _rope_tables(seq, head_dim):
    axis_dim = head_dim // 2
    t = jnp.arange(seq)
    n_rows = seq // GRID_W
    row = jnp.minimum(t // GRID_W, n_rows - 1).astype(F32)
    col = (t % GRID_W).astype(F32)
    inv = jnp.power(ROPE_THETA, -jnp.arange(0, axis_dim, 2, dtype=F32) / axis_dim)
    ang = jnp.concatenate([row[:, None] * inv, col[:, None] * inv], axis=-1)
    cos, sin = jnp.cos(ang), jnp.sin(ang)
    return jnp.concatenate([cos, cos], axis=-1), jnp.concatenate([-sin, sin], axis=-1)


def _even_first(a, n_heads, head_dim):
    perm = jnp.concatenate([jnp.arange(0, head_dim, 2), jnp.arange(1, head_dim, 2)])
    shape = a.shape
    return a.reshape(shape[:-1] + (n_heads, head_dim))[..., perm].reshape(shape)


def _qkv_kernel(*refs, n_q_blocks, head_dim, latent, q_scale, sub_rows):
    if latent:
        (x_ref, shift_ref, scale_ref, gpre_ref, w_ref, qg_ref, kg_ref, cos_ref, sin_ref,
         q_ref, k_ref, v_ref, h_sc) = refs
    else:
        (x_ref, shift_ref, scale_ref, gpre_ref, w_ref, qg_ref, kg_ref, q_ref, k_ref, v_ref, h_sc) = refs
    tn = k_ref.shape[1]
    n_heads = tn // head_dim
    tm = x_ref.shape[0]

    def head(a, hh, gain, rows):
        blk = _rms(a[:, hh * head_dim:(hh + 1) * head_dim], gain)
        if latent:
            blk = blk * cos_ref[rows, :] + pltpu.roll(blk, head_dim // 2, 1) * sin_ref[rows, :]
        return blk

    _norm_mod_rows(x_ref, h_sc, gpre_ref, scale_ref, shift_ref, 0, sub_rows)
    for r in range(0, tm, sub_rows):
        if r + sub_rows < tm:
            _norm_mod_rows(x_ref, h_sc, gpre_ref, scale_ref, shift_ref, r + sub_rows, sub_rows, static_rows=True)
        rows = pl.ds(r, sub_rows)
        h = h_sc[rows, :]
        for block in range(n_q_blocks + 2):
            a = jnp.dot(h, w_ref[:, block * tn:(block + 1) * tn], preferred_element_type=F32)
            if block < n_q_blocks:
                for hh in range(n_heads):
                    col = block * tn + hh * head_dim
                    q_ref[rows, col:col + head_dim] = (head(a, hh, qg_ref[...], rows) * q_scale).astype(q_ref.dtype)
            elif block == n_q_blocks:
                for hh in range(n_heads):
                    k_ref[rows, hh * head_dim:(hh + 1) * head_dim] = head(a, hh, kg_ref[...], rows).astype(k_ref.dtype)
            else:
                v_ref[rows, :] = a.astype(v_ref.dtype)


def _qkv_call(x, mod_l, g_pre, layer, w_qkv, q_gain, k_gain, b_layer, n_kv, trunk, kv_dtype):
    rows, d = x.shape
    hd = q_gain.shape[-1]
    tn = n_kv * hd
    cols = w_qkv.shape[2]
    nq = cols - 2 * tn
    assert nq % tn == 0
    nqb = nq // tn
    tm, sub = _wide_row_tile(trunk)
    in_specs = [
        pl.BlockSpec((tm, d), lambda i: (i, 0)),
        pl.BlockSpec((1, 1, d), _mod_spec(trunk, tm, 0)),
        pl.BlockSpec((1, 1, d), _mod_spec(trunk, tm, 1)),
        pl.BlockSpec((None, 1, d), lambda i: (layer, 0, 0)),
        pl.BlockSpec((None, d, cols), lambda i: (b_layer, 0, 0), pipeline_mode=pl.Buffered(1)),
        pl.BlockSpec((None, 1, hd), lambda i: (b_layer, 0, 0)),
        pl.BlockSpec((None, 1, hd), lambda i: (b_layer, 0, 0)),
    ]
    args = [x, mod_l, mod_l, g_pre, w_qkv, q_gain, k_gain]
    if trunk.latent:
        assert trunk.seq % tm == 0 and trunk.seq % GRID_W == 0
        tiles_per_seq = trunk.seq // tm
        in_specs += [pl.BlockSpec((tm, hd), lambda i: (i % tiles_per_seq, 0))] * 2
        args += list(_rope_tables(trunk.seq, hd))
    blocks = [((tm, d), F32), ((tm, nq), BF16), ((tm, tn), kv_dtype), ((tm, tn), kv_dtype),
              ((tm, hd), F32), ((tm, hd), F32), ((tm, hd), F32)]
    scratch = [((tm, d), BF16)]
    resident = [((d, cols), BF16)]
    return pl.pallas_call(
        functools.partial(_qkv_kernel, n_q_blocks=nqb, head_dim=hd, latent=trunk.latent,
                          q_scale=hd ** -0.5 * LOG2_E, sub_rows=sub),
        grid=(rows // tm,),
        in_specs=in_specs,
        out_specs=[
            pl.BlockSpec((tm, nq), lambda i: (i, 0)),
            pl.BlockSpec((tm, tn), lambda i: (i, 0)),
            pl.BlockSpec((tm, tn), lambda i: (i, 0)),
        ],
        out_shape=[
            jax.ShapeDtypeStruct((rows, nq), BF16),
            jax.ShapeDtypeStruct((rows, tn), kv_dtype),
            jax.ShapeDtypeStruct((rows, tn), kv_dtype),
        ],
        scratch_shapes=[pltpu.VMEM(s, t) for s, t in scratch],
        compiler_params=_params(("parallel",),
                                _vmem_limit(blocks, scratch + resident, temps=[((sub, tn), F32)] * (nqb + 2))),
        name="qkv",
    )(*args)


def _attn_kernel(q_ref, k_ref, v_ref, o_ref, *, q_per_kv, head_dim):
    hd = head_dim
    tq, tk = q_ref.shape[1], k_ref.shape[1]
    n_groups = ATTN_HEAD_GROUPS if q_per_kv % ATTN_HEAD_GROUPS == 0 else 1
    per = q_per_kv // n_groups
    k = k_ref[0].astype(BF16)
    v1 = jnp.concatenate([v_ref[0].astype(BF16), jnp.ones((tk, hd), BF16)], axis=1)
    qs = [jnp.concatenate([q_ref[0, :, g * hd:(g + 1) * hd] for g in range(i * per, (i + 1) * per)], axis=0)
          for i in range(n_groups)]
    ss = [lax.dot_general(q, k, (((1,), (1,)), ((), ())), preferred_element_type=F32) for q in qs]
    m_curs = [jnp.max(s, axis=-1, keepdims=True) for s in ss]

    def write(i, acc):
        out = acc[:, :hd] / acc[:, hd:]
        for j in range(per):
            g = i * per + j
            o_ref[0, :, g * hd:(g + 1) * hd] = out[j * tq:(j + 1) * tq].astype(o_ref.dtype)

    ps = [jnp.exp2(s - m).astype(BF16) for s, m in zip(ss, m_curs)]
    accs = [jnp.dot(p, v1, preferred_element_type=F32) for p in ps]
    for i, acc in enumerate(accs):
        write(i, acc)


def _attn_call(q, k, v, n_kv):
    b, t, nq = q.shape
    s = k.shape[1]
    hd = k.shape[2] // n_kv
    gw = nq // n_kv
    tq = min(ATTN_Q_TILE, t)
    assert t % tq == 0 and s <= ATTN_MAX_KEYS and s % V7X_LANES == 0 and hd % V7X_LANES == 0
    rows = (gw // hd) * tq
    blocks = [((1, tq, gw), BF16), ((1, s, hd), k.dtype), ((1, s, hd), v.dtype), ((1, tq, gw), BF16)]
    temps = [((rows, s), F32)] * 2 + [((rows, s), BF16), ((s, 2 * hd), BF16), ((rows, 2 * hd), F32)]
    return pl.pallas_call(
        functools.partial(_attn_kernel, q_per_kv=gw // hd, head_dim=hd),
        grid=(b, n_kv, t // tq),
        in_specs=[
            pl.BlockSpec((1, tq, gw), lambda bi, h, qi: (bi, qi, h)),
            pl.BlockSpec((1, s, hd), lambda bi, h, qi: (bi, 0, h)),
            pl.BlockSpec((1, s, hd), lambda bi, h, qi: (bi, 0, h)),
        ],
        out_specs=pl.BlockSpec((1, tq, gw), lambda bi, h, qi: (bi, qi, h)),
        out_shape=jax.ShapeDtypeStruct((b, t, nq), BF16),
        compiler_params=_params(("parallel", "parallel", "parallel"), _vmem_limit(blocks, temps=temps)),
        name="attn",
    )(q, k, v)


def _pool_kernel(x_ref, xp_ref, xn_ref, shift_ref, scale_ref, gate_ref, gpre_ref, gpost_ref, w_ref, cs_ref,
                 o_ref, h_sc, ta_sc, tb_sc, m_sc, *, tiles_per_seq):
    tm, d = x_ref.shape
    pad = POOL_HALO
    gdim = d // len(POOL_WINDOWS)
    gpre, scale, shift = gpre_ref[...], scale_ref[0], shift_ref[0]
    t_in_seq = lax.rem(pl.program_id(0), tiles_per_seq)
    keep_prev = (t_in_seq > 0).astype(F32)
    keep_next = (t_in_seq < tiles_per_seq - 1).astype(F32)
    zeros = jnp.zeros((pad, d), F32)
    h_sc[0:pad, :] = zeros
    h_sc[pad:2 * pad, :] = _norm_mod(xp_ref[...], gpre, scale, shift) * keep_prev
    h_sc[2 * pad:2 * pad + tm, :] = _norm_mod(x_ref[...], gpre, scale, shift)
    h_sc[2 * pad + tm:3 * pad + tm, :] = _norm_mod(xn_ref[...], gpre, scale, shift) * keep_next
    h_sc[3 * pad + tm:, :] = zeros
    for t_sc in (ta_sc, tb_sc):
        t_sc[0:pad, :] = zeros[:, :gdim]
        t_sc[3 * pad + tm:, :] = zeros[:, :gdim]

    ext = tm + 2 * pad
    pos = t_in_seq * tm + lax.broadcasted_iota(jnp.int32, (tm, 1), 0)
    seq = tiles_per_seq * tm
    all_cols = pl.ds(0, gdim)
    for j, w in enumerate(POOL_WINDOWS):
        half = w // 2
        cols = pl.ds(j * gdim, gdim)
        src, src_cols, span = h_sc, cols, 1
        for dst in (ta_sc, tb_sc, ta_sc):
            if span * 2 >= w:
                break
            dst[pl.ds(pad, ext), :] = src[pl.ds(pad, ext), src_cols] + src[pl.ds(pad + span, ext), src_cols]
            src, src_cols, span = dst, all_cols, span * 2
        win = src[pl.ds(2 * pad - half, tm), src_cols] + src[pl.ds(2 * pad, tm), src_cols]
        cnt = (jnp.minimum(pos + half, seq) - jnp.maximum(pos - half, 0)).astype(F32)
        p = win * (1.0 / cnt) - h_sc[pl.ds(2 * pad, tm), cols]
        y = jnp.dot(p.astype(BF16), w_ref[j], preferred_element_type=F32)
        m_sc[:, cols] = y * cs_ref[:, cols]
    o_ref[...] = x_ref[...] + gate_ref[0] * _rms(m_sc[...], gpost_ref[...])


def _pool_call(x, mod_l, g_pre, g_post, layer, w_pool, c_scale, c_layer, trunk):
    rows, d = x.shape
    tm = min(_row_tile(trunk), trunk.seq)
    assert trunk.seq % tm == 0 and tm % POOL_HALO == 0 and max(POOL_WINDOWS) // 2 <= POOL_HALO
    groups, gdim = w_pool.shape[1], w_pool.shape[2]
    assert groups == len(POOL_WINDOWS) and POOL_WINDOWS == (2, 4, 8, 16)
    halo_per_tile = tm // POOL_HALO
    last_halo = rows // POOL_HALO - 1
    blocks = [((tm, d), F32), ((tm, d), F32), ((groups, gdim, gdim), BF16)]
    ext_rows = tm + 4 * POOL_HALO
    scratch = [((ext_rows, d), F32), ((ext_rows, gdim), F32), ((ext_rows, gdim), F32), ((tm, d), F32)]
    return pl.pallas_call(
        functools.partial(_pool_kernel, tiles_per_seq=trunk.seq // tm),
        grid=(rows // tm,),
        in_specs=[
            pl.BlockSpec((tm, d), lambda i: (i, 0)),
            pl.BlockSpec((POOL_HALO, d), lambda i: (jnp.maximum(i * halo_per_tile - 1, 0), 0)),
            pl.BlockSpec((POOL_HALO, d), lambda i: (jnp.minimum((i + 1) * halo_per_tile, last_halo), 0)),
            pl.BlockSpec((1, 1, d), _mod_spec(trunk, tm, 0)),
            pl.BlockSpec((1, 1, d), _mod_spec(trunk, tm, 1)),
            pl.BlockSpec((1, 1, d), _mod_spec(trunk, tm, 2)),
            pl.BlockSpec((None, 1, d), lambda i: (layer, 0, 0)),
            pl.BlockSpec((None, 1, d), lambda i: (layer, 0, 0)),
            pl.BlockSpec((None, groups, gdim, gdim), lambda i: (c_layer, 0, 0, 0)),
            pl.BlockSpec((None, 1, d), lambda i: (c_layer, 0, 0)),
        ],
        out_specs=pl.BlockSpec((tm, d), lambda i: (i, 0)),
        out_shape=jax.ShapeDtypeStruct((rows, d), F32),
        scratch_shapes=[pltpu.VMEM(s, t) for s, t in scratch],
        compiler_params=_params(("parallel",), _vmem_limit(blocks, scratch, temps=[((tm, d), F32)] * 2)),
        name="pool",
    )(x, x, x, mod_l, mod_l, mod_l, g_pre, g_post, w_pool, c_scale)


def _trunk(x3, trunk, mod, cache_kv, p):
    b, t, d = x3.shape
    x = x3.reshape(b * t, d)
    depth = p["norm_mix_pre"].shape[0]
    n_kv = p["n_kv"]
    ia = ib = ic = 0
    new_k, new_v = [], []
    for l in range(depth):
        mod_l = mod[l]
        g_pre, g_post = p["norm_mix_pre"], p["norm_mix_post"]
        kind = l % 3
        if kind == 0:
            y = _mixa_call(x, mod_l, g_pre, l, p["a_w_in"], p["a_norm_v"], p["a_w_s"], p["a_b_s"], ia, trunk)
            x = _proj_call(y, p["a_w_out"], ia, x, mod_l, g_post, l, trunk)
            ia += 1
        elif kind == 1:
            kv_dtype = BF16 if trunk.latent else F32
            sfx = "_rot" if trunk.latent else ""
            q, k, v = _qkv_call(x, mod_l, g_pre, l, p["b_w_qkv" + sfx], p["b_q_norm" + sfx], p["b_k_norm" + sfx], ib,
                                n_kv, trunk, kv_dtype)
            k3, v3 = k.reshape(b, t, -1), v.reshape(b, t, -1)
            if trunk.latent:
                ck, cv = cache_kv
                past = ck.shape[2]
                ck_rot = _even_first(ck[:, ib].reshape(b, past, -1), n_kv, ck.shape[-1])
                k3 = jnp.concatenate([k3, ck_rot.astype(BF16)], axis=1)
                v3 = jnp.concatenate([v3, cv[:, ib].reshape(b, past, -1).astype(BF16)], axis=1)
            else:
                new_k.append(k3.reshape(b, t, n_kv, -1))
                new_v.append(v3.reshape(b, t, n_kv, -1))
            o = _attn_call(q.reshape(b, t, -1), k3, v3, n_kv)
            x = _proj_call(o.reshape(b * t, -1), p["b_w_o"], ib, x, mod_l, g_post, l, trunk)
            ib += 1
        else:
            x = _pool_call(x, mod_l, g_pre, g_post, l, p["c_w_pool"], p["c_scale"], ic, trunk)
            ic += 1
        x = _ffn_call(x, mod_l, p["norm_ffn_pre"], p["norm_ffn_post"], p["f_w_gu"], p["f_w_down"], l, trunk)
    return x.reshape(b, t, d), new_k, new_v


def kernel(x_prompt, x_sample, cache_k, cache_v, c, c_ctx, w_mod, b_mod, norm_mix_pre, norm_mix_post, norm_ffn_pre, norm_ffn_post, a_w_in, a_norm_v, a_w_s, a_b_s, a_w_out, b_w_qkv, b_q_norm, b_k_norm, b_w_o, c_w_pool, c_scale, f_w_gu, f_w_down):
    batch, seq, d = x_prompt.shape
    dec_batch, dec_seq, _ = x_sample.shape
    depth = w_mod.shape[0]
    assert 1 + dec_batch <= MOD_ROWS

    cond = jnp.concatenate([c_ctx[None, :], c, jnp.zeros((MOD_ROWS - 1 - dec_batch, d), F32)], axis=0)
    mod = _mod_call(cond, w_mod, b_mod).reshape(depth, MOD_ROWS * N_MOD, 1, d)

    rows3 = lambda a: a.reshape(a.shape[0], 1, a.shape[-1])
    p = {
        "n_kv": cache_k.shape[3],
        "norm_mix_pre": rows3(norm_mix_pre), "norm_mix_post": rows3(norm_mix_post),
        "norm_ffn_pre": rows3(norm_ffn_pre), "norm_ffn_post": rows3(norm_ffn_post),
        "a_w_in": a_w_in.astype(BF16), "a_norm_v": rows3(a_norm_v), "a_w_s": a_w_s.astype(BF16),
        "a_b_s": a_b_s[..., None], "a_w_out": a_w_out.astype(BF16),
        "b_w_qkv": b_w_qkv.astype(BF16), "b_q_norm": rows3(b_q_norm), "b_k_norm": rows3(b_k_norm),
        "b_w_o": b_w_o.astype(BF16),
        "c_w_pool": c_w_pool.astype(BF16), "c_scale": rows3(c_scale),
        "f_w_gu": f_w_gu.astype(BF16), "f_w_down": f_w_down.astype(BF16),
    }
    n_kv, hd = cache_k.shape[3], cache_k.shape[4]
    kv_cols = n_kv * hd
    q_cols = b_w_qkv.shape[2] - 2 * kv_cols
    w_b = p["b_w_qkv"]
    p["b_w_qkv_rot"] = jnp.concatenate([
        _even_first(w_b[..., :q_cols], q_cols // hd, hd),
        _even_first(w_b[..., q_cols:q_cols + kv_cols], n_kv, hd),
        w_b[..., q_cols + kv_cols:]], axis=-1)
    p["b_q_norm_rot"] = _even_first(p["b_q_norm"], 1, hd)
    p["b_k_norm_rot"] = _even_first(p["b_k_norm"], 1, hd)
    ctx = Trunk(batch=batch, seq=seq, mod_base=0, rows_per_cond=batch * seq, latent=False)
    lat = Trunk(batch=dec_batch, seq=dec_seq, mod_base=1, rows_per_cond=dec_seq, latent=True)
    y_prompt, new_k, new_v = _trunk(x_prompt, ctx, mod, None, p)
    y_sample, _, _ = _trunk(x_sample, lat, mod, (cache_k, cache_v), p)
    return (y_prompt, y_sample, jnp.stack(new_k, axis=1), jnp.stack(new_v, axis=1))
```
